```python
import jax
import jax.numpy as jnp
from jax import lax
import numpy as np

D_MODEL = 1024
BATCH = 8
SEQ = 4096
DEPTH = 4

CTX_LEN = 256
GRID_W = 64
EPS = 1e-6
ROPE_THETA = 10000.0

GLA_HEADS = 4
GLA_V = 3 * D_MODEL // 8
GLA_DV = GLA_V // GLA_HEADS
GLA_DK = GLA_DV // 2
GLA_QK = GLA_HEADS * GLA_DK
GLA_GATE_RANK = 16
GLA_GATE_TAU = 16.0
GLA_CHUNK = 16

NA_HEADS = 6
NA_W = 3 * D_MODEL // 8
NA_DH = NA_W // NA_HEADS
NA_KH_MAX = 8
NA_KW = 16

POOL_WINDOWS = (2, 4, 8, 16)
POOL_WIDTH = D_MODEL // 4
POOL_GROUP = POOL_WIDTH // len(POOL_WINDOWS)

MIX_WIDTH = GLA_V + NA_W + POOL_WIDTH
IN_SPLITS = (GLA_QK, GLA_QK, GLA_V, GLA_GATE_RANK, GLA_GATE_RANK, GLA_V, NA_W, NA_W, NA_W, POOL_WIDTH)
IN_WIDTH = sum(IN_SPLITS)

N_EXPERTS = 32
TOP_K = 4
D_FF = D_MODEL
SWIGLU_LIMIT = 7.0
SWIGLU_ALPHA = 1.702
MOE_BLOCK = 128

kernel_name = 'hybrid_gla_natten_pool_moe_diffusion_block'


def rmsnorm(x, g):
    xf = x.astype(jnp.float32)
    y = xf * lax.rsqrt(jnp.mean(xf * xf, axis=-1, keepdims=True) + EPS) * g.astype(jnp.float32)
    return y.astype(x.dtype)


def modulate(h, shift, scale):
    return h * (1 + scale) + shift


def split_in(p):
    return jnp.split(p, np.cumsum(IN_SPLITS)[:-1].tolist(), axis=-1)


def axial_rope(x, row, col):
    dt = x.dtype
    xf = x.astype(jnp.float32)
    half = x.shape[-1] // 2

    def rotate(xp, pos):
        d = xp.shape[-1]
        inv_freq = ROPE_THETA ** (-jnp.arange(0, d, 2, dtype=jnp.float32) / d)
        ang = pos.astype(jnp.float32)[:, None] * inv_freq[None]
        cos = jnp.cos(ang)[None, :, None]
        sin = jnp.sin(ang)[None, :, None]
        x1, x2 = jnp.split(xp, 2, axis=-1)
        return jnp.concatenate([x1 * cos - x2 * sin, x1 * sin + x2 * cos], axis=-1)

    return jnp.concatenate([rotate(xf[..., :half], row), rotate(xf[..., half:], col)], axis=-1).astype(dt)


def gla_chunked(q, k, v, log_a, s0, with_output=True):
    B, H, T, dk = q.shape
    dv = v.shape[-1]
    n = T // GLA_CHUNK
    f32 = jnp.float32
    qc = q.reshape(B, H, n, GLA_CHUNK, dk).astype(f32)
    kc = k.reshape(B, H, n, GLA_CHUNK, dk).astype(f32)
    vc = v.reshape(B, H, n, GLA_CHUNK, dv).astype(f32)
    b = jnp.cumsum(log_a.reshape(B, H, n, GLA_CHUNK, dk).astype(f32), axis=3)
    b_last = b[:, :, :, -1]
    k_to_end = kc * jnp.exp(b_last[:, :, :, None] - b)
    ds = jnp.einsum('bhncd,bhncv->bhndv', k_to_end, vc)

    def step(s, inp):
        ds_n, bl_n = inp
        return s * jnp.exp(bl_n)[..., None] + ds_n, s

    s_fin, s_start = lax.scan(step, s0, (jnp.moveaxis(ds, 2, 0), jnp.moveaxis(b_last, 2, 0)))
    if not with_output:
        return None, s_fin
    s_start = jnp.moveaxis(s_start, 0, 2)
    o_inter = jnp.einsum('bhncd,bhndv->bhncv', qc * jnp.exp(b), s_start)
    incl = jnp.tril(jnp.ones((GLA_CHUNK, GLA_CHUNK), bool))
    rel = b[:, :, :, :, None, :] - b[:, :, :, None, :, :]
    decay = jnp.where(incl[:, :, None], jnp.exp(jnp.minimum(rel, 0.0)), 0.0)
    scores = jnp.einsum('bhnid,bhnjd,bhnijd->bhnij', qc, kc, decay)
    o_intra = jnp.einsum('bhnij,bhnjv->bhniv', scores, vc)
    o = (o_intra + o_inter).reshape(B, H, T, dv)
    return o.astype(v.dtype), s_fin


def gla_output(o, g, norm_g):
    o = jnp.swapaxes(o, 1, 2)
    B, T = o.shape[0], o.shape[1]
    of = o.astype(jnp.float32)
    of = of * lax.rsqrt(jnp.mean(of * of, axis=-1, keepdims=True) + EPS)
    of = of.reshape(B, T, GLA_V) * norm_g.astype(jnp.float32)
    return (of * jax.nn.silu(g.astype(jnp.float32))).astype(g.dtype)


def gla_group(pieces_x, pieces_c, row, col, wa2, ba, norm_g, need_ctx):
    def prep(pieces, rotate):
        q, k, v, a_f, a_b, g = pieces
        B, T, _ = q.shape
        q = q.reshape(B, T, GLA_HEADS, GLA_DK)
        k = k.reshape(B, T, GLA_HEADS, GLA_DK)
        if rotate:
            q = axial_rope(q, row, col)
            k = axial_rope(k, row, col)
        q = q * GLA_DK ** -0.5
        v = v.reshape(B, T, GLA_HEADS, GLA_DV)

        def log_decay(a, d):
            z = (a @ wa2[d] + ba[d]).astype(jnp.float32)
            return (jax.nn.log_sigmoid(z) / GLA_GATE_TAU).reshape(B, T, GLA_HEADS, GLA_DK)

        tr = lambda t: jnp.swapaxes(t, 1, 2)
        return tr(q), tr(k), tr(v), tr(log_decay(a_f, 0)), tr(log_decay(a_b, 1)), g

    qx, kx, vx, lfx, lbx, gx = prep(pieces_x, True)
    qc, kc, vc, lfc, lbc, gc = prep(pieces_c, False)
    B = qx.shape[0]
    s0 = jnp.zeros((B, GLA_HEADS, GLA_DK, GLA_DV), jnp.float32)
    flip = lambda t: jnp.flip(t, axis=2)
    oc_f, sc_f = gla_chunked(qc, kc, vc, lfc, s0, need_ctx)
    oc_b, sc_b = gla_chunked(flip(qc), flip(kc), flip(vc), flip(lbc), s0, need_ctx)
    ox_f, _ = gla_chunked(qx, kx, vx, lfx, sc_f)
    ox_b, _ = gla_chunked(flip(qx), flip(kx), flip(vx), flip(lbx), sc_b)
    out_x = gla_output(ox_f + flip(ox_b), gx, norm_g)
    out_c = gla_output(oc_f + flip(oc_b), gc, norm_g) if need_ctx else None
    return out_x, out_c


def neighbourhood_attention(q, k, v, kc, vc, rpb):
    B, S, H, dh = q.shape
    R = S // GRID_W
    kh = min(NA_KH_MAX, R)
    kg = k.reshape(B, R, GRID_W, H, dh)
    vg = v.reshape(B, R, GRID_W, H, dh)
    qg = q.reshape(B, R, GRID_W, H, dh)
    cols = jnp.arange(GRID_W)
    col_idx = jnp.clip(cols - NA_KW // 2, 0, GRID_W - NA_KW)[:, None] + jnp.arange(NA_KW)[None]
    col_bias_idx = col_idx - cols[:, None] + NA_KW - 1
    rows = jnp.arange(R)
    row_start = jnp.clip(rows - kh // 2, 0, R - kh)
    n_nb = kh * NA_KW

    def one_row(args):
        r, rs, q_r = args
        k_blk = lax.dynamic_slice_in_dim(kg, rs, kh, axis=1)
        v_blk = lax.dynamic_slice_in_dim(vg, rs, kh, axis=1)
        k_nb = k_blk[:, :, col_idx]
        v_nb = v_blk[:, :, col_idx]
        row_bias_idx = rs + jnp.arange(kh) - r + NA_KH_MAX - 1
        bias = rpb[:, row_bias_idx[None, :, None], col_bias_idx[:, None, :]]
        s_nb = jnp.einsum('bqhd,biqjhd->bhqij', q_r, k_nb) + bias
        s_ctx = jnp.einsum('bqhd,blhd->bhql', q_r, kc)
        logits = jnp.concatenate([s_nb.reshape(B, H, GRID_W, n_nb), s_ctx], axis=-1).astype(jnp.float32)
        p = jax.nn.softmax(logits, axis=-1).astype(q_r.dtype)
        p_nb = p[..., :n_nb].reshape(B, H, GRID_W, kh, NA_KW)
        p_ctx = p[..., n_nb:]
        return (jnp.einsum('bhqij,biqjhd->bqhd', p_nb, v_nb)
                + jnp.einsum('bhql,blhd->bqhd', p_ctx, vc))

    out = lax.map(one_row, (rows, row_start, jnp.moveaxis(qg, 1, 0)))
    return jnp.moveaxis(out, 0, 1).reshape(B, S, H * dh)


def context_attention(q, k, v):
    B, L, H, dh = q.shape
    s = jnp.einsum('bqhd,bkhd->bhqk', q, k).astype(jnp.float32)
    p = jax.nn.softmax(s, axis=-1).astype(q.dtype)
    return jnp.einsum('bhqk,bkhd->bqhd', p, v).reshape(B, L, H * dh)


def na_group(pieces_x, pieces_c, rpb, need_ctx):
    qx, kx, vx = pieces_x
    qc, kc, vc = pieces_c
    B, S, _ = qx.shape
    L = kc.shape[1]
    hx = lambda t: t.reshape(B, S, NA_HEADS, NA_DH)
    hc = lambda t: t.reshape(B, L, NA_HEADS, NA_DH)
    kch, vch = hc(kc), hc(vc)
    out_x = neighbourhood_attention(hx(qx) * NA_DH ** -0.5, hx(kx), hx(vx), kch, vch, rpb)
    out_c = context_attention(hc(qc) * NA_DH ** -0.5, kch, vch) if need_ctx else None
    return out_x, out_c


def pool_mix(u, pool_w, pool_scale):
    B, T, _ = u.shape
    uf = u.astype(jnp.float32)
    cs = jnp.concatenate([jnp.zeros((B, 1, POOL_WIDTH), jnp.float32), jnp.cumsum(uf, axis=1)], axis=1)
    t = jnp.arange(T)
    groups = []
    for g, w in enumerate(POOL_WINDOWS):
        lo = jnp.clip(t - w // 2, 0, T)
        hi = jnp.clip(t + w - w // 2, 0, T)
        sl = slice(g * POOL_GROUP, (g + 1) * POOL_GROUP)
        cnt = (hi - lo).astype(jnp.float32)[None, :, None]
        groups.append((cs[:, hi, sl] - cs[:, lo, sl]) / cnt - uf[:, :, sl])
    y = jnp.stack(groups, axis=2).astype(u.dtype)
    y = jnp.einsum('btgi,gio->btgo', y, pool_w).reshape(B, T, POOL_WIDTH)
    return y * pool_scale


def token_mixer(hx, hc, row, col, w_in, gla_wa2, gla_ba, gla_norm_g, na_rpb, pool_w, pool_scale, w_out, need_ctx):
    px = split_in(hx @ w_in)
    pc = split_in(hc @ w_in)
    gla_x, gla_c = gla_group(px[0:6], pc[0:6], row, col, gla_wa2, gla_ba, gla_norm_g, need_ctx)
    na_x, na_c = na_group(px[6:9], pc[6:9], na_rpb, need_ctx)
    out_x = jnp.concatenate([gla_x, na_x, pool_mix(px[9], pool_w, pool_scale)], axis=-1) @ w_out
    if not need_ctx:
        return out_x, None
    out_c = jnp.concatenate([gla_c, na_c, pool_mix(pc[9], pool_w, pool_scale)], axis=-1) @ w_out
    return out_x, out_c


def moe_ffn(h, router_w, router_b, w_gu, b_gu, w_down, b_down):
    T, D = h.shape
    logits = (h @ router_w + router_b).astype(jnp.float32)
    top_logit, top_e = lax.top_k(logits, TOP_K)
    gate = jax.nn.softmax(top_logit, axis=-1).astype(h.dtype)
    A = T * TOP_K
    flat_e = top_e.reshape(A)
    flat_tok = jnp.repeat(jnp.arange(T, dtype=jnp.int32), TOP_K)
    flat_gate = gate.reshape(A)
    order = jnp.argsort(flat_e)
    e_sorted = flat_e[order]
    counts = jnp.bincount(flat_e, length=N_EXPERTS)
    padded = (counts + MOE_BLOCK - 1) // MOE_BLOCK * MOE_BLOCK
    start = jnp.cumsum(counts) - counts
    pend = jnp.cumsum(padded)
    pstart = pend - padded
    dest = pstart[e_sorted] + jnp.arange(A) - start[e_sorted]
    n_blocks = (A + MOE_BLOCK - 1) // MOE_BLOCK + N_EXPERTS
    P = n_blocks * MOE_BLOCK
    slot_tok = jnp.full((P,), T, jnp.int32).at[dest].set(flat_tok[order])
    slot_gate = jnp.zeros((P,), h.dtype).at[dest].set(flat_gate[order])
    block_e = jnp.minimum(jnp.searchsorted(pend, jnp.arange(n_blocks) * MOE_BLOCK, side='right'), N_EXPERTS - 1)
    h_pad = jnp.concatenate([h, jnp.zeros((1, D), h.dtype)], axis=0)

    def expert_block(args):
        tok_b, e = args
        xb = h_pad[tok_b]
        gu = xb @ w_gu[e] + b_gu[e]
        g, u = jnp.split(gu, 2, axis=-1)
        g = jnp.minimum(g, SWIGLU_LIMIT)
        u = jnp.clip(u, -SWIGLU_LIMIT, SWIGLU_LIMIT)
        act = g * jax.nn.sigmoid(SWIGLU_ALPHA * g) * (u + 1)
        return act @ w_down[e] + b_down[e]

    yb = lax.map(expert_block, (slot_tok.reshape(n_blocks, MOE_BLOCK), block_e)).reshape(P, D)
    y = jnp.zeros((T + 1, D), h.dtype).at[slot_tok].add(yb * slot_gate[:, None])
    return y[:T]


def setup_inputs(seed: int = 0) -> dict:
    key = jax.random.key(seed)
    ks = jax.random.split(key, 24)
    f32 = jnp.float32
    D = D_MODEL
    nrm = lambda k, shape, s: jax.random.normal(k, shape, f32) * s
    return {
        'x': nrm(ks[0], (BATCH, SEQ, D), 1.0),
        'c': nrm(ks[1], (BATCH, D), 1.0),
        'ctx': nrm(ks[2], (BATCH, CTX_LEN, D), 1.0),
        'c_ctx': nrm(ks[3], (D,), 1.0),
        'norm1_g': 1.0 + nrm(ks[4], (DEPTH, D), 0.05),
        'norm2_g': 1.0 + nrm(ks[5], (DEPTH, D), 0.05),
        'w_ada': nrm(ks[6], (DEPTH, D, 6 * D), 0.5 * D ** -0.5),
        'b_ada': nrm(ks[7], (DEPTH, 6 * D), 0.02),
        'w_in': nrm(ks[8], (DEPTH, D, IN_WIDTH), D ** -0.5),
        'gla_wa2': nrm(ks[9], (DEPTH, 2, GLA_GATE_RANK, GLA_QK), GLA_GATE_RANK ** -0.5),
        'gla_ba': nrm(ks[10], (DEPTH, 2, GLA_QK), 0.1),
        'gla_norm_g': 1.0 + nrm(ks[11], (DEPTH, GLA_V), 0.05),
        'na_rpb': nrm(ks[12], (DEPTH, NA_HEADS, 2 * NA_KH_MAX - 1, 2 * NA_KW - 1), 0.05),
        'pool_w': nrm(ks[13], (DEPTH, len(POOL_WINDOWS), POOL_GROUP, POOL_GROUP), POOL_GROUP ** -0.5),
        'pool_scale': 1.0 + nrm(ks[14], (DEPTH, POOL_WIDTH), 0.1),
        'w_out': nrm(ks[15], (DEPTH, MIX_WIDTH, D), MIX_WIDTH ** -0.5),
        'router_w': nrm(ks[16], (DEPTH, D, N_EXPERTS), D ** -0.5),
        'router_b': nrm(ks[17], (DEPTH, N_EXPERTS), 0.01),
        'w_gu': nrm(ks[18], (DEPTH, N_EXPERTS, D, 2 * D_FF), D ** -0.5),
        'b_gu': nrm(ks[19], (DEPTH, N_EXPERTS, 2 * D_FF), 0.01),
        'w_down': nrm(ks[20], (DEPTH, N_EXPERTS, D_FF, D), D_FF ** -0.5),
        'b_down': nrm(ks[21], (DEPTH, N_EXPERTS, D), 0.01),
        'final_g': 1.0 + nrm(ks[22], (D,), 0.05),
    }


def reference(x, c, ctx, c_ctx, norm1_g, norm2_g, w_ada, b_ada, w_in, gla_wa2, gla_ba, gla_norm_g,
              na_rpb, pool_w, pool_scale, w_out, router_w, router_b, w_gu, b_gu, w_down, b_down, final_g):
    B, S, D = x.shape
    L = ctx.shape[1]
    t = jnp.arange(S)
    row = t // GRID_W
    col = t % GRID_W
    cond_x = jax.nn.silu(c)
    cond_c = jax.nn.silu(c_ctx)[None]
    xc = ctx
    for l in range(DEPTH):
        need_ctx = l < DEPTH - 1
        mod_x = (cond_x @ w_ada[l] + b_ada[l])[:, None]
        mod_c = (cond_c @ w_ada[l] + b_ada[l])[:, None]
        sh1x, sc1x, g1x, sh2x, sc2x, g2x = jnp.split(mod_x, 6, axis=-1)
        sh1c, sc1c, g1c, sh2c, sc2c, g2c = jnp.split(mod_c, 6, axis=-1)
        hx = modulate(rmsnorm(x, norm1_g[l]), sh1x, sc1x)
        hc = modulate(rmsnorm(xc, norm1_g[l]), sh1c, sc1c)
        mx, mc = token_mixer(hx, hc, row, col, w_in[l], gla_wa2[l], gla_ba[l], gla_norm_g[l], na_rpb[l],
                             pool_w[l], pool_scale[l], w_out[l], need_ctx)
        x = x + g1x * mx
        hx2 = modulate(rmsnorm(x, norm2_g[l]), sh2x, sc2x)
        moe_args = (router_w[l], router_b[l], w_gu[l], b_gu[l], w_down[l], b_down[l])
        if need_ctx:
            xc = xc + g1c * mc
            hc2 = modulate(rmsnorm(xc, norm2_g[l]), sh2c, sc2c)
            tokens = jnp.concatenate([hx2.reshape(B * S, D), hc2.reshape(B * L, D)], axis=0)
            y = moe_ffn(tokens, *moe_args)
            yx = y[:B * S].reshape(B, S, D)
            xc = xc + g2c * y[B * S:].reshape(B, L, D)
        else:
            yx = moe_ffn(hx2.reshape(B * S, D), *moe_args).reshape(B, S, D)
        x = x + g2x * yx
    return rmsnorm(x, final_g)
```

```python
import functools

import numpy as np
import jax
import jax.numpy as jnp
from jax import lax
from jax.experimental import pallas as pl
from jax.experimental.pallas import tpu as pltpu

F32 = jnp.float32
BF16 = jnp.bfloat16

EPS = 1e-6
ROPE_THETA = 10000.0
GRID_W = 64

GLA_HEADS = 4
GLA_DK = 48
GLA_DV = 96
GLA_QK = GLA_HEADS * GLA_DK
GLA_V = GLA_HEADS * GLA_DV
GLA_RANK = 16
GLA_TAU = 16.0
GLA_CHUNK = 64

NA_HEADS = 6
NA_DH = 64
NA_W = NA_HEADS * NA_DH
NA_KH = 8
NA_KW = 16

POOL_WINDOWS = (2, 4, 8, 16)
POOL_WIDTH = 256
POOL_GROUP = 64
POOL_HALO = 8

N_EXPERTS = 32
TOP_K = 4
SWIGLU_LIMIT = 7.0
SWIGLU_ALPHA = 1.702

LANE = 128
TOKEN_TILE = 256
MOE_ROWS = 256
NEG = -1e30
VMEM_LIMIT = 56 * 1024 * 1024

QK_PAD = 256
C_QK = 0
C_SW = C_QK + 2 * QK_PAD
C_V = C_SW + 2 * QK_PAD
C_G = C_V + GLA_V
C_A = C_G + GLA_V
C_NQ = C_A + LANE
C_NK = C_NQ + NA_W
C_NV = C_NK + NA_W
C_PU = C_NV + NA_W
C_END = C_PU + POOL_WIDTH


def _dot(a, b):
    return jnp.dot(a, b, preferred_element_type=F32)


def _dot_nt(a, b):
    return lax.dot_general(a, b, (((1,), (1,)), ((), ())), preferred_element_type=F32)


def _split(x):
    hi = x.astype(BF16)
    lo = (x - hi.astype(F32)).astype(BF16)
    return hi, lo


def _sigmoid(x):
    return 1.0 / (1.0 + jnp.exp(-x))


def _params(sem):
    return pltpu.CompilerParams(dimension_semantics=sem, vmem_limit_bytes=VMEM_LIMIT)


def _mod_kernel(c_ref, w_ref, b_ref, o_ref):
    c = c_ref[...]
    s = c * _sigmoid(c)
    sh, sl = _split(s)
    wh, wl = _split(w_ref[...])
    o_ref[...] = _dot(sh, wh) + _dot(sl, wh) + _dot(sh, wl) + b_ref[...]


def _modulation(cond, w_ada, b_ada):
    R, D = cond.shape
    N = w_ada.shape[1]
    bn = 1024
    return pl.pallas_call(
        _mod_kernel,
        grid=(N // bn,),
        in_specs=[pl.BlockSpec((R, D), lambda j: (0, 0)),
                  pl.BlockSpec((D, bn), lambda j: (0, j)),
                  pl.BlockSpec((1, bn), lambda j: (0, j))],
        out_specs=pl.BlockSpec((R, bn), lambda j: (0, j)),
        out_shape=jax.ShapeDtypeStruct((R, N), F32),
        compiler_params=_params(("arbitrary",)),
        name="adaln_mod",
    )(cond, w_ada, b_ada.reshape(1, N))


def _inproj_kernel(x_ref, mod_ref, g_ref, w_ref, cos_ref, sin_ref,
                   q_ref, k_ref, v_ref, gg_ref, a_ref, nq_ref, nk_ref, nv_ref, pu_ref):
    x = x_ref[...]
    ms = jnp.mean(x * x, axis=-1, keepdims=True)
    h = x * lax.rsqrt(ms + EPS) * g_ref[...]
    h = h * (1.0 + mod_ref[1:2, :]) + mod_ref[0:1, :]
    hb = h.astype(BF16)

    def proj(lo, hi):
        return _dot(hb, w_ref[:, lo:hi])

    rot = proj(C_QK, C_SW) * cos_ref[...] + proj(C_SW, C_V) * sin_ref[...]
    q_ref[...] = rot[:, 0:GLA_QK].astype(BF16)
    k_ref[...] = rot[:, QK_PAD:QK_PAD + GLA_QK].astype(BF16)
    v_ref[...] = proj(C_V, C_G).astype(BF16)
    gg_ref[...] = proj(C_G, C_A).astype(BF16)
    a_ref[...] = proj(C_A, C_NQ)
    nq_ref[...] = (proj(C_NQ, C_NK) * (NA_DH ** -0.5)).astype(BF16)
    nk_ref[...] = proj(C_NK, C_NV).astype(BF16)
    nv_ref[...] = proj(C_NV, C_PU).astype(BF16)
    pu_ref[...] = proj(C_PU, C_END)


def _input_projection(x, mod, norm_g, w, cos_t, sin_t, nb, tpb):
    N, D = x.shape
    tm = TOKEN_TILE

    def mod_idx(i):
        return (jnp.where(i % tpb == 0, nb, i // tpb), 0, 0)

    tok = lambda w_: pl.BlockSpec((tm, w_), lambda i: (i, 0))
    outs = [(GLA_QK, BF16), (GLA_QK, BF16), (GLA_V, BF16), (GLA_V, BF16), (LANE, F32),
            (NA_W, BF16), (NA_W, BF16), (NA_W, BF16), (POOL_WIDTH, F32)]
    return pl.pallas_call(
        _inproj_kernel,
        grid=(N // tm,),
        in_specs=[tok(D),
                  pl.BlockSpec((None, 6, D), mod_idx),
                  pl.BlockSpec((1, D), lambda i: (0, 0)),
                  pl.BlockSpec((D, C_END), lambda i: (0, 0)),
                  pl.BlockSpec((tm, 2 * QK_PAD), lambda i: (i % tpb, 0)),
                  pl.BlockSpec((tm, 2 * QK_PAD), lambda i: (i % tpb, 0))],
        out_specs=[tok(w_) for w_, _ in outs],
        out_shape=[jax.ShapeDtypeStruct((N, w_), dt) for w_, dt in outs],
        compiler_params=_params(("arbitrary",)),
        name="in_proj",
    )(x, mod, norm_g.reshape(1, D), w, cos_t, sin_t)


def _lane_head(idx, width):
    h = jnp.zeros_like(idx)
    for t in range(1, GLA_HEADS):
        h = h + (idx >= t * width).astype(jnp.int32)
    return h


def _gla_kernel(q_ref, k_ref, v_ref, g_ref, a_ref, wa_ref, ba_ref, ng_ref, o_ref,
                st_ref, of_ref, *, n_ctx, n_all):
    C = GLA_CHUNK
    i32 = jnp.int32
    r64 = lax.broadcasted_iota(i32, (C, C), 0)
    c64 = lax.broadcasted_iota(i32, (C, C), 1)
    tri_f = jnp.where(c64 <= r64, 1.0, 0.0).astype(BF16)
    tri_b = jnp.where(c64 >= r64, 1.0, 0.0).astype(BF16)
    kmask = (lax.broadcasted_iota(i32, (GLA_HEADS * C, GLA_QK), 0) // C
             == _lane_head(lax.broadcasted_iota(i32, (GLA_HEADS * C, GLA_QK), 1), GLA_DK))
    vmask = (lax.broadcasted_iota(i32, (GLA_HEADS * C, GLA_V), 0) // C
             == _lane_head(lax.broadcasted_iota(i32, (GLA_HEADS * C, GLA_V), 1), GLA_DV))
    smask = (_lane_head(lax.broadcasted_iota(i32, (GLA_V, GLA_QK), 0), GLA_DV)
             == _lane_head(lax.broadcasted_iota(i32, (GLA_V, GLA_QK), 1), GLA_DK))
    hind = jnp.where(_lane_head(lax.broadcasted_iota(i32, (GLA_V, GLA_V), 0), GLA_DV)
                     == _lane_head(lax.broadcasted_iota(i32, (GLA_V, GLA_V), 1), GLA_DV),
                     1.0, 0.0).astype(BF16)
    qi_ = lax.broadcasted_iota(i32, (C, GLA_HEADS * C), 0)
    kj_ = lax.broadcasted_iota(i32, (C, GLA_HEADS * C), 1) % C
    causal_f = qi_ >= kj_
    causal_b = qi_ <= kj_

    def chunk(ci, fwd):
        d = 0 if fwd else 1
        r0 = pl.multiple_of(ci * C, C)
        rows = pl.ds(r0, C)
        q = q_ref[rows, :].astype(F32)
        k = k_ref[rows, :].astype(F32)
        v = v_ref[rows, :]
        ah, al = _split(a_ref[rows, :])
        wh, wl = _split(wa_ref[d])
        z = _dot(ah, wh) + _dot(al, wh) + _dot(ah, wl) + ba_ref[d]
        la = (jnp.minimum(z, 0.0) - jnp.log(1.0 + jnp.exp(-jnp.abs(z)))) * (1.0 / GLA_TAU)
        lh, ll = _split(la)
        tri = tri_f if fwd else tri_b
        b = _dot(tri, lh) + _dot(tri, ll)
        if fwd:
            bmid = b[C // 2 - 1:C // 2, :]
            btot = b[C - 1:C, :]
        else:
            bmid = b[C // 2:C // 2 + 1, :]
            btot = b[0:1, :]
        qe = (q * jnp.exp(b - bmid)).astype(BF16)
        ke = (k * jnp.exp(bmid - b)).astype(BF16)
        qs = (q * jnp.exp(b)).astype(BF16)
        kt = (k * jnp.exp(btot - b)).astype(BF16)
        zero = jnp.zeros((), BF16)
        kbd = jnp.where(kmask, jnp.concatenate([ke] * GLA_HEADS, axis=0), zero)
        vbd = jnp.where(vmask, jnp.concatenate([v] * GLA_HEADS, axis=0), zero)
        sc = _dot_nt(qe, kbd)
        sc = jnp.where(causal_f if fwd else causal_b, sc, 0.0)
        st = st_ref[...]
        o = _dot(sc.astype(BF16), vbd) + _dot_nt(qs, st.astype(BF16))
        vt = v.astype(F32).T.astype(BF16)
        ds = _dot(vt, kt)
        st_ref[...] = st * jnp.exp(btot) + jnp.where(smask, ds, 0.0)
        if fwd:
            of_ref[rows, :] = o
        else:
            tot = of_ref[rows, :] + o
            th, tl = _split(tot * tot)
            ms = (_dot(th, hind) + _dot(tl, hind)) * (1.0 / GLA_DV)
            g = g_ref[rows, :].astype(F32)
            y = tot * lax.rsqrt(ms + EPS) * ng_ref[...] * (g * _sigmoid(g))
            o_ref[rows, :] = y.astype(o_ref.dtype)

    def run(fwd, start, count, step):
        def body(i, carry):
            chunk(start + step * i, fwd)
            return carry
        lax.fori_loop(0, count, body, 0)

    st_ref[...] = jnp.zeros_like(st_ref)
    run(True, 0, n_all, 1)
    st_ref[...] = jnp.zeros_like(st_ref)
    run(False, n_ctx - 1, n_ctx, -1)
    run(False, n_all - 1, n_all - n_ctx, -1)


def _gla(q, k, v, g, a, wa, ba, norm_g, nb, T, L):
    r3 = lambda t: t.reshape(nb, T, t.shape[-1])
    seq = lambda w_: pl.BlockSpec((None, T, w_), lambda b: (b, 0, 0))
    kern = functools.partial(_gla_kernel, n_ctx=L // GLA_CHUNK, n_all=T // GLA_CHUNK)
    out = pl.pallas_call(
        kern,
        grid=(nb,),
        in_specs=[seq(GLA_QK), seq(GLA_QK), seq(GLA_V), seq(GLA_V), seq(LANE),
                  pl.BlockSpec((2, LANE, GLA_QK), lambda b: (0, 0, 0)),
                  pl.BlockSpec((2, 1, GLA_QK), lambda b: (0, 0, 0)),
                  pl.BlockSpec((1, GLA_V), lambda b: (0, 0))],
        out_specs=seq(GLA_V),
        out_shape=jax.ShapeDtypeStruct((nb, T, GLA_V), BF16),
        scratch_shapes=[pltpu.VMEM((GLA_V, GLA_QK), F32), pltpu.VMEM((T, GLA_V), F32)],
        compiler_params=_params(("arbitrary",)),
        name="gla",
    )(r3(q), r3(k), r3(v), r3(g), r3(a), wa, ba, norm_g.reshape(1, GLA_V))
    return out.reshape(nb * T, GLA_V)


def _na_kernel(q_ref, k_ref, v_ref, bias_ref, o_ref, *, n_ctx_steps, n_rows, L):
    j = pl.program_id(1)
    W = GRID_W
    lane = lax.broadcasted_iota(jnp.int32, (2 * W, LANE), 1)
    row = lax.broadcasted_iota(jnp.int32, (2 * W, LANE), 0)
    qmask = (row < W) == (lane < NA_DH)
    out_lo = lax.broadcasted_iota(jnp.int32, (W, LANE), 1) < NA_DH
    zero = jnp.zeros((), BF16)

    def pair_q(p):
        q2 = q_ref[:, p * LANE:(p + 1) * LANE]
        return jnp.where(qmask, jnp.concatenate([q2, q2], axis=0), zero)

    def finish(p, acc, l):
        o2 = acc / l
        o_ref[:, p * LANE:(p + 1) * LANE] = jnp.where(out_lo, o2[:W], o2[W:]).astype(o_ref.dtype)

    @pl.when(j < n_ctx_steps)
    def _():
        for p in range(NA_HEADS // 2):
            cols = slice(p * LANE, (p + 1) * LANE)
            s = _dot_nt(pair_q(p), k_ref[0:L, cols])
            m = jnp.max(s, axis=-1, keepdims=True)
            e = jnp.exp(s - m)
            l = jnp.sum(e, axis=-1, keepdims=True)
            finish(p, _dot(e.astype(BF16), v_ref[0:L, cols]), l)

    @pl.when(j >= n_ctx_steps)
    def _():
        r = j - n_ctx_steps
        rs = jnp.clip(r - NA_KH // 2, 0, n_rows - NA_KH)
        k0 = pl.multiple_of(L + rs * W, W)
        win = pl.ds(k0, NA_KH * W)
        for p in range(NA_HEADS // 2):
            cols = slice(p * LANE, (p + 1) * LANE)
            qb = pair_q(p)
            s_w = _dot_nt(qb, k_ref[win, cols]) + bias_ref[p]
            s_c = _dot_nt(qb, k_ref[0:L, cols])
            m = jnp.maximum(jnp.max(s_w, axis=-1, keepdims=True), jnp.max(s_c, axis=-1, keepdims=True))
            e_w = jnp.exp(s_w - m)
            e_c = jnp.exp(s_c - m)
            l = jnp.sum(e_w, axis=-1, keepdims=True) + jnp.sum(e_c, axis=-1, keepdims=True)
            acc = _dot(e_w.astype(BF16), v_ref[win, cols]) + _dot(e_c.astype(BF16), v_ref[0:L, cols])
            finish(p, acc, l)


def _na_bias_table(rpb):
    W = GRID_W
    q = np.arange(W)[:, None]
    kc = np.arange(W)[None, :]
    cs = np.clip(q - NA_KW // 2, 0, W - NA_KW)
    valid = (kc >= cs) & (kc < cs + NA_KW)
    cidx = np.clip(kc - q + NA_KW - 1, 0, 2 * NA_KW - 2)
    vi = np.arange(NA_KH)[:, None]
    ridx = vi + np.arange(NA_KH)[None, :]
    t = rpb[:, ridx][:, :, :, cidx]
    t = jnp.where(valid[None, None, None], t, NEG)
    t = jnp.transpose(t, (1, 0, 3, 2, 4))
    return t.reshape(NA_KH, NA_HEADS // 2, 2 * W, NA_KH * W).astype(F32)


def _neighbourhood_attention(nq, nk, nv, bias, nb, T, L):
    W = GRID_W
    n_ctx_steps = L // W
    n_rows = (T - L) // W
    spb = T // W
    r3 = lambda t: t.reshape(nb, T, NA_W)

    def bias_idx(b, j):
        r = j - n_ctx_steps
        rs = jnp.clip(r - NA_KH // 2, 0, n_rows - NA_KH)
        return (jnp.clip(rs - r + NA_KH - 1, 0, NA_KH - 1), 0, 0, 0)

    kern = functools.partial(_na_kernel, n_ctx_steps=n_ctx_steps, n_rows=n_rows, L=L)
    return pl.pallas_call(
        kern,
        grid=(nb, spb),
        in_specs=[pl.BlockSpec((W, NA_W), lambda b, j: (b * spb + j, 0)),
                  pl.BlockSpec((None, T, NA_W), lambda b, j: (b, 0, 0)),
                  pl.BlockSpec((None, T, NA_W), lambda b, j: (b, 0, 0)),
                  pl.BlockSpec((None, NA_HEADS // 2, 2 * W, NA_KH * W), bias_idx)],
        out_specs=pl.BlockSpec((W, NA_W), lambda b, j: (b * spb + j, 0)),
        out_shape=jax.ShapeDtypeStruct((nb * T, NA_W), BF16),
        compiler_params=_params(("arbitrary", "arbitrary")),
        name="neigh_attn",
    )(nq, r3(nk), r3(nv), bias)


def _pool_kernel(u_ref, w_ref, sc_ref, o_ref, *, T, L):
    i = pl.program_id(1)
    tm = TOKEN_TILE
    ext = tm + 2 * POOL_HALO
    t0 = i * tm
    e0 = pl.multiple_of(jnp.clip(t0 - POOL_HALO, 0, T - ext), 8)
    seg_lo = jnp.where(t0 < L, 0, L)
    seg_hi = jnp.where(t0 < L, L, T)
    uh, ul = _split(u_ref[pl.ds(e0, ext), :])
    t = t0 + lax.broadcasted_iota(jnp.int32, (tm, ext), 0)
    c = e0 + lax.broadcasted_iota(jnp.int32, (tm, ext), 1)
    tcol = t0 + lax.broadcasted_iota(jnp.int32, (tm, 1), 0)
    lane = lax.broadcasted_iota(jnp.int32, (tm, POOL_WIDTH), 1)
    y = jnp.zeros((tm, POOL_WIDTH), F32)
    for gi, w in enumerate(POOL_WINDOWS):
        lo = jnp.maximum(t - w // 2, seg_lo)
        hi = jnp.minimum(t + w - w // 2, seg_hi)
        band = jnp.where((c >= lo) & (c < hi), 1.0, 0.0).astype(BF16)
        cnt = (jnp.minimum(tcol + w - w // 2, seg_hi) - jnp.maximum(tcol - w // 2, seg_lo)).astype(F32)
        s = (_dot(band, uh) + _dot(band, ul)) / cnt
        y = jnp.where((lane >= gi * POOL_GROUP) & (lane < (gi + 1) * POOL_GROUP), s, y)
    y = y - u_ref[pl.ds(pl.multiple_of(t0, tm), tm), :]
    o_ref[...] = (_dot(y.astype(BF16), w_ref[...]) * sc_ref[...]).astype(o_ref.dtype)


def _pool(u, w_bd, scale, nb, T, L):
    tpb = T // TOKEN_TILE
    kern = functools.partial(_pool_kernel, T=T, L=L)
    return pl.pallas_call(
        kern,
        grid=(nb, tpb),
        in_specs=[pl.BlockSpec((None, T, POOL_WIDTH), lambda b, i: (b, 0, 0)),
                  pl.BlockSpec((POOL_WIDTH, POOL_WIDTH), lambda b, i: (0, 0)),
                  pl.BlockSpec((1, POOL_WIDTH), lambda b, i: (0, 0))],
        out_specs=pl.BlockSpec((TOKEN_TILE, POOL_WIDTH), lambda b, i: (b * tpb + i, 0)),
        out_shape=jax.ShapeDtypeStruct((nb * T, POOL_WIDTH), BF16),
        compiler_params=_params(("arbitrary", "arbitrary")),
        name="pool_mix",
    )(u.reshape(nb, T, POOL_WIDTH), w_bd, scale.reshape(1, POOL_WIDTH))


def _outproj_kernel(gla_ref, na_ref, pm_ref, x_ref, mod_ref, g_ref, wo_ref, rw_ref, rb_ref,
                    x1_ref, h2_ref, te_ref, tg_ref):
    mx = (_dot(gla_ref[...], wo_ref[0:GLA_V, :])
          + _dot(na_ref[...], wo_ref[GLA_V:GLA_V + NA_W, :])
          + _dot(pm_ref[...], wo_ref[GLA_V + NA_W:, :]))
    x1 = x_ref[...] + mod_ref[2:3, :] * mx
    x1_ref[...] = x1
    ms = jnp.mean(x1 * x1, axis=-1, keepdims=True)
    h = x1 * lax.rsqrt(ms + EPS) * g_ref[...]
    h = h * (1.0 + mod_ref[4:5, :]) + mod_ref[3:4, :]
    h2_ref[...] = h.astype(BF16)
    hh, hl = _split(h)
    wh, wl = _split(rw_ref[...])
    logits = _dot(hh, wh) + _dot(hl, wh) + _dot(hh, wl) + rb_ref[...]
    lane = lax.broadcasted_iota(jnp.int32, logits.shape, 1)
    te = jnp.zeros(logits.shape, jnp.int32)
    tv = jnp.zeros(logits.shape, F32)
    cur = logits
    vals = []
    for kk in range(TOP_K):
        m = jnp.max(cur, axis=-1, keepdims=True)
        idx = jnp.min(jnp.where(cur == m, lane, LANE), axis=-1, keepdims=True)
        vals.append(m)
        te = jnp.where(lane == kk, idx, te)
        cur = jnp.where(lane == idx, NEG * 2.0, cur)
    den = sum(jnp.exp(v - vals[0]) for v in vals)
    for kk in range(TOP_K):
        tv = jnp.where(lane == kk, jnp.exp(vals[kk] - vals[0]) / den, tv)
    te_ref[...] = te
    tg_ref[...] = tv


def _output_projection(gla, na, pm, x, mod, norm_g, w_out, rw, rb, nb, tpb):
    N, D = x.shape
    tm = TOKEN_TILE

    def mod_idx(i):
        return (jnp.where(i % tpb == 0, nb, i // tpb), 0, 0)

    tok = lambda w_: pl.BlockSpec((tm, w_), lambda i: (i, 0))
    return pl.pallas_call(
        _outproj_kernel,
        grid=(N // tm,),
        in_specs=[tok(GLA_V), tok(NA_W), tok(POOL_WIDTH), tok(D),
                  pl.BlockSpec((None, 6, D), mod_idx),
                  pl.BlockSpec((1, D), lambda i: (0, 0)),
                  pl.BlockSpec((D, D), lambda i: (0, 0)),
                  pl.BlockSpec((D, LANE), lambda i: (0, 0)),
                  pl.BlockSpec((1, LANE), lambda i: (0, 0))],
        out_specs=[tok(D), tok(D), tok(LANE), tok(LANE)],
        out_shape=[jax.ShapeDtypeStruct((N, D), F32), jax.ShapeDtypeStruct((N, D), BF16),
                   jax.ShapeDtypeStruct((N, LANE), jnp.int32), jax.ShapeDtypeStruct((N, LANE), F32)],
        compiler_params=_params(("arbitrary",)),
        name="out_proj_router",
    )(gla, na, pm, x, mod, norm_g.reshape(1, D), w_out, rw, rb)


def _moe_kernel(be_ref, nv_ref, xs_ref, gate_ref, wgu_ref, bgu_ref, wd_ref, bd_ref, y_ref):
    i = pl.program_id(0)

    @pl.when(i < nv_ref[0])
    def _():
        dff = wd_ref.shape[0]
        gu = _dot(xs_ref[...], wgu_ref[...]) + bgu_ref[...]
        g = jnp.minimum(gu[:, :dff], SWIGLU_LIMIT)
        u = jnp.clip(gu[:, dff:], -SWIGLU_LIMIT, SWIGLU_LIMIT)
        act = g * _sigmoid(SWIGLU_ALPHA * g) * (u + 1.0)
        y = _dot(act.astype(BF16), wd_ref[...]) + bd_ref[...]
        y_ref[...] = y * gate_ref[...]

    @pl.when(i >= nv_ref[0])
    def _():
        y_ref[...] = jnp.zeros_like(y_ref)


def _moe_blocks(block_e, n_valid, xs, slot_gate, w_gu, b_gu, w_down, b_down):
    P, D = xs.shape
    E, _, F2 = w_gu.shape
    bm = MOE_ROWS
    grid_spec = pltpu.PrefetchScalarGridSpec(
        num_scalar_prefetch=2,
        grid=(P // bm,),
        in_specs=[pl.BlockSpec((bm, D), lambda i, be, nv: (i, 0)),
                  pl.BlockSpec((bm, 1), lambda i, be, nv: (i, 0)),
                  pl.BlockSpec((None, D, F2), lambda i, be, nv: (be[i], 0, 0)),
                  pl.BlockSpec((None, 1, F2), lambda i, be, nv: (be[i], 0, 0)),
                  pl.BlockSpec((None, F2 // 2, D), lambda i, be, nv: (be[i], 0, 0)),
                  pl.BlockSpec((None, 1, D), lambda i, be, nv: (be[i], 0, 0))],
        out_specs=pl.BlockSpec((bm, D), lambda i, be, nv: (i, 0)),
    )
    return pl.pallas_call(
        _moe_kernel,
        grid_spec=grid_spec,
        out_shape=jax.ShapeDtypeStruct((P, D), F32),
        compiler_params=_params(("arbitrary",)),
        name="moe_experts",
    )(block_e, n_valid, xs, slot_gate, w_gu, b_gu.reshape(E, 1, F2), w_down, b_down.reshape(E, 1, D))


def _combine_kernel(x_ref, y_ref, mod_ref, o_ref):
    y = y_ref[0] + y_ref[1] + y_ref[2] + y_ref[3]
    o_ref[...] = x_ref[...] + mod_ref[5:6, :] * y


def _combine(x1, y4, mod, nb, tpb):
    N, D = x1.shape
    tm = TOKEN_TILE

    def mod_idx(i):
        return (jnp.where(i % tpb == 0, nb, i // tpb), 0, 0)

    return pl.pallas_call(
        _combine_kernel,
        grid=(N // tm,),
        in_specs=[pl.BlockSpec((tm, D), lambda i: (i, 0)),
                  pl.BlockSpec((TOP_K, tm, D), lambda i: (0, i, 0)),
                  pl.BlockSpec((None, 6, D), mod_idx)],
        out_specs=pl.BlockSpec((tm, D), lambda i: (i, 0)),
        out_shape=jax.ShapeDtypeStruct((N, D), F32),
        compiler_params=_params(("arbitrary",)),
        name="moe_combine",
    )(x1, y4, mod)


def _final_kernel(x_ref, g_ref, o_ref):
    x = x_ref[...]
    ms = jnp.mean(x * x, axis=-1, keepdims=True)
    o_ref[...] = x * lax.rsqrt(ms + EPS) * g_ref[...]


def _final_norm(x, g, nb, tpb, S):
    N, D = x.shape
    tm = TOKEN_TILE
    lt = S // tm
    ct = tpb - lt
    out = pl.pallas_call(
        _final_kernel,
        grid=(nb, lt),
        in_specs=[pl.BlockSpec((tm, D), lambda b, i: (b * tpb + ct + i, 0)),
                  pl.BlockSpec((1, D), lambda b, i: (0, 0))],
        out_specs=pl.BlockSpec((tm, D), lambda b, i: (b * lt + i, 0)),
        out_shape=jax.ShapeDtypeStruct((nb * S, D), F32),
        compiler_params=_params(("arbitrary", "arbitrary")),
        name="final_norm",
    )(x, g.reshape(1, D))
    return out.reshape(nb, S, D)


def _rope_tables(S, L):
    half = GLA_DK // 2
    nfreq = half // 2
    inv_freq = ROPE_THETA ** (-jnp.arange(0, half, 2, dtype=F32) / half)
    t = jnp.arange(S)
    pos = jnp.stack([(t // GRID_W).astype(F32), (t % GRID_W).astype(F32)], axis=0)
    d = np.arange(GLA_QK) % GLA_DK
    axis = d // half
    e = d % half
    fidx = e % nfreq
    sign = np.where(e < nfreq, -1.0, 1.0).astype(np.float32)
    ang = pos[axis].T * inv_freq[fidx][None, :]
    cos = jnp.cos(ang)
    sin = jnp.sin(ang) * sign[None, :]
    cos = jnp.concatenate([jnp.ones((L, GLA_QK), F32), cos], axis=0)
    sin = jnp.concatenate([jnp.zeros((L, GLA_QK), F32), sin], axis=0)
    pad = jnp.zeros((L + S, QK_PAD - GLA_QK), F32)
    qs = GLA_DK ** -0.5
    cos_t = jnp.concatenate([cos * qs, pad, cos, pad], axis=1)
    sin_t = jnp.concatenate([sin * qs, pad, sin, pad], axis=1)
    return cos_t, sin_t


def _partner_perm():
    d = np.arange(GLA_QK) % GLA_DK
    e = d % (GLA_DK // 2)
    nfreq = GLA_DK // 4
    return np.where(e < nfreq, np.arange(GLA_QK) + nfreq, np.arange(GLA_QK) - nfreq)


def _layout_w_in(w_in):
    D = w_in.shape[0]
    o = np.cumsum([0, GLA_QK, GLA_QK, GLA_V, GLA_RANK, GLA_RANK, GLA_V, NA_W, NA_W, NA_W, POOL_WIDTH])
    piece = lambda n: w_in[:, o[n]:o[n + 1]]
    perm = _partner_perm()
    z = lambda n: jnp.zeros((D, n), w_in.dtype)
    qpad = z(QK_PAD - GLA_QK)
    cols = [piece(0), qpad, piece(1), qpad,
            piece(0)[:, perm], qpad, piece(1)[:, perm], qpad,
            piece(2), piece(5),
            piece(3), piece(4), z(LANE - 2 * GLA_RANK),
            piece(6), piece(7), piece(8), piece(9)]
    return jnp.concatenate(cols, axis=1).astype(BF16)


def _layout_gate(wa2, ba):
    w = jnp.zeros((2, LANE, GLA_QK), F32)
    w = w.at[0, 0:GLA_RANK].set(wa2[0]).at[1, GLA_RANK:2 * GLA_RANK].set(wa2[1])
    return w, ba.reshape(2, 1, GLA_QK)


def _block_diag(pool_w):
    G = pool_w.shape[0]
    w = jnp.zeros((POOL_WIDTH, POOL_WIDTH), F32)
    for g in range(G):
        w = w.at[g * POOL_GROUP:(g + 1) * POOL_GROUP, g * POOL_GROUP:(g + 1) * POOL_GROUP].set(pool_w[g])
    return w.astype(BF16)


def _routing(top_e, top_g, n_blocks):
    N = top_e.shape[0]
    A = N * TOP_K
    bm = MOE_ROWS
    flat_e = top_e[:, :TOP_K].reshape(A)
    flat_g = top_g[:, :TOP_K].reshape(A)
    onehot = (flat_e[:, None] == jnp.arange(N_EXPERTS, dtype=jnp.int32)[None, :]).astype(jnp.int32)
    csum = jnp.cumsum(onehot, axis=0)
    rank = jnp.take_along_axis(csum, flat_e[:, None], axis=1)[:, 0] - 1
    counts = csum[-1]
    padded = (counts + bm - 1) // bm * bm
    pend = jnp.cumsum(padded)
    pstart = pend - padded
    dest = pstart[flat_e] + rank
    P = n_blocks * bm
    slot_tok = jnp.zeros((P,), jnp.int32).at[dest].set(jnp.arange(A, dtype=jnp.int32) // TOP_K)
    slot_gate = jnp.zeros((P,), F32).at[dest].set(flat_g)
    block_e = jnp.minimum(jnp.searchsorted(pend, jnp.arange(n_blocks, dtype=jnp.int32) * bm, side='right'),
                          N_EXPERTS - 1).astype(jnp.int32)
    n_valid = (pend[-1] // bm).astype(jnp.int32).reshape(1)
    return slot_tok, slot_gate.reshape(P, 1), block_e, n_valid, dest.reshape(N, TOP_K)


def kernel(x, c, ctx, c_ctx, norm1_g, norm2_g, w_ada, b_ada, w_in, gla_wa2, gla_ba, gla_norm_g, na_rpb,
           pool_w, pool_scale, w_out, router_w, router_b, w_gu, b_gu, w_down, b_down, final_g):
    B, S, D = x.shape
    L = ctx.shape[1]
    assert L == TOKEN_TILE and S % TOKEN_TILE == 0 and S % GRID_W == 0
    T = L + S
    tpb = T // TOKEN_TILE
    N = B * T
    A = N * TOP_K
    n_blocks = (A + MOE_ROWS - 1) // MOE_ROWS + N_EXPERTS

    xa = jnp.concatenate([ctx, x], axis=1).reshape(N, D)
    cond_rows = 8 * ((B + 1 + 7) // 8)
    cond = jnp.zeros((cond_rows, D), F32).at[:B].set(c).at[B].set(c_ctx)
    cos_t, sin_t = _rope_tables(S, L)

    def layer(xa, p):
        (n1, n2, wada, bada, win, wa2, ba, gng, rpb, pw, psc, wout, rw, rb, wgu, bgu, wdn, bdn) = p
        mod = _modulation(cond, wada, bada).reshape(cond_rows, 6, D)
        wa, bap = _layout_gate(wa2, ba)
        q, k, v, g, a, nq, nk, nv, pu = _input_projection(xa, mod, n1, _layout_w_in(win), cos_t, sin_t, B, tpb)
        gla_o = _gla(q, k, v, g, a, wa, bap, gng, B, T, L)
        na_o = _neighbourhood_attention(nq, nk, nv, _na_bias_table(rpb), B, T, L)
        pool_o = _pool(pu, _block_diag(pw), psc, B, T, L)
        rwp = jnp.zeros((D, LANE), F32).at[:, :N_EXPERTS].set(rw)
        rbp = jnp.full((1, LANE), NEG, F32).at[0, :N_EXPERTS].set(rb)
        x1, h2, top_e, top_g = _output_projection(gla_o, na_o, pool_o, xa, mod, n2, wout.astype(BF16),
                                                  rwp, rbp, B, tpb)
        slot_tok, slot_gate, block_e, n_valid, dest = _routing(top_e, top_g, n_blocks)
        xs = jnp.take(h2, slot_tok, axis=0)
        yb = _moe_blocks(block_e, n_valid, xs, slot_gate, wgu.astype(BF16), bgu, wdn.astype(BF16), bdn)
        y4 = jnp.take(yb, dest.T.reshape(-1), axis=0).reshape(TOP_K, N, D)
        return _combine(x1, y4, mod, B, tpb), None

    params = (norm1_g, norm2_g, w_ada, b_ada, w_in, gla_wa2, gla_ba, gla_norm_g, na_rpb, pool_w, pool_scale,
              w_out, router_w, router_b, w_gu, b_gu, w_down, b_down)
    xa, _ = lax.scan(layer, xa, params)
    return _final_norm(xa, final_g, B, tpb, S)
```

```python
import functools

import numpy as np
import jax
import jax.numpy as jnp
from jax import lax
from jax.experimental import pallas as pl
from jax.experimental.pallas import tpu as pltpu

F32 = jnp.float32
BF16 = jnp.bfloat16

EPS = 1e-6
ROPE_THETA = 10000.0
GRID_W = 64

GLA_HEADS = 4
GLA_DK = 48
GLA_DV = 96
GLA_QK = GLA_HEADS * GLA_DK
GLA_V = GLA_HEADS * GLA_DV
GLA_RANK = 16
GLA_TAU = 16.0
GLA_CHUNK = 64

NA_HEADS = 6
NA_DH = 64
NA_W = NA_HEADS * NA_DH
NA_KH = 8
NA_KW = 16

POOL_WINDOWS = (2, 4, 8, 16)
POOL_WIDTH = 256
POOL_GROUP = 64
POOL_HALO = 8

N_EXPERTS = 32
TOP_K = 4
SWIGLU_LIMIT = 7.0
SWIGLU_ALPHA = 1.702

LANE = 128
TOKEN_TILE = 256
MOE_ROWS = 256
NEG = -1e30
VMEM_LIMIT = 56 * 1024 * 1024

QK_PAD = 256
C_QK = 0
C_SW = C_QK + 2 * QK_PAD
C_V = C_SW + 2 * QK_PAD
C_G = C_V + GLA_V
C_A = C_G + GLA_V
C_NQ = C_A + LANE
C_NK = C_NQ + NA_W
C_NV = C_NK + NA_W
C_PU = C_NV + NA_W
C_END = C_PU + POOL_WIDTH


def _dot(a, b):
    return jnp.dot(a, b, preferred_element_type=F32)


def _dot_nt(a, b):
    return lax.dot_general(a, b, (((1,), (1,)), ((), ())), preferred_element_type=F32)


def _split(x):
    hi = x.astype(BF16)
    lo = (x - hi.astype(F32)).astype(BF16)
    return hi, lo


def _sigmoid(x):
    return 1.0 / (1.0 + jnp.exp(-x))


def _params(sem):
    return pltpu.CompilerParams(dimension_semantics=sem, vmem_limit_bytes=VMEM_LIMIT)


def _mod_kernel(c_ref, w_ref, b_ref, o_ref):
    c = c_ref[...]
    s = c * _sigmoid(c)
    sh, sl = _split(s)
    wh, wl = _split(w_ref[...])
    o_ref[...] = _dot(sh, wh) + _dot(sl, wh) + _dot(sh, wl) + b_ref[...]


def _modulation(cond, w_ada, b_ada):
    R, D = cond.shape
    N = w_ada.shape[1]
    bn = 1024
    return pl.pallas_call(
        _mod_kernel,
        grid=(N // bn,),
        in_specs=[pl.BlockSpec((R, D), lambda j: (0, 0)),
                  pl.BlockSpec((D, bn), lambda j: (0, j)),
                  pl.BlockSpec((1, bn), lambda j: (0, j))],
        out_specs=pl.BlockSpec((R, bn), lambda j: (0, j)),
        out_shape=jax.ShapeDtypeStruct((R, N), F32),
        compiler_params=_params(("arbitrary",)),
        name="adaln_mod",
    )(cond, w_ada, b_ada.reshape(1, N))


def _inproj_kernel(x_ref, mod_ref, g_ref, w_ref, cos_ref, sin_ref,
                   q_ref, k_ref, v_ref, gg_ref, a_ref, nq_ref, nk_ref, nv_ref, pu_ref):
    x = x_ref[...]
    ms = jnp.mean(x * x, axis=-1, keepdims=True)
    h = x * lax.rsqrt(ms + EPS) * g_ref[...]
    h = h * (1.0 + mod_ref[1:2, :]) + mod_ref[0:1, :]
    hb = h.astype(BF16)

    def proj(lo, hi):
        return _dot(hb, w_ref[:, lo:hi])

    rot = proj(C_QK, C_SW) * cos_ref[...] + proj(C_SW, C_V) * sin_ref[...]
    q_ref[...] = rot[:, 0:GLA_QK].astype(BF16)
    k_ref[...] = rot[:, QK_PAD:QK_PAD + GLA_QK].astype(BF16)
    v_ref[...] = proj(C_V, C_G).astype(BF16)
    gg_ref[...] = proj(C_G, C_A).astype(BF16)
    a_ref[...] = proj(C_A, C_NQ)
    nq_ref[...] = (proj(C_NQ, C_NK) * (NA_DH ** -0.5)).astype(BF16)
    nk_ref[...] = proj(C_NK, C_NV).astype(BF16)
    nv_ref[...] = proj(C_NV, C_PU).astype(BF16)
    pu_ref[...] = proj(C_PU, C_END)


def _input_projection(x, mod, norm_g, w, cos_t, sin_t, nb, tpb):
    N, D = x.shape
    tm = TOKEN_TILE

    def mod_idx(i):
        return (jnp.where(i % tpb == 0, nb, i // tpb), 0, 0)

    tok = lambda w_: pl.BlockSpec((tm, w_), lambda i: (i, 0))
    outs = [(GLA_QK, BF16), (GLA_QK, BF16), (GLA_V, BF16), (GLA_V, BF16), (LANE, F32),
            (NA_W, BF16), (NA_W, BF16), (NA_W, BF16), (POOL_WIDTH, F32)]
    return pl.pallas_call(
        _inproj_kernel,
        grid=(N // tm,),
        in_specs=[tok(D),
                  pl.BlockSpec((None, 6, D), mod_idx),
                  pl.BlockSpec((1, D), lambda i: (0, 0)),
                  pl.BlockSpec((D, C_END), lambda i: (0, 0)),
                  pl.BlockSpec((tm, 2 * QK_PAD), lambda i: (i % tpb, 0)),
                  pl.BlockSpec((tm, 2 * QK_PAD), lambda i: (i % tpb, 0))],
        out_specs=[tok(w_) for w_, _ in outs],
        out_shape=[jax.ShapeDtypeStruct((N, w_), dt) for w_, dt in outs],
        compiler_params=_params(("arbitrary",)),
        name="in_proj",
    )(x, mod, norm_g.reshape(1, D), w, cos_t, sin_t)


def _lane_head(idx, width):
    h = jnp.zeros_like(idx)
    for t in range(1, GLA_HEADS):
        h = h + (idx >= t * width).astype(jnp.int32)
    return h


def _gla_kernel(q_ref, k_ref, v_ref, g_ref, a_ref, wa_ref, ba_ref, ng_ref, o_ref,
                st_ref, of_ref, *, n_ctx, n_all):
    C = GLA_CHUNK
    i32 = jnp.int32
    r64 = lax.broadcasted_iota(i32, (C, C), 0)
    c64 = lax.broadcasted_iota(i32, (C, C), 1)
    tri_f = jnp.where(c64 <= r64, 1.0, 0.0).astype(BF16)
    tri_b = jnp.where(c64 >= r64, 1.0, 0.0).astype(BF16)
    kmask = (lax.broadcasted_iota(i32, (GLA_HEADS * C, GLA_QK), 0) // C
             == _lane_head(lax.broadcasted_iota(i32, (GLA_HEADS * C, GLA_QK), 1), GLA_DK))
    vmask = (lax.broadcasted_iota(i32, (GLA_HEADS * C, GLA_V), 0) // C
             == _lane_head(lax.broadcasted_iota(i32, (GLA_HEADS * C, GLA_V), 1), GLA_DV))
    smask = (_lane_head(lax.broadcasted_iota(i32, (GLA_V, GLA_QK), 0), GLA_DV)
             == _lane_head(lax.broadcasted_iota(i32, (GLA_V, GLA_QK), 1), GLA_DK))
    hind = jnp.where(_lane_head(lax.broadcasted_iota(i32, (GLA_V, GLA_V), 0), GLA_DV)
                     == _lane_head(lax.broadcasted_iota(i32, (GLA_V, GLA_V), 1), GLA_DV),
                     1.0, 0.0).astype(BF16)
    qi_ = lax.broadcasted_iota(i32, (C, GLA_HEADS * C), 0)
    kj_ = lax.broadcasted_iota(i32, (C, GLA_HEADS * C), 1) % C
    causal_f = qi_ >= kj_
    causal_b = qi_ <= kj_

    def chunk(ci, fwd):
        d = 0 if fwd else 1
        r0 = pl.multiple_of(ci * C, C)
        rows = pl.ds(r0, C)
        q = q_ref[rows, :].astype(F32)
        k = k_ref[rows, :].astype(F32)
        v = v_ref[rows, :]
        ah, al = _split(a_ref[rows, :])
        wh, wl = _split(wa_ref[d])
        z = _dot(ah, wh) + _dot(al, wh) + _dot(ah, wl) + ba_ref[d]
        la = (jnp.minimum(z, 0.0) - jnp.log(1.0 + jnp.exp(-jnp.abs(z)))) * (1.0 / GLA_TAU)
        lh, ll = _split(la)
        tri = tri_f if fwd else tri_b
        b = _dot(tri, lh) + _dot(tri, ll)
        if fwd:
            bmid = b[C // 2 - 1:C // 2, :]
            btot = b[C - 1:C, :]
        else:
            bmid = b[C // 2:C // 2 + 1, :]
            btot = b[0:1, :]
        qe = (q * jnp.exp(b - bmid)).astype(BF16)
        ke = (k * jnp.exp(bmid - b)).astype(BF16)
        qs = (q * jnp.exp(b)).astype(BF16)
        kt = (k * jnp.exp(btot - b)).astype(BF16)
        zero = jnp.zeros((), BF16)
        kbd = jnp.where(kmask, jnp.concatenate([ke] * GLA_HEADS, axis=0), zero)
        vbd = jnp.where(vmask, jnp.concatenate([v] * GLA_HEADS, axis=0), zero)
        sc = _dot_nt(qe, kbd)
        sc = jnp.where(causal_f if fwd else causal_b, sc, 0.0)
        st = st_ref[...]
        o = _dot(sc.astype(BF16), vbd) + _dot_nt(qs, st.astype(BF16))
        vt = v.astype(F32).T.astype(BF16)
        ds = _dot(vt, kt)
        st_ref[...] = st * jnp.exp(btot) + jnp.where(smask, ds, 0.0)
        if fwd:
            of_ref[rows, :] = o
        else:
            tot = of_ref[rows, :] + o
            th, tl = _split(tot * tot)
            ms = (_dot(th, hind) + _dot(tl, hind)) * (1.0 / GLA_DV)
            g = g_ref[rows, :].astype(F32)
            y = tot * lax.rsqrt(ms + EPS) * ng_ref[...] * (g * _sigmoid(g))
            o_ref[rows, :] = y.astype(o_ref.dtype)

    def run(fwd, start, count, step):
        def body(i, carry):
            chunk(start + step * i, fwd)
            return carry
        lax.fori_loop(0, count, body, 0)

    st_ref[...] = jnp.zeros_like(st_ref)
    run(True, 0, n_all, 1)
    st_ref[...] = jnp.zeros_like(st_ref)
    run(False, n_ctx - 1, n_ctx, -1)
    run(False, n_all - 1, n_all - n_ctx, -1)


def _gla(q, k, v, g, a, wa, ba, norm_g, nb, T, L):
    seq = lambda w_: pl.BlockSpec((T, w_), lambda b: (b, 0))
    kern = functools.partial(_gla_kernel, n_ctx=L // GLA_CHUNK, n_all=T // GLA_CHUNK)
    return pl.pallas_call(
        kern,
        grid=(nb,),
        in_specs=[seq(GLA_QK), seq(GLA_QK), seq(GLA_V), seq(GLA_V), seq(LANE),
                  pl.BlockSpec((2, LANE, GLA_QK), lambda b: (0, 0, 0)),
                  pl.BlockSpec((2, 1, GLA_QK), lambda b: (0, 0, 0)),
                  pl.BlockSpec((1, GLA_V), lambda b: (0, 0))],
        out_specs=seq(GLA_V),
        out_shape=jax.ShapeDtypeStruct((nb * T, GLA_V), BF16),
        scratch_shapes=[pltpu.VMEM((GLA_V, GLA_QK), F32), pltpu.VMEM((T, GLA_V), F32)],
        compiler_params=_params(("arbitrary",)),
        name="gla",
    )(q, k, v, g, a, wa, ba, norm_g.reshape(1, GLA_V))


def _na_kernel(q_ref, k_ref, v_ref, bias_ref, o_ref, *, n_ctx_steps, n_rows, L):
    j = pl.program_id(1)
    W = GRID_W
    lane = lax.broadcasted_iota(jnp.int32, (2 * W, LANE), 1)
    row = lax.broadcasted_iota(jnp.int32, (2 * W, LANE), 0)
    qmask = (row < W) == (lane < NA_DH)
    out_lo = lax.broadcasted_iota(jnp.int32, (W, LANE), 1) < NA_DH
    zero = jnp.zeros((), BF16)

    def pair_q(p):
        q2 = q_ref[:, p * LANE:(p + 1) * LANE]
        return jnp.where(qmask, jnp.concatenate([q2, q2], axis=0), zero)

    def finish(p, acc, l):
        o2 = acc / l
        o_ref[:, p * LANE:(p + 1) * LANE] = jnp.where(out_lo, o2[:W], o2[W:]).astype(o_ref.dtype)

    @pl.when(j < n_ctx_steps)
    def _():
        for p in range(NA_HEADS // 2):
            cols = slice(p * LANE, (p + 1) * LANE)
            s = _dot_nt(pair_q(p), k_ref[0:L, cols])
            m = jnp.max(s, axis=-1, keepdims=True)
            e = jnp.exp(s - m)
            l = jnp.sum(e, axis=-1, keepdims=True)
            finish(p, _dot(e.astype(BF16), v_ref[0:L, cols]), l)

    @pl.when(j >= n_ctx_steps)
    def _():
        r = j - n_ctx_steps
        rs = jnp.clip(r - NA_KH // 2, 0, n_rows - NA_KH)
        k0 = pl.multiple_of(L + rs * W, W)
        win = pl.ds(k0, NA_KH * W)
        for p in range(NA_HEADS // 2):
            cols = slice(p * LANE, (p + 1) * LANE)
            qb = pair_q(p)
            s_w = _dot_nt(qb, k_ref[win, cols]) + bias_ref[p]
            s_c = _dot_nt(qb, k_ref[0:L, cols])
            m = jnp.maximum(jnp.max(s_w, axis=-1, keepdims=True), jnp.max(s_c, axis=-1, keepdims=True))
            e_w = jnp.exp(s_w - m)
            e_c = jnp.exp(s_c - m)
            l = jnp.sum(e_w, axis=-1, keepdims=True) + jnp.sum(e_c, axis=-1, keepdims=True)
            acc = _dot(e_w.astype(BF16), v_ref[win, cols]) + _dot(e_c.astype(BF16), v_ref[0:L, cols])
            finish(p, acc, l)


def _na_bias_table(rpb):
    W = GRID_W
    q = np.arange(W)[:, None]
    kc = np.arange(W)[None, :]
    cs = np.clip(q - NA_KW // 2, 0, W - NA_KW)
    valid = (kc >= cs) & (kc < cs + NA_KW)
    cidx = np.clip(kc - q + NA_KW - 1, 0, 2 * NA_KW - 2)
    vi = np.arange(NA_KH)[:, None]
    ridx = vi + np.arange(NA_KH)[None, :]
    t = rpb[:, ridx][:, :, :, cidx]
    t = jnp.where(valid[None, None, None], t, NEG)
    t = jnp.transpose(t, (1, 0, 3, 2, 4))
    return t.reshape(NA_KH, NA_HEADS // 2, 2 * W, NA_KH * W).astype(F32)


def _neighbourhood_attention(nq, nk, nv, bias, nb, T, L):
    W = GRID_W
    n_ctx_steps = L // W
    n_rows = (T - L) // W
    spb = T // W
    def bias_idx(b, j):
        r = j - n_ctx_steps
        rs = jnp.clip(r - NA_KH // 2, 0, n_rows - NA_KH)
        return (jnp.clip(rs - r + NA_KH - 1, 0, NA_KH - 1), 0, 0, 0)

    kern = functools.partial(_na_kernel, n_ctx_steps=n_ctx_steps, n_rows=n_rows, L=L)
    return pl.pallas_call(
        kern,
        grid=(nb, spb),
        in_specs=[pl.BlockSpec((W, NA_W), lambda b, j: (b * spb + j, 0)),
                  pl.BlockSpec((T, NA_W), lambda b, j: (b, 0)),
                  pl.BlockSpec((T, NA_W), lambda b, j: (b, 0)),
                  pl.BlockSpec((None, NA_HEADS // 2, 2 * W, NA_KH * W), bias_idx)],
        out_specs=pl.BlockSpec((W, NA_W), lambda b, j: (b * spb + j, 0)),
        out_shape=jax.ShapeDtypeStruct((nb * T, NA_W), BF16),
        compiler_params=_params(("arbitrary", "arbitrary")),
        name="neigh_attn",
    )(nq, nk, nv, bias)


def _pool_kernel(u_ref, w_ref, sc_ref, o_ref, *, T, L):
    i = pl.program_id(1)
    tm = TOKEN_TILE
    ext = tm + 2 * POOL_HALO
    t0 = i * tm
    e0 = pl.multiple_of(jnp.clip(t0 - POOL_HALO, 0, T - ext), 8)
    seg_lo = jnp.where(t0 < L, 0, L)
    seg_hi = jnp.where(t0 < L, L, T)
    uh, ul = _split(u_ref[pl.ds(e0, ext), :])
    t = t0 + lax.broadcasted_iota(jnp.int32, (tm, ext), 0)
    c = e0 + lax.broadcasted_iota(jnp.int32, (tm, ext), 1)
    tcol = t0 + lax.broadcasted_iota(jnp.int32, (tm, 1), 0)
    lane = lax.broadcasted_iota(jnp.int32, (tm, POOL_WIDTH), 1)
    y = jnp.zeros((tm, POOL_WIDTH), F32)
    for gi, w in enumerate(POOL_WINDOWS):
        lo = jnp.maximum(t - w // 2, seg_lo)
        hi = jnp.minimum(t + w - w // 2, seg_hi)
        band = jnp.where((c >= lo) & (c < hi), 1.0, 0.0).astype(BF16)
        cnt = (jnp.minimum(tcol + w - w // 2, seg_hi) - jnp.maximum(tcol - w // 2, seg_lo)).astype(F32)
        s = (_dot(band, uh) + _dot(band, ul)) / cnt
        y = jnp.where((lane >= gi * POOL_GROUP) & (lane < (gi + 1) * POOL_GROUP), s, y)
    y = y - u_ref[pl.ds(pl.multiple_of(t0, tm), tm), :]
    o_ref[...] = (_dot(y.astype(BF16), w_ref[...]) * sc_ref[...]).astype(o_ref.dtype)


def _pool(u, w_bd, scale, nb, T, L):
    tpb = T // TOKEN_TILE
    kern = functools.partial(_pool_kernel, T=T, L=L)
    return pl.pallas_call(
        kern,
        grid=(nb, tpb),
        in_specs=[pl.BlockSpec((T, POOL_WIDTH), lambda b, i: (b, 0)),
                  pl.BlockSpec((POOL_WIDTH, POOL_WIDTH), lambda b, i: (0, 0)),
                  pl.BlockSpec((1, POOL_WIDTH), lambda b, i: (0, 0))],
        out_specs=pl.BlockSpec((TOKEN_TILE, POOL_WIDTH), lambda b, i: (b * tpb + i, 0)),
        out_shape=jax.ShapeDtypeStruct((nb * T, POOL_WIDTH), BF16),
        compiler_params=_params(("arbitrary", "arbitrary")),
        name="pool_mix",
    )(u, w_bd, scale.reshape(1, POOL_WIDTH))


def _outproj_kernel(gla_ref, na_ref, pm_ref, x_ref, mod_ref, g_ref, wo_ref, rw_ref, rb_ref,
                    x1_ref, h2_ref, te_ref, tg_ref, rk_ref, cnt_ref, carry_ref):
    @pl.when(pl.program_id(0) == 0)
    def _():
        carry_ref[...] = jnp.zeros_like(carry_ref)

    mx = (_dot(gla_ref[...], wo_ref[0:GLA_V, :])
          + _dot(na_ref[...], wo_ref[GLA_V:GLA_V + NA_W, :])
          + _dot(pm_ref[...], wo_ref[GLA_V + NA_W:, :]))
    x1 = x_ref[...] + mod_ref[2:3, :] * mx
    x1_ref[...] = x1
    ms = jnp.mean(x1 * x1, axis=-1, keepdims=True)
    h = x1 * lax.rsqrt(ms + EPS) * g_ref[...]
    h = h * (1.0 + mod_ref[4:5, :]) + mod_ref[3:4, :]
    h2_ref[...] = h
    hh, hl = _split(h)
    wh, wl = _split(rw_ref[...])
    logits = _dot(hh, wh) + _dot(hl, wh) + _dot(hh, wl) + rb_ref[...]
    tm = logits.shape[0]
    lane = lax.broadcasted_iota(jnp.int32, logits.shape, 1)
    te = jnp.zeros(logits.shape, jnp.int32)
    tv = jnp.zeros(logits.shape, F32)
    cur = logits
    vals, idxs = [], []
    for kk in range(TOP_K):
        m = jnp.max(cur, axis=-1, keepdims=True)
        idx = jnp.min(jnp.where(cur == m, lane, LANE), axis=-1, keepdims=True)
        vals.append(m)
        idxs.append(idx)
        te = jnp.where(lane == kk, idx, te)
        cur = jnp.where(lane == idx, NEG * 2.0, cur)
    den = sum(jnp.exp(v - vals[0]) for v in vals)
    for kk in range(TOP_K):
        tv = jnp.where(lane == kk, jnp.exp(vals[kk] - vals[0]) / den, tv)
    te_ref[...] = te
    tg_ref[...] = tv
    hot = sum(jnp.where(lane == idx, 1.0, 0.0) for idx in idxs)
    before = (lax.broadcasted_iota(jnp.int32, (tm, tm), 1) < lax.broadcasted_iota(jnp.int32, (tm, tm), 0))
    base = _dot(jnp.where(before, 1.0, 0.0).astype(BF16), hot.astype(BF16)) + carry_ref[...]
    rk = jnp.zeros(logits.shape, jnp.int32)
    for kk in range(TOP_K):
        r = jnp.sum(jnp.where(lane == idxs[kk], base, 0.0), axis=-1, keepdims=True)
        rk = jnp.where(lane == kk, r.astype(jnp.int32), rk)
    rk_ref[...] = rk
    carry_ref[...] = carry_ref[...] + jnp.sum(hot, axis=0, keepdims=True)
    cnt_ref[...] = jnp.broadcast_to(carry_ref[...], cnt_ref.shape).astype(jnp.int32)


def _output_projection(gla, na, pm, x, mod, norm_g, w_out, rw, rb, nb, tpb):
    N, D = x.shape
    tm = TOKEN_TILE

    def mod_idx(i):
        return (jnp.where(i % tpb == 0, nb, i // tpb), 0, 0)

    tok = lambda w_: pl.BlockSpec((tm, w_), lambda i: (i, 0))
    return pl.pallas_call(
        _outproj_kernel,
        grid=(N // tm,),
        in_specs=[tok(GLA_V), tok(NA_W), tok(POOL_WIDTH), tok(D),
                  pl.BlockSpec((None, 6, D), mod_idx),
                  pl.BlockSpec((1, D), lambda i: (0, 0)),
                  pl.BlockSpec((D, D), lambda i: (0, 0)),
                  pl.BlockSpec((D, LANE), lambda i: (0, 0)),
                  pl.BlockSpec((1, LANE), lambda i: (0, 0))],
        out_specs=[tok(D), tok(D), tok(LANE), tok(LANE), tok(LANE),
                   pl.BlockSpec((8, LANE), lambda i: (0, 0))],
        out_shape=[jax.ShapeDtypeStruct((N, D), F32), jax.ShapeDtypeStruct((N, D), F32),
                   jax.ShapeDtypeStruct((N, LANE), jnp.int32), jax.ShapeDtypeStruct((N, LANE), F32),
                   jax.ShapeDtypeStruct((N, LANE), jnp.int32), jax.ShapeDtypeStruct((8, LANE), jnp.int32)],
        scratch_shapes=[pltpu.VMEM((1, LANE), F32)],
        compiler_params=_params(("arbitrary",)),
        name="out_proj_router",
    )(gla, na, pm, x, mod, norm_g.reshape(1, D), w_out, rw, rb)


def _dispatch_kernel(dest_ref, h_ref, xz_ref, xs_ref, sem):
    del xz_ref
    tm = h_ref.shape[0]

    def row(t, carry):
        for kk in range(TOP_K):
            d = dest_ref[t * TOP_K + kk]
            pltpu.make_async_copy(h_ref.at[pl.ds(t, 1)], xs_ref.at[pl.ds(d, 1)], sem).start()
        return carry

    lax.fori_loop(0, tm, row, 0, unroll=8)
    for kk in range(TOP_K):
        pltpu.make_async_copy(h_ref, xs_ref.at[pl.ds(0, tm)], sem).wait()


def _dispatch(dest_flat, h2, n_slots):
    N, D = h2.shape
    tm = TOKEN_TILE
    return pl.pallas_call(
        _dispatch_kernel,
        grid=(N // tm,),
        in_specs=[pl.BlockSpec((tm * TOP_K,), lambda i: (i,), memory_space=pltpu.SMEM),
                  pl.BlockSpec((tm, D), lambda i: (i, 0)),
                  pl.BlockSpec(memory_space=pl.ANY)],
        out_specs=pl.BlockSpec(memory_space=pl.ANY),
        out_shape=jax.ShapeDtypeStruct((n_slots, D), F32),
        scratch_shapes=[pltpu.SemaphoreType.DMA(())],
        input_output_aliases={2: 0},
        compiler_params=_params(("arbitrary",)),
        name="moe_dispatch",
    )(dest_flat, h2, jnp.zeros((n_slots, D), F32))


def _moe_kernel(be_ref, nv_ref, xs_ref, wgu_ref, bgu_ref, wd_ref, bd_ref, y_ref):
    i = pl.program_id(0)

    @pl.when(i < nv_ref[0])
    def _():
        dff = wd_ref.shape[0]
        gu = _dot(xs_ref[...].astype(BF16), wgu_ref[...]) + bgu_ref[...]
        g = jnp.minimum(gu[:, :dff], SWIGLU_LIMIT)
        u = jnp.clip(gu[:, dff:], -SWIGLU_LIMIT, SWIGLU_LIMIT)
        act = g * _sigmoid(SWIGLU_ALPHA * g) * (u + 1.0)
        y_ref[...] = _dot(act.astype(BF16), wd_ref[...]) + bd_ref[...]

    @pl.when(i >= nv_ref[0])
    def _():
        y_ref[...] = jnp.zeros_like(y_ref)


def _moe_blocks(block_e, n_valid, xs, w_gu, b_gu, w_down, b_down):
    P, D = xs.shape
    E, _, F2 = w_gu.shape
    bm = MOE_ROWS
    grid_spec = pltpu.PrefetchScalarGridSpec(
        num_scalar_prefetch=2,
        grid=(P // bm,),
        in_specs=[pl.BlockSpec((bm, D), lambda i, be, nv: (i, 0)),
                  pl.BlockSpec((None, D, F2), lambda i, be, nv: (be[i], 0, 0)),
                  pl.BlockSpec((None, 1, F2), lambda i, be, nv: (be[i], 0, 0)),
                  pl.BlockSpec((None, F2 // 2, D), lambda i, be, nv: (be[i], 0, 0)),
                  pl.BlockSpec((None, 1, D), lambda i, be, nv: (be[i], 0, 0))],
        out_specs=pl.BlockSpec((bm, D), lambda i, be, nv: (i, 0)),
    )
    return pl.pallas_call(
        _moe_kernel,
        grid_spec=grid_spec,
        out_shape=jax.ShapeDtypeStruct((P, D), F32),
        compiler_params=_params(("arbitrary",)),
        name="moe_experts",
    )(block_e, n_valid, xs, w_gu, b_gu.reshape(E, 1, F2), w_down, b_down.reshape(E, 1, D))


def _combine_kernel(dcur_ref, dnxt_ref, x_ref, tg_ref, mod_ref, yb_ref, o_ref, ybuf, sem, *, n_tiles):
    i = pl.program_id(0)
    tm = x_ref.shape[0]
    slot = i % 2

    def issue(dref, s):
        def row(t, carry):
            for kk in range(TOP_K):
                d = dref[t * TOP_K + kk]
                pltpu.make_async_copy(yb_ref.at[pl.ds(d, 1)], ybuf.at[s, kk, pl.ds(t, 1)], sem.at[s]).start()
            return carry
        lax.fori_loop(0, tm, row, 0, unroll=8)

    @pl.when(i == 0)
    def _():
        issue(dcur_ref, 0)

    @pl.when(i + 1 < n_tiles)
    def _():
        issue(dnxt_ref, 1 - slot)

    for kk in range(TOP_K):
        pltpu.make_async_copy(yb_ref.at[pl.ds(0, tm)], ybuf.at[slot, kk], sem.at[slot]).wait()
    tg = tg_ref[...]
    y = tg[:, 0:1] * ybuf[slot, 0]
    for kk in range(1, TOP_K):
        y = y + tg[:, kk:kk + 1] * ybuf[slot, kk]
    o_ref[...] = x_ref[...] + mod_ref[5:6, :] * y


def _combine(dest_flat, x1, top_g, yb, mod, nb, tpb):
    N, D = x1.shape
    tm = TOKEN_TILE
    n_tiles = N // tm

    def mod_idx(i):
        return (jnp.where(i % tpb == 0, nb, i // tpb), 0, 0)

    kern = functools.partial(_combine_kernel, n_tiles=n_tiles)
    return pl.pallas_call(
        kern,
        grid=(n_tiles,),
        in_specs=[pl.BlockSpec((tm * TOP_K,), lambda i: (i,), memory_space=pltpu.SMEM),
                  pl.BlockSpec((tm * TOP_K,), lambda i: (jnp.minimum(i + 1, n_tiles - 1),),
                               memory_space=pltpu.SMEM),
                  pl.BlockSpec((tm, D), lambda i: (i, 0)),
                  pl.BlockSpec((tm, LANE), lambda i: (i, 0)),
                  pl.BlockSpec((None, 6, D), mod_idx),
                  pl.BlockSpec(memory_space=pl.ANY)],
        out_specs=pl.BlockSpec((tm, D), lambda i: (i, 0)),
        out_shape=jax.ShapeDtypeStruct((N, D), F32),
        scratch_shapes=[pltpu.VMEM((2, TOP_K, tm, D), F32), pltpu.SemaphoreType.DMA((2,))],
        compiler_params=_params(("arbitrary",)),
        name="moe_combine",
    )(dest_flat, dest_flat, x1, top_g, mod, yb)


def _final_kernel(x_ref, g_ref, o_ref):
    x = x_ref[...]
    ms = jnp.mean(x * x, axis=-1, keepdims=True)
    o_ref[...] = x * lax.rsqrt(ms + EPS) * g_ref[...]


def _final_norm(x, g, nb, tpb, S):
    N, D = x.shape
    tm = TOKEN_TILE
    lt = S // tm
    ct = tpb - lt
    out = pl.pallas_call(
        _final_kernel,
        grid=(nb, lt),
        in_specs=[pl.BlockSpec((tm, D), lambda b, i: (b * tpb + ct + i, 0)),
                  pl.BlockSpec((1, D), lambda b, i: (0, 0))],
        out_specs=pl.BlockSpec((tm, D), lambda b, i: (b * lt + i, 0)),
        out_shape=jax.ShapeDtypeStruct((nb * S, D), F32),
        compiler_params=_params(("arbitrary", "arbitrary")),
        name="final_norm",
    )(x, g.reshape(1, D))
    return out.reshape(nb, S, D)


def _rope_tables(S, L):
    half = GLA_DK // 2
    nfreq = half // 2
    inv_freq = ROPE_THETA ** (-jnp.arange(0, half, 2, dtype=F32) / half)
    t = jnp.arange(S)
    pos = jnp.stack([(t // GRID_W).astype(F32), (t % GRID_W).astype(F32)], axis=0)
    d = np.arange(GLA_QK) % GLA_DK
    axis = d // half
    e = d % half
    fidx = e % nfreq
    sign = np.where(e < nfreq, -1.0, 1.0).astype(np.float32)
    ang = pos[axis].T * inv_freq[fidx][None, :]
    cos = jnp.cos(ang)
    sin = jnp.sin(ang) * sign[None, :]
    cos = jnp.concatenate([jnp.ones((L, GLA_QK), F32), cos], axis=0)
    sin = jnp.concatenate([jnp.zeros((L, GLA_QK), F32), sin], axis=0)
    pad = jnp.zeros((L + S, QK_PAD - GLA_QK), F32)
    qs = GLA_DK ** -0.5
    cos_t = jnp.concatenate([cos * qs, pad, cos, pad], axis=1)
    sin_t = jnp.concatenate([sin * qs, pad, sin, pad], axis=1)
    return cos_t, sin_t


def _partner_perm():
    d = np.arange(GLA_QK) % GLA_DK
    e = d % (GLA_DK // 2)
    nfreq = GLA_DK // 4
    return np.where(e < nfreq, np.arange(GLA_QK) + nfreq, np.arange(GLA_QK) - nfreq)


def _layout_w_in(w_in):
    D = w_in.shape[0]
    o = np.cumsum([0, GLA_QK, GLA_QK, GLA_V, GLA_RANK, GLA_RANK, GLA_V, NA_W, NA_W, NA_W, POOL_WIDTH])
    piece = lambda n: w_in[:, o[n]:o[n + 1]]
    perm = _partner_perm()
    z = lambda n: jnp.zeros((D, n), w_in.dtype)
    qpad = z(QK_PAD - GLA_QK)
    cols = [piece(0), qpad, piece(1), qpad,
            piece(0)[:, perm], qpad, piece(1)[:, perm], qpad,
            piece(2), piece(5),
            piece(3), piece(4), z(LANE - 2 * GLA_RANK),
            piece(6), piece(7), piece(8), piece(9)]
    return jnp.concatenate(cols, axis=1).astype(BF16)


def _layout_gate(wa2, ba):
    w = jnp.zeros((2, LANE, GLA_QK), F32)
    w = w.at[0, 0:GLA_RANK].set(wa2[0]).at[1, GLA_RANK:2 * GLA_RANK].set(wa2[1])
    return w, ba.reshape(2, 1, GLA_QK)


def _block_diag(pool_w):
    G = pool_w.shape[0]
    w = jnp.zeros((POOL_WIDTH, POOL_WIDTH), F32)
    for g in range(G):
        w = w.at[g * POOL_GROUP:(g + 1) * POOL_GROUP, g * POOL_GROUP:(g + 1) * POOL_GROUP].set(pool_w[g])
    return w.astype(BF16)


def _routing(top_e, rank, counts, n_blocks):
    bm = MOE_ROWS
    counts = counts[0, :N_EXPERTS]
    padded = (counts + bm - 1) // bm * bm
    pend = jnp.cumsum(padded)
    pstart = pend - padded
    experts = jnp.arange(N_EXPERTS, dtype=jnp.int32)
    te = top_e[:, :TOP_K]
    start_of = jnp.sum(jnp.where(te[:, :, None] == experts[None, None, :], pstart[None, None, :], 0), axis=-1)
    dest = (start_of + rank[:, :TOP_K]).astype(jnp.int32).reshape(-1)
    first_row = jnp.arange(n_blocks, dtype=jnp.int32) * bm
    block_e = jnp.minimum(jnp.sum((pend[None, :] <= first_row[:, None]).astype(jnp.int32), axis=1),
                          N_EXPERTS - 1).astype(jnp.int32)
    n_valid = (pend[-1] // bm).astype(jnp.int32).reshape(1)
    return dest, block_e, n_valid


def kernel(x, c, ctx, c_ctx, norm1_g, norm2_g, w_ada, b_ada, w_in, gla_wa2, gla_ba, gla_norm_g, na_rpb,
           pool_w, pool_scale, w_out, router_w, router_b, w_gu, b_gu, w_down, b_down, final_g):
    B, S, D = x.shape
    L = ctx.shape[1]
    assert L == TOKEN_TILE and S % TOKEN_TILE == 0 and S % GRID_W == 0
    T = L + S
    tpb = T // TOKEN_TILE
    N = B * T
    A = N * TOP_K
    n_blocks = (A + MOE_ROWS - 1) // MOE_ROWS + N_EXPERTS

    xa = jnp.concatenate([ctx, x], axis=1).reshape(N, D)
    cond_rows = 8 * ((B + 1 + 7) // 8)
    cond = jnp.zeros((cond_rows, D), F32).at[:B].set(c).at[B].set(c_ctx)
    cos_t, sin_t = _rope_tables(S, L)

    def layer(xa, p):
        (n1, n2, wada, bada, win, wa2, ba, gng, rpb, pw, psc, wout, rw, rb, wgu, bgu, wdn, bdn) = p
        mod = _modulation(cond, wada, bada).reshape(cond_rows, 6, D)
        wa, bap = _layout_gate(wa2, ba)
        q, k, v, g, a, nq, nk, nv, pu = _input_projection(xa, mod, n1, _layout_w_in(win), cos_t, sin_t, B, tpb)
        gla_o = _gla(q, k, v, g, a, wa, bap, gng, B, T, L)
        na_o = _neighbourhood_attention(nq, nk, nv, _na_bias_table(rpb), B, T, L)
        pool_o = _pool(pu, _block_diag(pw), psc, B, T, L)
        rwp = jnp.zeros((D, LANE), F32).at[:, :N_EXPERTS].set(rw)
        rbp = jnp.full((1, LANE), NEG, F32).at[0, :N_EXPERTS].set(rb)
        x1, h2, top_e, top_g, rank, counts = _output_projection(gla_o, na_o, pool_o, xa, mod, n2,
                                                                wout.astype(BF16), rwp, rbp, B, tpb)
        dest, block_e, n_valid = _routing(top_e, rank, counts, n_blocks)
        xs = _dispatch(dest, h2, n_blocks * MOE_ROWS)
        yb = _moe_blocks(block_e, n_valid, xs, wgu.astype(BF16), bgu, wdn.astype(BF16), bdn)
        return _combine(dest, x1, top_g, yb, mod, B, tpb), None

    params = (norm1_g, norm2_g, w_ada, b_ada, w_in, gla_wa2, gla_ba, gla_norm_g, na_rpb, pool_w, pool_scale,
              w_out, router_w, router_b, w_gu, b_gu, w_down, b_down)
    xa, _ = lax.scan(layer, xa, params)
    return _final_norm(xa, final_g, B, tpb, S)
```

```python
import functools

import numpy as np
import jax
import jax.numpy as jnp
from jax import lax
from jax.experimental import pallas as pl
from jax.experimental.pallas import tpu as pltpu

F32 = jnp.float32
BF16 = jnp.bfloat16

EPS = 1e-6
ROPE_THETA = 10000.0
GRID_W = 64

GLA_HEADS = 4
GLA_DK = 48
GLA_DV = 96
GLA_QK = GLA_HEADS * GLA_DK
GLA_V = GLA_HEADS * GLA_DV
GLA_RANK = 16
GLA_TAU = 16.0
GLA_CHUNK = 64

NA_HEADS = 6
NA_DH = 64
NA_W = NA_HEADS * NA_DH
NA_KH = 8
NA_KW = 16

POOL_WINDOWS = (2, 4, 8, 16)
POOL_WIDTH = 256
POOL_GROUP = 64
POOL_HALO = 8

N_EXPERTS = 32
TOP_K = 4
SWIGLU_LIMIT = 7.0
SWIGLU_ALPHA = 1.702

LANE = 128
TOKEN_TILE = 256
MOE_ROWS = 256
NEG = -1e30
VMEM_LIMIT = 56 * 1024 * 1024

QK_PAD = 256
C_QK = 0
C_SW = C_QK + 2 * QK_PAD
C_V = C_SW + 2 * QK_PAD
C_G = C_V + GLA_V
C_A = C_G + GLA_V
C_NQ = C_A + LANE
C_NK = C_NQ + NA_W
C_NV = C_NK + NA_W
C_PU = C_NV + NA_W
C_END = C_PU + POOL_WIDTH


def _dot(a, b):
    return jnp.dot(a, b, preferred_element_type=F32)


def _dot_nt(a, b):
    return lax.dot_general(a, b, (((1,), (1,)), ((), ())), preferred_element_type=F32)


def _split(x):
    hi = x.astype(BF16)
    lo = (x - hi.astype(F32)).astype(BF16)
    return hi, lo


def _sigmoid(x):
    return 1.0 / (1.0 + jnp.exp(-x))


def _params(sem):
    return pltpu.CompilerParams(dimension_semantics=sem, vmem_limit_bytes=VMEM_LIMIT)


def _mod_kernel(c_ref, w_ref, b_ref, o_ref):
    c = c_ref[...]
    s = c * _sigmoid(c)
    sh, sl = _split(s)
    wh, wl = _split(w_ref[...])
    o_ref[...] = _dot(sh, wh) + _dot(sl, wh) + _dot(sh, wl) + b_ref[...]


def _modulation(cond, w_ada, b_ada):
    R, D = cond.shape
    N = w_ada.shape[1]
    bn = 1024
    return pl.pallas_call(
        _mod_kernel,
        grid=(N // bn,),
        in_specs=[pl.BlockSpec((R, D), lambda j: (0, 0)),
                  pl.BlockSpec((D, bn), lambda j: (0, j)),
                  pl.BlockSpec((1, bn), lambda j: (0, j))],
        out_specs=pl.BlockSpec((R, bn), lambda j: (0, j)),
        out_shape=jax.ShapeDtypeStruct((R, N), F32),
        compiler_params=_params(("arbitrary",)),
        name="adaln_mod",
    )(cond, w_ada, b_ada.reshape(1, N))


def _inproj_kernel(x_ref, mod_ref, g_ref, w_ref, cos_ref, sin_ref, wa_ref, ba_ref, tri_ref,
                   q_ref, k_ref, v_ref, gg_ref, bf_ref, bb_ref, nq_ref, nk_ref, nv_ref, pu_ref):
    x = x_ref[...]
    ms = jnp.mean(x * x, axis=-1, keepdims=True)
    h = x * lax.rsqrt(ms + EPS) * g_ref[...]
    h = h * (1.0 + mod_ref[1:2, :]) + mod_ref[0:1, :]
    hb = h.astype(BF16)

    def proj(lo, hi):
        return _dot(hb, w_ref[:, lo:hi])

    rot = proj(C_QK, C_SW) * cos_ref[...] + proj(C_SW, C_V) * sin_ref[...]
    q_ref[...] = rot[:, 0:GLA_QK].astype(BF16)
    k_ref[...] = rot[:, QK_PAD:QK_PAD + GLA_QK].astype(BF16)
    v_ref[...] = proj(C_V, C_G).astype(BF16)
    gg_ref[...] = proj(C_G, C_A).astype(BF16)
    a = proj(C_A, C_NQ).astype(BF16)
    for d, out_ref in enumerate((bf_ref, bb_ref)):
        z = _dot(a, wa_ref[d]) + ba_ref[d]
        la = (jnp.minimum(z, 0.0) - jnp.log(1.0 + jnp.exp(-jnp.abs(z)))) * (1.0 / GLA_TAU)
        lh, ll = _split(la)
        out_ref[...] = _dot(tri_ref[d], lh) + _dot(tri_ref[d], ll)
    nq_ref[...] = (proj(C_NQ, C_NK) * (NA_DH ** -0.5)).astype(BF16)
    nk_ref[...] = proj(C_NK, C_NV).astype(BF16)
    nv_ref[...] = proj(C_NV, C_PU).astype(BF16)
    pu_ref[...] = proj(C_PU, C_END)


def _chunk_triangles():
    r = np.arange(TOKEN_TILE)[:, None]
    c = np.arange(TOKEN_TILE)[None, :]
    same = (r // GLA_CHUNK) == (c // GLA_CHUNK)
    return jnp.asarray(np.stack([same & (c <= r), same & (c >= r)]).astype(np.float32), dtype=BF16)


def _input_projection(x, mod, norm_g, w, cos_t, sin_t, wa, ba, nb, tpb):
    N, D = x.shape
    tm = TOKEN_TILE

    def mod_idx(i):
        return (jnp.where(i % tpb == 0, nb, i // tpb), 0, 0)

    tok = lambda w_: pl.BlockSpec((tm, w_), lambda i: (i, 0))
    outs = [(GLA_QK, BF16), (GLA_QK, BF16), (GLA_V, BF16), (GLA_V, BF16), (GLA_QK, F32), (GLA_QK, F32),
            (NA_W, BF16), (NA_W, BF16), (NA_W, BF16), (POOL_WIDTH, F32)]
    return pl.pallas_call(
        _inproj_kernel,
        grid=(N // tm,),
        in_specs=[tok(D),
                  pl.BlockSpec((None, 6, D), mod_idx),
                  pl.BlockSpec((1, D), lambda i: (0, 0)),
                  pl.BlockSpec((D, C_END), lambda i: (0, 0)),
                  pl.BlockSpec((tm, 2 * QK_PAD), lambda i: (i % tpb, 0)),
                  pl.BlockSpec((tm, 2 * QK_PAD), lambda i: (i % tpb, 0)),
                  pl.BlockSpec((2, LANE, GLA_QK), lambda i: (0, 0, 0)),
                  pl.BlockSpec((2, 1, GLA_QK), lambda i: (0, 0, 0)),
                  pl.BlockSpec((2, tm, tm), lambda i: (0, 0, 0))],
        out_specs=[tok(w_) for w_, _ in outs],
        out_shape=[jax.ShapeDtypeStruct((N, w_), dt) for w_, dt in outs],
        compiler_params=_params(("arbitrary",)),
        name="in_proj",
    )(x, mod, norm_g.reshape(1, D), w, cos_t, sin_t, wa, ba, _chunk_triangles())


def _lane_head(idx, width):
    h = jnp.zeros_like(idx)
    for t in range(1, GLA_HEADS):
        h = h + (idx >= t * width).astype(jnp.int32)
    return h


def _gla_kernel(q_ref, k_ref, v_ref, g_ref, bf_ref, bb_ref, ng_ref, o_ref,
                stf_ref, stb_ref, of_ref, ob_ref, *, n_ctx, n_all):
    assert n_all % 2 == 0
    C = GLA_CHUNK
    i32 = jnp.int32
    kmask = (lax.broadcasted_iota(i32, (GLA_HEADS * C, GLA_QK), 0) // C
             == _lane_head(lax.broadcasted_iota(i32, (GLA_HEADS * C, GLA_QK), 1), GLA_DK))
    vmask = (lax.broadcasted_iota(i32, (GLA_HEADS * C, GLA_V), 0) // C
             == _lane_head(lax.broadcasted_iota(i32, (GLA_HEADS * C, GLA_V), 1), GLA_DV))
    smask = (_lane_head(lax.broadcasted_iota(i32, (GLA_V, GLA_QK), 0), GLA_DV)
             == _lane_head(lax.broadcasted_iota(i32, (GLA_V, GLA_QK), 1), GLA_DK))
    hind = jnp.where(_lane_head(lax.broadcasted_iota(i32, (GLA_V, GLA_V), 0), GLA_DV)
                     == _lane_head(lax.broadcasted_iota(i32, (GLA_V, GLA_V), 1), GLA_DV),
                     1.0, 0.0).astype(BF16)
    qi_ = lax.broadcasted_iota(i32, (C, GLA_HEADS * C), 0)
    kj_ = lax.broadcasted_iota(i32, (C, GLA_HEADS * C), 1) % C
    causal_f = qi_ >= kj_
    causal_b = qi_ <= kj_

    def chunk(ci, fwd):
        st_ref = stf_ref if fwd else stb_ref
        r0 = pl.multiple_of(ci * C, C)
        rows = pl.ds(r0, C)
        q = q_ref[rows, :].astype(F32)
        k = k_ref[rows, :].astype(F32)
        v = v_ref[rows, :]
        b = (bf_ref if fwd else bb_ref)[rows, :]
        if fwd:
            bmid = b[C // 2 - 1:C // 2, :]
            btot = b[C - 1:C, :]
        else:
            bmid = b[C // 2:C // 2 + 1, :]
            btot = b[0:1, :]
        qe = (q * jnp.exp(b - bmid)).astype(BF16)
        ke = (k * jnp.exp(bmid - b)).astype(BF16)
        qs = (q * jnp.exp(b)).astype(BF16)
        kt = (k * jnp.exp(btot - b)).astype(BF16)
        zero = jnp.zeros((), BF16)
        kbd = jnp.where(kmask, jnp.concatenate([ke] * GLA_HEADS, axis=0), zero)
        vbd = jnp.where(vmask, jnp.concatenate([v] * GLA_HEADS, axis=0), zero)
        sc = _dot_nt(qe, kbd)
        sc = jnp.where(causal_f if fwd else causal_b, sc, 0.0)
        st = st_ref[...]
        o = _dot(sc.astype(BF16), vbd) + _dot_nt(qs, st.astype(BF16))
        vt = v.astype(F32).T.astype(BF16)
        ds = _dot(vt, kt)
        st_ref[...] = st * jnp.exp(btot) + jnp.where(smask, ds, 0.0)
        (of_ref if fwd else ob_ref)[rows, :] = o

    def finish(ci):
        rows = pl.ds(pl.multiple_of(ci * C, C), C)
        tot = of_ref[rows, :] + ob_ref[rows, :]
        th, tl = _split(tot * tot)
        ms = (_dot(th, hind) + _dot(tl, hind)) * (1.0 / GLA_DV)
        g = g_ref[rows, :].astype(F32)
        y = tot * lax.rsqrt(ms + EPS) * ng_ref[...] * (g * _sigmoid(g))
        o_ref[rows, :] = y.astype(o_ref.dtype)

    def both(f0, b0, count):
        def body(i, carry):
            chunk(f0 + i, True)
            chunk(b0 - i, False)
            return carry
        lax.fori_loop(0, count, body, 0)

    stf_ref[...] = jnp.zeros_like(stf_ref)
    stb_ref[...] = jnp.zeros_like(stb_ref)
    both(0, n_ctx - 1, n_ctx)
    both(n_ctx, n_all - 1, n_all - n_ctx)

    def fin_body(i, carry):
        finish(2 * i)
        finish(2 * i + 1)
        return carry
    lax.fori_loop(0, n_all // 2, fin_body, 0)


def _gla(q, k, v, g, bf, bb, norm_g, nb, T, L):
    seq = lambda w_: pl.BlockSpec((T, w_), lambda b: (b, 0))
    once = lambda w_: pl.BlockSpec((T, w_), lambda b: (b, 0), pipeline_mode=pl.Buffered(1))
    kern = functools.partial(_gla_kernel, n_ctx=L // GLA_CHUNK, n_all=T // GLA_CHUNK)
    return pl.pallas_call(
        kern,
        grid=(nb,),
        in_specs=[seq(GLA_QK), seq(GLA_QK), seq(GLA_V), seq(GLA_V), once(GLA_QK), once(GLA_QK),
                  pl.BlockSpec((1, GLA_V), lambda b: (0, 0))],
        out_specs=seq(GLA_V),
        out_shape=jax.ShapeDtypeStruct((nb * T, GLA_V), BF16),
        scratch_shapes=[pltpu.VMEM((GLA_V, GLA_QK), F32), pltpu.VMEM((GLA_V, GLA_QK), F32),
                        pltpu.VMEM((T, GLA_V), F32), pltpu.VMEM((T, GLA_V), F32)],
        compiler_params=_params(("arbitrary",)),
        name="gla",
    )(q, k, v, g, bf, bb, norm_g.reshape(1, GLA_V))


def _na_kernel(q_ref, k_ref, v_ref, bias_ref, o_ref, *, n_ctx_steps, n_rows, L):
    j = pl.program_id(1)
    W = GRID_W
    lane = lax.broadcasted_iota(jnp.int32, (2 * W, LANE), 1)
    row = lax.broadcasted_iota(jnp.int32, (2 * W, LANE), 0)
    qmask = (row < W) == (lane < NA_DH)
    out_lo = lax.broadcasted_iota(jnp.int32, (W, LANE), 1) < NA_DH
    zero = jnp.zeros((), BF16)

    def pair_q(p):
        q2 = q_ref[:, p * LANE:(p + 1) * LANE]
        return jnp.where(qmask, jnp.concatenate([q2, q2], axis=0), zero)

    def finish(p, acc, l):
        o2 = acc / l
        o_ref[:, p * LANE:(p + 1) * LANE] = jnp.where(out_lo, o2[:W], o2[W:]).astype(o_ref.dtype)

    @pl.when(j < n_ctx_steps)
    def _():
        for p in range(NA_HEADS // 2):
            cols = slice(p * LANE, (p + 1) * LANE)
            s = _dot_nt(pair_q(p), k_ref[0:L, cols])
            m = jnp.max(s, axis=-1, keepdims=True)
            e = jnp.exp(s - m)
            l = jnp.sum(e, axis=-1, keepdims=True)
            finish(p, _dot(e.astype(BF16), v_ref[0:L, cols]), l)

    @pl.when(j >= n_ctx_steps)
    def _():
        r = j - n_ctx_steps
        rs = jnp.clip(r - NA_KH // 2, 0, n_rows - NA_KH)
        k0 = pl.multiple_of(L + rs * W, W)
        win = pl.ds(k0, NA_KH * W)
        for p in range(NA_HEADS // 2):
            cols = slice(p * LANE, (p + 1) * LANE)
            qb = pair_q(p)
            s_w = _dot_nt(qb, k_ref[win, cols]) + bias_ref[p]
            s_c = _dot_nt(qb, k_ref[0:L, cols])
            m = jnp.maximum(jnp.max(s_w, axis=-1, keepdims=True), jnp.max(s_c, axis=-1, keepdims=True))
            e_w = jnp.exp(s_w - m)
            e_c = jnp.exp(s_c - m)
            l = jnp.sum(e_w, axis=-1, keepdims=True) + jnp.sum(e_c, axis=-1, keepdims=True)
            acc = _dot(e_w.astype(BF16), v_ref[win, cols]) + _dot(e_c.astype(BF16), v_ref[0:L, cols])
            finish(p, acc, l)


def _na_bias_table(rpb):
    W = GRID_W
    q = np.arange(W)[:, None]
    kc = np.arange(W)[None, :]
    cs = np.clip(q - NA_KW // 2, 0, W - NA_KW)
    valid = (kc >= cs) & (kc < cs + NA_KW)
    cidx = np.clip(kc - q + NA_KW - 1, 0, 2 * NA_KW - 2)
    vi = np.arange(NA_KH)[:, None]
    ridx = vi + np.arange(NA_KH)[None, :]
    t = rpb[:, ridx][:, :, :, cidx]
    t = jnp.where(valid[None, None, None], t, NEG)
    t = jnp.transpose(t, (1, 0, 3, 2, 4))
    return t.reshape(NA_KH, NA_HEADS // 2, 2 * W, NA_KH * W).astype(F32)


def _neighbourhood_attention(nq, nk, nv, bias, nb, T, L):
    W = GRID_W
    n_ctx_steps = L // W
    n_rows = (T - L) // W
    spb = T // W
    def bias_idx(b, j):
        r = j - n_ctx_steps
        rs = jnp.clip(r - NA_KH // 2, 0, n_rows - NA_KH)
        return (jnp.clip(rs - r + NA_KH - 1, 0, NA_KH - 1), 0, 0, 0)

    kern = functools.partial(_na_kernel, n_ctx_steps=n_ctx_steps, n_rows=n_rows, L=L)
    return pl.pallas_call(
        kern,
        grid=(nb, spb),
        in_specs=[pl.BlockSpec((W, NA_W), lambda b, j: (b * spb + j, 0)),
                  pl.BlockSpec((T, NA_W), lambda b, j: (b, 0)),
                  pl.BlockSpec((T, NA_W), lambda b, j: (b, 0)),
                  pl.BlockSpec((None, NA_HEADS // 2, 2 * W, NA_KH * W), bias_idx)],
        out_specs=pl.BlockSpec((W, NA_W), lambda b, j: (b * spb + j, 0)),
        out_shape=jax.ShapeDtypeStruct((nb * T, NA_W), BF16),
        compiler_params=_params(("arbitrary", "arbitrary")),
        name="neigh_attn",
    )(nq, nk, nv, bias)


def _pool_kernel(u_ref, w_ref, sc_ref, o_ref, *, T, L):
    i = pl.program_id(1)
    tm = TOKEN_TILE
    ext = tm + 2 * POOL_HALO
    t0 = i * tm
    e0 = pl.multiple_of(jnp.clip(t0 - POOL_HALO, 0, T - ext), 8)
    seg_lo = jnp.where(t0 < L, 0, L)
    seg_hi = jnp.where(t0 < L, L, T)
    uh, ul = _split(u_ref[pl.ds(e0, ext), :])
    t = t0 + lax.broadcasted_iota(jnp.int32, (tm, ext), 0)
    c = e0 + lax.broadcasted_iota(jnp.int32, (tm, ext), 1)
    tcol = t0 + lax.broadcasted_iota(jnp.int32, (tm, 1), 0)
    lane = lax.broadcasted_iota(jnp.int32, (tm, POOL_WIDTH), 1)
    y = jnp.zeros((tm, POOL_WIDTH), F32)
    for gi, w in enumerate(POOL_WINDOWS):
        lo = jnp.maximum(t - w // 2, seg_lo)
        hi = jnp.minimum(t + w - w // 2, seg_hi)
        band = jnp.where((c >= lo) & (c < hi), 1.0, 0.0).astype(BF16)
        cnt = (jnp.minimum(tcol + w - w // 2, seg_hi) - jnp.maximum(tcol - w // 2, seg_lo)).astype(F32)
        s = (_dot(band, uh) + _dot(band, ul)) / cnt
        y = jnp.where((lane >= gi * POOL_GROUP) & (lane < (gi + 1) * POOL_GROUP), s, y)
    y = y - u_ref[pl.ds(pl.multiple_of(t0, tm), tm), :]
    o_ref[...] = (_dot(y.astype(BF16), w_ref[...]) * sc_ref[...]).astype(o_ref.dtype)


def _pool(u, w_bd, scale, nb, T, L):
    tpb = T // TOKEN_TILE
    kern = functools.partial(_pool_kernel, T=T, L=L)
    return pl.pallas_call(
        kern,
        grid=(nb, tpb),
        in_specs=[pl.BlockSpec((T, POOL_WIDTH), lambda b, i: (b, 0)),
                  pl.BlockSpec((POOL_WIDTH, POOL_WIDTH), lambda b, i: (0, 0)),
                  pl.BlockSpec((1, POOL_WIDTH), lambda b, i: (0, 0))],
        out_specs=pl.BlockSpec((TOKEN_TILE, POOL_WIDTH), lambda b, i: (b * tpb + i, 0)),
        out_shape=jax.ShapeDtypeStruct((nb * T, POOL_WIDTH), BF16),
        compiler_params=_params(("arbitrary", "arbitrary")),
        name="pool_mix",
    )(u, w_bd, scale.reshape(1, POOL_WIDTH))


def _outproj_kernel(gla_ref, na_ref, pm_ref, x_ref, mod_ref, g_ref, wo_ref, rw_ref, rb_ref,
                    x1_ref, h2_ref, te_ref, tg_ref, rk_ref, cnt_ref, carry_ref):
    @pl.when(pl.program_id(0) == 0)
    def _():
        carry_ref[...] = jnp.zeros_like(carry_ref)

    mx = (_dot(gla_ref[...], wo_ref[0:GLA_V, :])
          + _dot(na_ref[...], wo_ref[GLA_V:GLA_V + NA_W, :])
          + _dot(pm_ref[...], wo_ref[GLA_V + NA_W:, :]))
    x1 = x_ref[...] + mod_ref[2:3, :] * mx
    x1_ref[...] = x1
    ms = jnp.mean(x1 * x1, axis=-1, keepdims=True)
    h = x1 * lax.rsqrt(ms + EPS) * g_ref[...]
    h = h * (1.0 + mod_ref[4:5, :]) + mod_ref[3:4, :]
    h2_ref[...] = h
    hh, hl = _split(h)
    wh, wl = _split(rw_ref[...])
    logits = _dot(hh, wh) + _dot(hl, wh) + _dot(hh, wl) + rb_ref[...]
    tm = logits.shape[0]
    lane = lax.broadcasted_iota(jnp.int32, logits.shape, 1)
    te = jnp.zeros(logits.shape, jnp.int32)
    tv = jnp.zeros(logits.shape, F32)
    cur = logits
    vals, idxs = [], []
    for kk in range(TOP_K):
        m = jnp.max(cur, axis=-1, keepdims=True)
        idx = jnp.min(jnp.where(cur == m, lane, LANE), axis=-1, keepdims=True)
        vals.append(m)
        idxs.append(idx)
        te = jnp.where(lane == kk, idx, te)
        cur = jnp.where(lane == idx, NEG * 2.0, cur)
    den = sum(jnp.exp(v - vals[0]) for v in vals)
    for kk in range(TOP_K):
        tv = jnp.where(lane == kk, jnp.exp(vals[kk] - vals[0]) / den, tv)
    te_ref[...] = te
    tg_ref[...] = tv
    hot = sum(jnp.where(lane == idx, 1.0, 0.0) for idx in idxs)
    before = (lax.broadcasted_iota(jnp.int32, (tm, tm), 1) < lax.broadcasted_iota(jnp.int32, (tm, tm), 0))
    base = _dot(jnp.where(before, 1.0, 0.0).astype(BF16), hot.astype(BF16)) + carry_ref[...]
    rk = jnp.zeros(logits.shape, jnp.int32)
    for kk in range(TOP_K):
        r = jnp.sum(jnp.where(lane == idxs[kk], base, 0.0), axis=-1, keepdims=True)
        rk = jnp.where(lane == kk, r.astype(jnp.int32), rk)
    rk_ref[...] = rk
    carry_ref[...] = carry_ref[...] + jnp.sum(hot, axis=0, keepdims=True)
    cnt_ref[...] = jnp.broadcast_to(carry_ref[...], cnt_ref.shape).astype(jnp.int32)


def _output_projection(gla, na, pm, x, mod, norm_g, w_out, rw, rb, nb, tpb):
    N, D = x.shape
    tm = TOKEN_TILE

    def mod_idx(i):
        return (jnp.where(i % tpb == 0, nb, i // tpb), 0, 0)

    tok = lambda w_: pl.BlockSpec((tm, w_), lambda i: (i, 0))
    return pl.pallas_call(
        _outproj_kernel,
        grid=(N // tm,),
        in_specs=[tok(GLA_V), tok(NA_W), tok(POOL_WIDTH), tok(D),
                  pl.BlockSpec((None, 6, D), mod_idx),
                  pl.BlockSpec((1, D), lambda i: (0, 0)),
                  pl.BlockSpec((D, D), lambda i: (0, 0)),
                  pl.BlockSpec((D, LANE), lambda i: (0, 0)),
                  pl.BlockSpec((1, LANE), lambda i: (0, 0))],
        out_specs=[tok(D), tok(D), tok(LANE), tok(LANE), tok(LANE),
                   pl.BlockSpec((8, LANE), lambda i: (0, 0))],
        out_shape=[jax.ShapeDtypeStruct((N, D), F32), jax.ShapeDtypeStruct((N, D), F32),
                   jax.ShapeDtypeStruct((N, LANE), jnp.int32), jax.ShapeDtypeStruct((N, LANE), F32),
                   jax.ShapeDtypeStruct((N, LANE), jnp.int32), jax.ShapeDtypeStruct((8, LANE), jnp.int32)],
        scratch_shapes=[pltpu.VMEM((1, LANE), F32)],
        compiler_params=_params(("arbitrary",)),
        name="out_proj_router",
    )(gla, na, pm, x, mod, norm_g.reshape(1, D), w_out, rw, rb)


SUB = 8


def _row_of(ref, r):
    return ref.at[lax.shift_right_logical(r, 3), pl.ds(jnp.bitwise_and(r, SUB - 1), 1)]


def _dispatch_kernel(plo_ref, phi_ref, nv_ref, dest_ref, h_ref, xs_ref, zrow, zblk, sem, zsem, *, n_tiles):
    groups = h_ref.shape[0]

    def group(g, carry):
        base = g * (SUB * TOP_K)
        for u in range(SUB):
            for kk in range(TOP_K):
                d = dest_ref[base + u * TOP_K + kk]
                pltpu.make_async_copy(h_ref.at[g, pl.ds(u, 1)], _row_of(xs_ref, d), sem).start(priority=kk % 2)
        return carry

    lax.fori_loop(0, groups, group, 0)
    for kk in range(TOP_K):
        pltpu.make_async_copy(h_ref, xs_ref.at[pl.ds(0, groups)], sem).wait()

    @pl.when(pl.program_id(0) == n_tiles - 1)
    def _():
        zrow[...] = jnp.zeros_like(zrow)

        def pad_copy(r):
            return pltpu.make_async_copy(zrow, _row_of(xs_ref, r), zsem)

        def per_expert(e, carry):
            def start(r, c):
                pad_copy(r).start()
                return c

            def wait(r, c):
                pad_copy(r).wait()
                return c

            lax.fori_loop(plo_ref[e], phi_ref[e], start, 0)
            lax.fori_loop(plo_ref[e], phi_ref[e], wait, 0)
            return carry

        lax.fori_loop(0, N_EXPERTS, per_expert, 0)

        zblk[...] = jnp.zeros_like(zblk)
        bg = zblk.shape[0]

        def blk_copy(b):
            return pltpu.make_async_copy(zblk, xs_ref.at[pl.ds(pl.multiple_of(b * bg, bg), bg)], zsem)

        def blk_start(b, c):
            blk_copy(b).start()
            return c

        def blk_wait(b, c):
            blk_copy(b).wait()
            return c

        n_blocks = xs_ref.shape[0] // bg
        lax.fori_loop(nv_ref[0], n_blocks, blk_start, 0)
        lax.fori_loop(nv_ref[0], n_blocks, blk_wait, 0)


def _dispatch(pad_lo, pad_hi, n_valid, dest_flat, h2, n_slots):
    N, D = h2.shape
    tm = TOKEN_TILE
    n_tiles = N // tm
    grid_spec = pltpu.PrefetchScalarGridSpec(
        num_scalar_prefetch=3,
        grid=(n_tiles,),
        in_specs=[pl.BlockSpec((tm * TOP_K,), lambda i, lo, hi, nv: (i,), memory_space=pltpu.SMEM),
                  pl.BlockSpec((tm // SUB, SUB, D), lambda i, lo, hi, nv: (i, 0, 0))],
        out_specs=pl.BlockSpec(memory_space=pl.ANY),
        scratch_shapes=[pltpu.VMEM((1, D), F32), pltpu.VMEM((MOE_ROWS // SUB, SUB, D), F32),
                        pltpu.SemaphoreType.DMA(()), pltpu.SemaphoreType.DMA(())],
    )
    xs = pl.pallas_call(
        functools.partial(_dispatch_kernel, n_tiles=n_tiles),
        grid_spec=grid_spec,
        out_shape=jax.ShapeDtypeStruct((n_slots // SUB, SUB, D), F32),
        compiler_params=_params(("arbitrary",)),
        name="moe_dispatch",
    )(pad_lo, pad_hi, n_valid, dest_flat, h2.reshape(N // SUB, SUB, D))
    return xs.reshape(n_slots, D)


def _moe_kernel(be_ref, nv_ref, xs_ref, wgu_ref, bgu_ref, wd_ref, bd_ref, y_ref):
    i = pl.program_id(0)

    @pl.when(i < nv_ref[0])
    def _():
        dff = wd_ref.shape[0]
        gu = _dot(xs_ref[...].astype(BF16), wgu_ref[...]) + bgu_ref[...]
        g = jnp.minimum(gu[:, :dff], SWIGLU_LIMIT)
        u = jnp.clip(gu[:, dff:], -SWIGLU_LIMIT, SWIGLU_LIMIT)
        act = g * _sigmoid(SWIGLU_ALPHA * g) * (u + 1.0)
        y_ref[...] = _dot(act.astype(BF16), wd_ref[...]) + bd_ref[...]

    @pl.when(i >= nv_ref[0])
    def _():
        y_ref[...] = jnp.zeros_like(y_ref)


def _moe_blocks(block_e, n_valid, xs, w_gu, b_gu, w_down, b_down):
    P, D = xs.shape
    E, _, F2 = w_gu.shape
    bm = MOE_ROWS
    grid_spec = pltpu.PrefetchScalarGridSpec(
        num_scalar_prefetch=2,
        grid=(P // bm,),
        in_specs=[pl.BlockSpec((bm, D), lambda i, be, nv: (i, 0)),
                  pl.BlockSpec((None, D, F2), lambda i, be, nv: (be[i], 0, 0)),
                  pl.BlockSpec((None, 1, F2), lambda i, be, nv: (be[i], 0, 0)),
                  pl.BlockSpec((None, F2 // 2, D), lambda i, be, nv: (be[i], 0, 0)),
                  pl.BlockSpec((None, 1, D), lambda i, be, nv: (be[i], 0, 0))],
        out_specs=pl.BlockSpec((bm, D), lambda i, be, nv: (i, 0)),
    )
    return pl.pallas_call(
        _moe_kernel,
        grid_spec=grid_spec,
        out_shape=jax.ShapeDtypeStruct((P, D), F32),
        compiler_params=_params(("arbitrary",)),
        name="moe_experts",
    )(block_e, n_valid, xs, w_gu, b_gu.reshape(E, 1, F2), w_down, b_down.reshape(E, 1, D))


def _combine_kernel(dcur_ref, dnxt_ref, x_ref, tg_ref, mod_ref, yb_ref, o_ref, ybuf, sem, *, n_tiles):
    i = pl.program_id(0)
    tm = x_ref.shape[0]
    slot = i % 2

    def issue(dref, s):
        def group(g, carry):
            base = g * (SUB * TOP_K)
            for u in range(SUB):
                for kk in range(TOP_K):
                    d = dref[base + u * TOP_K + kk]
                    pltpu.make_async_copy(_row_of(yb_ref, d), ybuf.at[s, kk, g, pl.ds(u, 1)],
                                          sem.at[s]).start(priority=kk % 2)
            return carry
        lax.fori_loop(0, tm // SUB, group, 0)

    @pl.when(i == 0)
    def _():
        issue(dcur_ref, 0)

    @pl.when(i + 1 < n_tiles)
    def _():
        issue(dnxt_ref, 1 - slot)

    for kk in range(TOP_K):
        pltpu.make_async_copy(yb_ref.at[pl.ds(0, tm // SUB)], ybuf.at[slot, kk], sem.at[slot]).wait()
    tg = tg_ref[...]
    rows = lambda kk: ybuf[slot, kk].reshape(tm, x_ref.shape[1])
    y = tg[:, 0:1] * rows(0)
    for kk in range(1, TOP_K):
        y = y + tg[:, kk:kk + 1] * rows(kk)
    o_ref[...] = x_ref[...] + mod_ref[5:6, :] * y


def _combine(dest_flat, x1, top_g, yb, mod, nb, tpb):
    N, D = x1.shape
    tm = TOKEN_TILE
    n_tiles = N // tm

    def mod_idx(i):
        return (jnp.where(i % tpb == 0, nb, i // tpb), 0, 0)

    kern = functools.partial(_combine_kernel, n_tiles=n_tiles)
    return pl.pallas_call(
        kern,
        grid=(n_tiles,),
        in_specs=[pl.BlockSpec((tm * TOP_K,), lambda i: (i,), memory_space=pltpu.SMEM),
                  pl.BlockSpec((tm * TOP_K,), lambda i: (jnp.minimum(i + 1, n_tiles - 1),),
                               memory_space=pltpu.SMEM),
                  pl.BlockSpec((tm, D), lambda i: (i, 0)),
                  pl.BlockSpec((tm, LANE), lambda i: (i, 0)),
                  pl.BlockSpec((None, 6, D), mod_idx),
                  pl.BlockSpec(memory_space=pl.ANY)],
        out_specs=pl.BlockSpec((tm, D), lambda i: (i, 0)),
        out_shape=jax.ShapeDtypeStruct((N, D), F32),
        scratch_shapes=[pltpu.VMEM((2, TOP_K, tm // SUB, SUB, D), F32), pltpu.SemaphoreType.DMA((2,))],
        compiler_params=_params(("arbitrary",)),
        name="moe_combine",
    )(dest_flat, dest_flat, x1, top_g, mod, yb.reshape(yb.shape[0] // SUB, SUB, D))


def _final_kernel(x_ref, g_ref, o_ref):
    x = x_ref[...]
    ms = jnp.mean(x * x, axis=-1, keepdims=True)
    o_ref[...] = x * lax.rsqrt(ms + EPS) * g_ref[...]


def _final_norm(x, g, nb, tpb, S):
    N, D = x.shape
    tm = TOKEN_TILE
    lt = S // tm
    ct = tpb - lt
    out = pl.pallas_call(
        _final_kernel,
        grid=(nb, lt),
        in_specs=[pl.BlockSpec((tm, D), lambda b, i: (b * tpb + ct + i, 0)),
                  pl.BlockSpec((1, D), lambda b, i: (0, 0))],
        out_specs=pl.BlockSpec((tm, D), lambda b, i: (b * lt + i, 0)),
        out_shape=jax.ShapeDtypeStruct((nb * S, D), F32),
        compiler_params=_params(("arbitrary", "arbitrary")),
        name="final_norm",
    )(x, g.reshape(1, D))
    return out.reshape(nb, S, D)


def _rope_tables(S, L):
    half = GLA_DK // 2
    nfreq = half // 2
    inv_freq = ROPE_THETA ** (-jnp.arange(0, half, 2, dtype=F32) / half)
    t = jnp.arange(S)
    pos = jnp.stack([(t // GRID_W).astype(F32), (t % GRID_W).astype(F32)], axis=0)
    d = np.arange(GLA_QK) % GLA_DK
    axis = d // half
    e = d % half
    fidx = e % nfreq
    sign = np.where(e < nfreq, -1.0, 1.0).astype(np.float32)
    ang = pos[axis].T * inv_freq[fidx][None, :]
    cos = jnp.cos(ang)
    sin = jnp.sin(ang) * sign[None, :]
    cos = jnp.concatenate([jnp.ones((L, GLA_QK), F32), cos], axis=0)
    sin = jnp.concatenate([jnp.zeros((L, GLA_QK), F32), sin], axis=0)
    pad = jnp.zeros((L + S, QK_PAD - GLA_QK), F32)
    qs = GLA_DK ** -0.5
    cos_t = jnp.concatenate([cos * qs, pad, cos, pad], axis=1)
    sin_t = jnp.concatenate([sin * qs, pad, sin, pad], axis=1)
    return cos_t, sin_t


def _partner_perm():
    d = np.arange(GLA_QK) % GLA_DK
    e = d % (GLA_DK // 2)
    nfreq = GLA_DK // 4
    return np.where(e < nfreq, np.arange(GLA_QK) + nfreq, np.arange(GLA_QK) - nfreq)


def _layout_w_in(w_in):
    D = w_in.shape[0]
    o = np.cumsum([0, GLA_QK, GLA_QK, GLA_V, GLA_RANK, GLA_RANK, GLA_V, NA_W, NA_W, NA_W, POOL_WIDTH])
    piece = lambda n: w_in[:, o[n]:o[n + 1]]
    perm = _partner_perm()
    z = lambda n: jnp.zeros((D, n), w_in.dtype)
    qpad = z(QK_PAD - GLA_QK)
    cols = [piece(0), qpad, piece(1), qpad,
            piece(0)[:, perm], qpad, piece(1)[:, perm], qpad,
            piece(2), piece(5),
            piece(3), piece(4), z(LANE - 2 * GLA_RANK),
            piece(6), piece(7), piece(8), piece(9)]
    return jnp.concatenate(cols, axis=1).astype(BF16)


def _layout_gate(wa2, ba):
    w = jnp.zeros((2, LANE, GLA_QK), F32)
    w = w.at[0, 0:GLA_RANK].set(wa2[0]).at[1, GLA_RANK:2 * GLA_RANK].set(wa2[1])
    return w.astype(BF16), ba.reshape(2, 1, GLA_QK)


def _block_diag(pool_w):
    G = pool_w.shape[0]
    w = jnp.zeros((POOL_WIDTH, POOL_WIDTH), F32)
    for g in range(G):
        w = w.at[g * POOL_GROUP:(g + 1) * POOL_GROUP, g * POOL_GROUP:(g + 1) * POOL_GROUP].set(pool_w[g])
    return w.astype(BF16)


def _routing(top_e, rank, counts, n_blocks):
    bm = MOE_ROWS
    counts = counts[0, :N_EXPERTS]
    padded = (counts + bm - 1) // bm * bm
    pend = jnp.cumsum(padded)
    pstart = pend - padded
    experts = jnp.arange(N_EXPERTS, dtype=jnp.int32)
    te = top_e[:, :TOP_K]
    start_of = jnp.sum(jnp.where(te[:, :, None] == experts[None, None, :], pstart[None, None, :], 0), axis=-1)
    dest = (start_of + rank[:, :TOP_K]).astype(jnp.int32).reshape(-1)
    first_row = jnp.arange(n_blocks, dtype=jnp.int32) * bm
    block_e = jnp.minimum(jnp.sum((pend[None, :] <= first_row[:, None]).astype(jnp.int32), axis=1),
                          N_EXPERTS - 1).astype(jnp.int32)
    n_valid = (pend[-1] // bm).astype(jnp.int32).reshape(1)
    pad_lo = (pstart + counts).astype(jnp.int32)
    return dest, block_e, n_valid, pad_lo, pend.astype(jnp.int32)


def kernel(x, c, ctx, c_ctx, norm1_g, norm2_g, w_ada, b_ada, w_in, gla_wa2, gla_ba, gla_norm_g, na_rpb,
           pool_w, pool_scale, w_out, router_w, router_b, w_gu, b_gu, w_down, b_down, final_g):
    B, S, D = x.shape
    L = ctx.shape[1]
    assert L == TOKEN_TILE and S % TOKEN_TILE == 0 and S % GRID_W == 0
    T = L + S
    tpb = T // TOKEN_TILE
    N = B * T
    A = N * TOP_K
    n_blocks = (A + MOE_ROWS - 1) // MOE_ROWS + N_EXPERTS

    xa = jnp.concatenate([ctx, x], axis=1).reshape(N, D)
    cond_rows = 8 * ((B + 1 + 7) // 8)
    cond = jnp.zeros((cond_rows, D), F32).at[:B].set(c).at[B].set(c_ctx)
    cos_t, sin_t = _rope_tables(S, L)

    def layer(xa, p):
        (n1, n2, wada, bada, win, wa2, ba, gng, rpb, pw, psc, wout, rw, rb, wgu, bgu, wdn, bdn) = p
        mod = _modulation(cond, wada, bada).reshape(cond_rows, 6, D)
        wa, bap = _layout_gate(wa2, ba)
        q, k, v, g, bf, bb, nq, nk, nv, pu = _input_projection(xa, mod, n1, _layout_w_in(win), cos_t, sin_t,
                                                               wa, bap, B, tpb)
        gla_o = _gla(q, k, v, g, bf, bb, gng, B, T, L)
        na_o = _neighbourhood_attention(nq, nk, nv, _na_bias_table(rpb), B, T, L)
        pool_o = _pool(pu, _block_diag(pw), psc, B, T, L)
        rwp = jnp.zeros((D, LANE), F32).at[:, :N_EXPERTS].set(rw)
        rbp = jnp.full((1, LANE), NEG, F32).at[0, :N_EXPERTS].set(rb)
        x1, h2, top_e, top_g, rank, counts = _output_projection(gla_o, na_o, pool_o, xa, mod, n2,
                                                                wout.astype(BF16), rwp, rbp, B, tpb)
        dest, block_e, n_valid, pad_lo, pad_hi = _routing(top_e, rank, counts, n_blocks)
        xs = _dispatch(pad_lo, pad_hi, n_valid, dest, h2, n_blocks * MOE_ROWS)
        yb = _moe_blocks(block_e, n_valid, xs, wgu.astype(BF16), bgu, wdn.astype(BF16), bdn)
        return _combine(dest, x1, top_g, yb, mod, B, tpb), None

    params = (norm1_g, norm2_g, w_ada, b_ada, w_in, gla_wa2, gla_ba, gla_norm_g, na_rpb, pool_w, pool_scale,
              w_out, router_w, router_b, w_gu, b_gu, w_down, b_down)
    xa, _ = lax.scan(layer, xa, params)
    return _final_norm(xa, final_g, B, tpb, S)
```

```python
import functools

import numpy as np
import jax
import jax.numpy as jnp
from jax import lax
from jax.experimental import pallas as pl
from jax.experimental.pallas import tpu as pltpu

F32 = jnp.float32
BF16 = jnp.bfloat16

EPS = 1e-6
ROPE_THETA = 10000.0
GRID_W = 64

GLA_HEADS = 4
GLA_DK = 48
GLA_DV = 96
GLA_QK = GLA_HEADS * GLA_DK
GLA_V = GLA_HEADS * GLA_DV
GLA_RANK = 16
GLA_TAU = 16.0
GLA_CHUNK = 64

NA_HEADS = 6
NA_DH = 64
NA_W = NA_HEADS * NA_DH
NA_KH = 8
NA_KW = 16

POOL_WINDOWS = (2, 4, 8, 16)
POOL_WIDTH = 256
POOL_GROUP = 64
POOL_HALO = 8

N_EXPERTS = 32
TOP_K = 4
SWIGLU_LIMIT = 7.0
SWIGLU_ALPHA = 1.702

LANE = 128
TOKEN_TILE = 256
MOE_ROWS = 256
NEG = -1e30
VMEM_LIMIT = 56 * 1024 * 1024

QK_PAD = 256
C_QK = 0
C_SW = C_QK + 2 * QK_PAD
C_V = C_SW + 2 * QK_PAD
C_G = C_V + GLA_V
C_A = C_G + GLA_V
C_NQ = C_A + LANE
C_NK = C_NQ + NA_W
C_NV = C_NK + NA_W
C_PU = C_NV + NA_W
C_END = C_PU + POOL_WIDTH


def _dot(a, b):
    return jnp.dot(a, b, preferred_element_type=F32)


def _dot_nt(a, b):
    return lax.dot_general(a, b, (((1,), (1,)), ((), ())), preferred_element_type=F32)


def _split(x):
    hi = x.astype(BF16)
    lo = (x - hi.astype(F32)).astype(BF16)
    return hi, lo


def _sigmoid(x):
    return 1.0 / (1.0 + jnp.exp(-x))


def _params(sem):
    return pltpu.CompilerParams(dimension_semantics=sem, vmem_limit_bytes=VMEM_LIMIT)


def _mod_kernel(c_ref, w_ref, b_ref, o_ref):
    c = c_ref[...]
    s = c * _sigmoid(c)
    sh, sl = _split(s)
    wh, wl = _split(w_ref[...])
    o_ref[...] = _dot(sh, wh) + _dot(sl, wh) + _dot(sh, wl) + b_ref[...]


def _modulation(cond, w_ada, b_ada):
    R, D = cond.shape
    N = w_ada.shape[1]
    bn = 1024
    return pl.pallas_call(
        _mod_kernel,
        grid=(N // bn,),
        in_specs=[pl.BlockSpec((R, D), lambda j: (0, 0)),
                  pl.BlockSpec((D, bn), lambda j: (0, j)),
                  pl.BlockSpec((1, bn), lambda j: (0, j))],
        out_specs=pl.BlockSpec((R, bn), lambda j: (0, j)),
        out_shape=jax.ShapeDtypeStruct((R, N), F32),
        compiler_params=_params(("arbitrary",)),
        name="adaln_mod",
    )(cond, w_ada, b_ada.reshape(1, N))


def _inproj_kernel(x_ref, mod_ref, g_ref, w_ref, cos_ref, sin_ref, wa_ref, ba_ref, tri_ref,
                   q_ref, k_ref, v_ref, gg_ref, bf_ref, bb_ref, nq_ref, nk_ref, nv_ref, pu_ref):
    x = x_ref[...]
    ms = jnp.mean(x * x, axis=-1, keepdims=True)
    h = x * lax.rsqrt(ms + EPS) * g_ref[...]
    h = h * (1.0 + mod_ref[1:2, :]) + mod_ref[0:1, :]
    hb = h.astype(BF16)

    def proj(lo, hi):
        return _dot(hb, w_ref[:, lo:hi])

    rot = proj(C_QK, C_SW) * cos_ref[...] + proj(C_SW, C_V) * sin_ref[...]
    q_ref[...] = rot[:, 0:GLA_QK].astype(BF16)
    k_ref[...] = rot[:, QK_PAD:QK_PAD + GLA_QK].astype(BF16)
    v_ref[...] = proj(C_V, C_G).astype(BF16)
    gg_ref[...] = proj(C_G, C_A).astype(BF16)
    a = proj(C_A, C_NQ).astype(BF16)
    for d, out_ref in enumerate((bf_ref, bb_ref)):
        z = _dot(a, wa_ref[d]) + ba_ref[d]
        la = (jnp.minimum(z, 0.0) - jnp.log(1.0 + jnp.exp(-jnp.abs(z)))) * (1.0 / GLA_TAU)
        lh, ll = _split(la)
        out_ref[...] = _dot(tri_ref[d], lh) + _dot(tri_ref[d], ll)
    nq_ref[...] = (proj(C_NQ, C_NK) * (NA_DH ** -0.5)).astype(BF16)
    nk_ref[...] = proj(C_NK, C_NV).astype(BF16)
    nv_ref[...] = proj(C_NV, C_PU).astype(BF16)
    pu_ref[...] = proj(C_PU, C_END)


def _chunk_triangles():
    r = np.arange(TOKEN_TILE)[:, None]
    c = np.arange(TOKEN_TILE)[None, :]
    same = (r // GLA_CHUNK) == (c // GLA_CHUNK)
    return jnp.asarray(np.stack([same & (c <= r), same & (c >= r)]).astype(np.float32), dtype=BF16)


def _input_projection(x, mod, norm_g, w, cos_t, sin_t, wa, ba, nb, tpb):
    N, D = x.shape
    tm = TOKEN_TILE

    def mod_idx(i):
        return (jnp.where(i % tpb == 0, nb, i // tpb), 0, 0)

    tok = lambda w_: pl.BlockSpec((tm, w_), lambda i: (i, 0))
    outs = [(GLA_QK, BF16), (GLA_QK, BF16), (GLA_V, BF16), (GLA_V, BF16), (GLA_QK, F32), (GLA_QK, F32),
            (NA_W, BF16), (NA_W, BF16), (NA_W, BF16), (POOL_WIDTH, F32)]
    return pl.pallas_call(
        _inproj_kernel,
        grid=(N // tm,),
        in_specs=[tok(D),
                  pl.BlockSpec((None, 6, D), mod_idx),
                  pl.BlockSpec((1, D), lambda i: (0, 0)),
                  pl.BlockSpec((D, C_END), lambda i: (0, 0)),
                  pl.BlockSpec((tm, 2 * QK_PAD), lambda i: (i % tpb, 0)),
                  pl.BlockSpec((tm, 2 * QK_PAD), lambda i: (i % tpb, 0)),
                  pl.BlockSpec((2, LANE, GLA_QK), lambda i: (0, 0, 0)),
                  pl.BlockSpec((2, 1, GLA_QK), lambda i: (0, 0, 0)),
                  pl.BlockSpec((2, tm, tm), lambda i: (0, 0, 0))],
        out_specs=[tok(w_) for w_, _ in outs],
        out_shape=[jax.ShapeDtypeStruct((N, w_), dt) for w_, dt in outs],
        compiler_params=_params(("arbitrary",)),
        name="in_proj",
    )(x, mod, norm_g.reshape(1, D), w, cos_t, sin_t, wa, ba, _chunk_triangles())


def _lane_head(idx, width):
    h = jnp.zeros_like(idx)
    for t in range(1, GLA_HEADS):
        h = h + (idx >= t * width).astype(jnp.int32)
    return h


def _gla_kernel(q_ref, k_ref, v_ref, g_ref, bf_ref, bb_ref, ng_ref, o_ref,
                stf_ref, stb_ref, of_ref, ob_ref, *, n_ctx, n_all):
    assert n_all % 2 == 0
    C = GLA_CHUNK
    i32 = jnp.int32
    kmask = (lax.broadcasted_iota(i32, (GLA_HEADS * C, GLA_QK), 0) // C
             == _lane_head(lax.broadcasted_iota(i32, (GLA_HEADS * C, GLA_QK), 1), GLA_DK))
    vmask = (lax.broadcasted_iota(i32, (GLA_HEADS * C, GLA_V), 0) // C
             == _lane_head(lax.broadcasted_iota(i32, (GLA_HEADS * C, GLA_V), 1), GLA_DV))
    smask = (_lane_head(lax.broadcasted_iota(i32, (GLA_V, GLA_QK), 0), GLA_DV)
             == _lane_head(lax.broadcasted_iota(i32, (GLA_V, GLA_QK), 1), GLA_DK))
    hind = jnp.where(_lane_head(lax.broadcasted_iota(i32, (GLA_V, GLA_V), 0), GLA_DV)
                     == _lane_head(lax.broadcasted_iota(i32, (GLA_V, GLA_V), 1), GLA_DV),
                     1.0, 0.0).astype(BF16)
    qi_ = lax.broadcasted_iota(i32, (C, GLA_HEADS * C), 0)
    kj_ = lax.broadcasted_iota(i32, (C, GLA_HEADS * C), 1) % C
    causal_f = qi_ >= kj_
    causal_b = qi_ <= kj_

    def chunk(ci, fwd):
        st_ref = stf_ref if fwd else stb_ref
        r0 = pl.multiple_of(ci * C, C)
        rows = pl.ds(r0, C)
        q = q_ref[rows, :].astype(F32)
        k = k_ref[rows, :].astype(F32)
        v = v_ref[rows, :]
        b = (bf_ref if fwd else bb_ref)[rows, :]
        if fwd:
            bmid = b[C // 2 - 1:C // 2, :]
            btot = b[C - 1:C, :]
        else:
            bmid = b[C // 2:C // 2 + 1, :]
            btot = b[0:1, :]
        qe = (q * jnp.exp(b - bmid)).astype(BF16)
        ke = (k * jnp.exp(bmid - b)).astype(BF16)
        qs = (q * jnp.exp(b)).astype(BF16)
        kt = (k * jnp.exp(btot - b)).astype(BF16)
        zero = jnp.zeros((), BF16)
        kbd = jnp.where(kmask, jnp.concatenate([ke] * GLA_HEADS, axis=0), zero)
        vbd = jnp.where(vmask, jnp.concatenate([v] * GLA_HEADS, axis=0), zero)
        sc = _dot_nt(qe, kbd)
        sc = jnp.where(causal_f if fwd else causal_b, sc, 0.0)
        st = st_ref[...]
        o = _dot(sc.astype(BF16), vbd) + _dot_nt(qs, st.astype(BF16))
        vt = v.astype(F32).T.astype(BF16)
        ds = _dot(vt, kt)
        st_ref[...] = st * jnp.exp(btot) + jnp.where(smask, ds, 0.0)
        (of_ref if fwd else ob_ref)[rows, :] = o

    def finish(ci):
        rows = pl.ds(pl.multiple_of(ci * C, C), C)
        tot = of_ref[rows, :] + ob_ref[rows, :]
        th, tl = _split(tot * tot)
        ms = (_dot(th, hind) + _dot(tl, hind)) * (1.0 / GLA_DV)
        g = g_ref[rows, :].astype(F32)
        y = tot * lax.rsqrt(ms + EPS) * ng_ref[...] * (g * _sigmoid(g))
        o_ref[rows, :] = y.astype(o_ref.dtype)

    def both(f0, b0, count):
        def body(i, carry):
            chunk(f0 + i, True)
            chunk(b0 - i, False)
            return carry
        lax.fori_loop(0, count, body, 0)

    stf_ref[...] = jnp.zeros_like(stf_ref)
    stb_ref[...] = jnp.zeros_like(stb_ref)
    both(0, n_ctx - 1, n_ctx)
    both(n_ctx, n_all - 1, n_all - n_ctx)

    def fin_body(i, carry):
        finish(2 * i)
        finish(2 * i + 1)
        return carry
    lax.fori_loop(0, n_all // 2, fin_body, 0)


def _gla(q, k, v, g, bf, bb, norm_g, nb, T, L):
    seq = lambda w_: pl.BlockSpec((T, w_), lambda b: (b, 0))
    once = lambda w_: pl.BlockSpec((T, w_), lambda b: (b, 0), pipeline_mode=pl.Buffered(1))
    kern = functools.partial(_gla_kernel, n_ctx=L // GLA_CHUNK, n_all=T // GLA_CHUNK)
    return pl.pallas_call(
        kern,
        grid=(nb,),
        in_specs=[seq(GLA_QK), seq(GLA_QK), seq(GLA_V), seq(GLA_V), once(GLA_QK), once(GLA_QK),
                  pl.BlockSpec((1, GLA_V), lambda b: (0, 0))],
        out_specs=seq(GLA_V),
        out_shape=jax.ShapeDtypeStruct((nb * T, GLA_V), BF16),
        scratch_shapes=[pltpu.VMEM((GLA_V, GLA_QK), F32), pltpu.VMEM((GLA_V, GLA_QK), F32),
                        pltpu.VMEM((T, GLA_V), F32), pltpu.VMEM((T, GLA_V), F32)],
        compiler_params=_params(("arbitrary",)),
        name="gla",
    )(q, k, v, g, bf, bb, norm_g.reshape(1, GLA_V))


NA_GROUP = TOKEN_TILE // GRID_W
NA_WIN = NA_KH + NA_GROUP - 1


def _na_window_start(group, n_rows):
    return jnp.clip(group * NA_GROUP - NA_KH // 2, 0, n_rows - NA_WIN)


def _na_kernel(q_ref, k_ref, v_ref, bias_ref, o_ref, *, n_rows, L):
    j = pl.program_id(1)
    W = GRID_W
    Q = q_ref.shape[0]
    lane = lax.broadcasted_iota(jnp.int32, (2 * Q, LANE), 1)
    row = lax.broadcasted_iota(jnp.int32, (2 * Q, LANE), 0)
    qmask = (row < Q) == (lane < NA_DH)
    out_lo = lax.broadcasted_iota(jnp.int32, (Q, LANE), 1) < NA_DH
    zero = jnp.zeros((), BF16)

    def pair_q(p):
        q2 = q_ref[:, p * LANE:(p + 1) * LANE]
        return jnp.where(qmask, jnp.concatenate([q2, q2], axis=0), zero)

    def finish(p, acc, l):
        o2 = acc / l
        o_ref[:, p * LANE:(p + 1) * LANE] = jnp.where(out_lo, o2[:Q], o2[Q:]).astype(o_ref.dtype)

    @pl.when(j == 0)
    def _():
        for p in range(NA_HEADS // 2):
            cols = slice(p * LANE, (p + 1) * LANE)
            s = _dot_nt(pair_q(p), k_ref[0:L, cols])
            m = jnp.max(s, axis=-1, keepdims=True)
            e = jnp.exp(s - m)
            l = jnp.sum(e, axis=-1, keepdims=True)
            finish(p, _dot(e.astype(BF16), v_ref[0:L, cols]), l)

    @pl.when(j > 0)
    def _():
        k0 = pl.multiple_of(L + _na_window_start(j - 1, n_rows) * W, W)
        win = pl.ds(k0, NA_WIN * W)
        for p in range(NA_HEADS // 2):
            cols = slice(p * LANE, (p + 1) * LANE)
            qb = pair_q(p)
            s_w = _dot_nt(qb, k_ref[win, cols]) + bias_ref[p]
            s_c = _dot_nt(qb, k_ref[0:L, cols])
            m = jnp.maximum(jnp.max(s_w, axis=-1, keepdims=True), jnp.max(s_c, axis=-1, keepdims=True))
            e_w = jnp.exp(s_w - m)
            e_c = jnp.exp(s_c - m)
            l = jnp.sum(e_w, axis=-1, keepdims=True) + jnp.sum(e_c, axis=-1, keepdims=True)
            acc = _dot(e_w.astype(BF16), v_ref[win, cols]) + _dot(e_c.astype(BF16), v_ref[0:L, cols])
            finish(p, acc, l)


def _na_bias_table(rpb, n_rows):
    W, G = GRID_W, NA_GROUP
    n_groups = n_rows // G
    assert n_rows >= NA_WIN and n_groups >= 3
    qc = np.arange(W)[:, None]
    kc = np.arange(W)[None, :]
    cs = np.clip(qc - NA_KW // 2, 0, W - NA_KW)
    cvalid = (kc >= cs) & (kc < cs + NA_KW)
    cidx = np.clip(kc - qc + NA_KW - 1, 0, 2 * NA_KW - 2)
    rvalid, ridx = [], []
    for grp in (0, 1, n_groups - 1):
        ws = int(np.clip(grp * G - NA_KH // 2, 0, n_rows - NA_WIN))
        r = (grp * G + np.arange(G))[:, None]
        rs = np.clip(r - NA_KH // 2, 0, n_rows - NA_KH)
        key_row = (ws + np.arange(NA_WIN))[None, :]
        rvalid.append((key_row >= rs) & (key_row < rs + NA_KH))
        ridx.append(np.clip(key_row - r + NA_KH - 1, 0, 2 * NA_KH - 2))
    e_r = jnp.asarray(np.eye(2 * NA_KH - 1, dtype=np.float32)[np.stack(ridx)])
    e_c = jnp.asarray(np.eye(2 * NA_KW - 1, dtype=np.float32)[cidx])
    t = jnp.einsum('vria,hab,qkb->vhrqik', e_r, rpb.astype(F32), e_c, precision=lax.Precision.HIGHEST)
    valid = np.stack(rvalid)[:, None, :, None, :, None] & cvalid[None, None, None, :, None, :]
    t = jnp.where(jnp.asarray(valid), t, NEG)
    return t.reshape(3, NA_HEADS // 2, 2 * G * W, NA_WIN * W)


def _neighbourhood_attention(nq, nk, nv, bias, nb, T, L):
    W = GRID_W
    n_rows = (T - L) // W
    n_groups = n_rows // NA_GROUP
    Q = NA_GROUP * W
    spb = T // Q

    def bias_idx(b, j):
        return (jnp.where(j <= 1, 0, jnp.where(j == n_groups, 2, 1)), 0, 0, 0)

    kern = functools.partial(_na_kernel, n_rows=n_rows, L=L)
    return pl.pallas_call(
        kern,
        grid=(nb, spb),
        in_specs=[pl.BlockSpec((Q, NA_W), lambda b, j: (b * spb + j, 0)),
                  pl.BlockSpec((T, NA_W), lambda b, j: (b, 0)),
                  pl.BlockSpec((T, NA_W), lambda b, j: (b, 0)),
                  pl.BlockSpec((None, NA_HEADS // 2, 2 * Q, NA_WIN * W), bias_idx)],
        out_specs=pl.BlockSpec((Q, NA_W), lambda b, j: (b * spb + j, 0)),
        out_shape=jax.ShapeDtypeStruct((nb * T, NA_W), BF16),
        compiler_params=_params(("arbitrary", "arbitrary")),
        name="neigh_attn",
    )(nq, nk, nv, bias)


def _pool_kernel(u_ref, w_ref, sc_ref, o_ref, *, T, L):
    i = pl.program_id(1)
    tm = TOKEN_TILE
    ext = tm + 2 * POOL_HALO
    t0 = i * tm
    e0 = pl.multiple_of(jnp.clip(t0 - POOL_HALO, 0, T - ext), 8)
    seg_lo = jnp.where(t0 < L, 0, L)
    seg_hi = jnp.where(t0 < L, L, T)
    uh, ul = _split(u_ref[pl.ds(e0, ext), :])
    t = t0 + lax.broadcasted_iota(jnp.int32, (tm, ext), 0)
    c = e0 + lax.broadcasted_iota(jnp.int32, (tm, ext), 1)
    tcol = t0 + lax.broadcasted_iota(jnp.int32, (tm, 1), 0)
    lane = lax.broadcasted_iota(jnp.int32, (tm, POOL_WIDTH), 1)
    y = jnp.zeros((tm, POOL_WIDTH), F32)
    for gi, w in enumerate(POOL_WINDOWS):
        lo = jnp.maximum(t - w // 2, seg_lo)
        hi = jnp.minimum(t + w - w // 2, seg_hi)
        band = jnp.where((c >= lo) & (c < hi), 1.0, 0.0).astype(BF16)
        cnt = (jnp.minimum(tcol + w - w // 2, seg_hi) - jnp.maximum(tcol - w // 2, seg_lo)).astype(F32)
        s = (_dot(band, uh) + _dot(band, ul)) / cnt
        y = jnp.where((lane >= gi * POOL_GROUP) & (lane < (gi + 1) * POOL_GROUP), s, y)
    y = y - u_ref[pl.ds(pl.multiple_of(t0, tm), tm), :]
    o_ref[...] = (_dot(y.astype(BF16), w_ref[...]) * sc_ref[...]).astype(o_ref.dtype)


def _pool(u, w_bd, scale, nb, T, L):
    tpb = T // TOKEN_TILE
    kern = functools.partial(_pool_kernel, T=T, L=L)
    return pl.pallas_call(
        kern,
        grid=(nb, tpb),
        in_specs=[pl.BlockSpec((T, POOL_WIDTH), lambda b, i: (b, 0)),
                  pl.BlockSpec((POOL_WIDTH, POOL_WIDTH), lambda b, i: (0, 0)),
                  pl.BlockSpec((1, POOL_WIDTH), lambda b, i: (0, 0))],
        out_specs=pl.BlockSpec((TOKEN_TILE, POOL_WIDTH), lambda b, i: (b * tpb + i, 0)),
        out_shape=jax.ShapeDtypeStruct((nb * T, POOL_WIDTH), BF16),
        compiler_params=_params(("arbitrary", "arbitrary")),
        name="pool_mix",
    )(u, w_bd, scale.reshape(1, POOL_WIDTH))


def _outproj_kernel(gla_ref, na_ref, pm_ref, x_ref, mod_ref, g_ref, wo_ref, rw_ref, rb_ref,
                    x1_ref, h2_ref, te_ref, tg_ref, rk_ref, cnt_ref, carry_ref):
    @pl.when(pl.program_id(0) == 0)
    def _():
        carry_ref[...] = jnp.zeros_like(carry_ref)

    mx = (_dot(gla_ref[...], wo_ref[0:GLA_V, :])
          + _dot(na_ref[...], wo_ref[GLA_V:GLA_V + NA_W, :])
          + _dot(pm_ref[...], wo_ref[GLA_V + NA_W:, :]))
    x1 = x_ref[...] + mod_ref[2:3, :] * mx
    x1_ref[...] = x1
    ms = jnp.mean(x1 * x1, axis=-1, keepdims=True)
    h = x1 * lax.rsqrt(ms + EPS) * g_ref[...]
    h = h * (1.0 + mod_ref[4:5, :]) + mod_ref[3:4, :]
    h2_ref[...] = h
    hh, hl = _split(h)
    wh, wl = _split(rw_ref[...])
    logits = _dot(hh, wh) + _dot(hl, wh) + _dot(hh, wl) + rb_ref[...]
    tm = logits.shape[0]
    E = N_EXPERTS
    lt = logits.T[0:E, :]
    erow = lax.broadcasted_iota(jnp.int32, (E, tm), 0)
    cur = lt
    vals, idxs = [], []
    for kk in range(TOP_K):
        m = jnp.max(cur, axis=0, keepdims=True)
        idx = jnp.min(jnp.where(cur == m, erow, E), axis=0, keepdims=True)
        vals.append(m)
        idxs.append(idx)
        cur = jnp.where(erow == idx, NEG, cur)
    expv = [jnp.exp(v - vals[0]) for v in vals]
    den = sum(expv)
    hot = sum(jnp.where(erow == idx, 1.0, 0.0) for idx in idxs)
    before = (lax.broadcasted_iota(jnp.int32, (tm, tm), 0) < lax.broadcasted_iota(jnp.int32, (tm, tm), 1))
    base = _dot(hot.astype(BF16), jnp.where(before, 1.0, 0.0).astype(BF16)) + carry_ref[...]
    krow = lax.broadcasted_iota(jnp.int32, (SUB, tm), 0)
    te = jnp.zeros((SUB, tm), jnp.int32)
    rk = jnp.zeros((SUB, tm), jnp.int32)
    tv = jnp.zeros((SUB, tm), F32)
    for kk in range(TOP_K):
        r = jnp.sum(jnp.where(erow == idxs[kk], base, 0.0), axis=0, keepdims=True)
        te = jnp.where(krow == kk, idxs[kk], te)
        rk = jnp.where(krow == kk, r.astype(jnp.int32), rk)
        tv = jnp.where(krow == kk, expv[kk] / den, tv)
    te_ref[...] = te
    rk_ref[...] = rk
    tg_ref[...] = tv
    carry_ref[...] = carry_ref[...] + jnp.sum(hot, axis=1, keepdims=True)
    cnt_ref[...] = jnp.broadcast_to(carry_ref[...], cnt_ref.shape).astype(jnp.int32)


def _output_projection(gla, na, pm, x, mod, norm_g, w_out, rw, rb, nb, tpb):
    N, D = x.shape
    tm = TOKEN_TILE

    def mod_idx(i):
        return (jnp.where(i % tpb == 0, nb, i // tpb), 0, 0)

    tok = lambda w_: pl.BlockSpec((tm, w_), lambda i: (i, 0))
    n_tiles = N // tm
    route = pl.BlockSpec((SUB, tm), lambda i: (i, 0))
    return pl.pallas_call(
        _outproj_kernel,
        grid=(n_tiles,),
        in_specs=[tok(GLA_V), tok(NA_W), tok(POOL_WIDTH), tok(D),
                  pl.BlockSpec((None, 6, D), mod_idx),
                  pl.BlockSpec((1, D), lambda i: (0, 0)),
                  pl.BlockSpec((D, D), lambda i: (0, 0)),
                  pl.BlockSpec((D, LANE), lambda i: (0, 0)),
                  pl.BlockSpec((1, LANE), lambda i: (0, 0))],
        out_specs=[tok(D), tok(D), route, route, route,
                   pl.BlockSpec((N_EXPERTS, LANE), lambda i: (0, 0))],
        out_shape=[jax.ShapeDtypeStruct((N, D), F32), jax.ShapeDtypeStruct((N, D), F32),
                   jax.ShapeDtypeStruct((n_tiles * SUB, tm), jnp.int32),
                   jax.ShapeDtypeStruct((n_tiles * SUB, tm), F32),
                   jax.ShapeDtypeStruct((n_tiles * SUB, tm), jnp.int32),
                   jax.ShapeDtypeStruct((N_EXPERTS, LANE), jnp.int32)],
        scratch_shapes=[pltpu.VMEM((N_EXPERTS, 1), F32)],
        compiler_params=_params(("arbitrary",)),
        name="out_proj_router",
    )(gla, na, pm, x, mod, norm_g.reshape(1, D), w_out, rw, rb)


SUB = 8


def _row_of(ref, r):
    return ref.at[lax.shift_right_logical(r, 3), pl.ds(jnp.bitwise_and(r, SUB - 1), 1)]


def _dispatch_kernel(plo_ref, phi_ref, nv_ref, dest_ref, h_ref, xs_ref, zrow, zblk, sem, zsem, *, n_tiles):
    groups = h_ref.shape[0]

    def group(g, carry):
        base = g * (SUB * TOP_K)
        for u in range(SUB):
            for kk in range(TOP_K):
                d = dest_ref[base + u * TOP_K + kk]
                pltpu.make_async_copy(h_ref.at[g, pl.ds(u, 1)], _row_of(xs_ref, d), sem).start(priority=kk % 2)
        return carry

    lax.fori_loop(0, groups, group, 0)
    for kk in range(TOP_K):
        pltpu.make_async_copy(h_ref, xs_ref.at[pl.ds(0, groups)], sem).wait()

    @pl.when(pl.program_id(0) == n_tiles - 1)
    def _():
        zrow[...] = jnp.zeros_like(zrow)

        def pad_copy(r):
            return pltpu.make_async_copy(zrow, _row_of(xs_ref, r), zsem)

        def per_expert(e, carry):
            def start(r, c):
                pad_copy(r).start()
                return c

            def wait(r, c):
                pad_copy(r).wait()
                return c

            lax.fori_loop(plo_ref[e], phi_ref[e], start, 0)
            lax.fori_loop(plo_ref[e], phi_ref[e], wait, 0)
            return carry

        lax.fori_loop(0, N_EXPERTS, per_expert, 0)

        zblk[...] = jnp.zeros_like(zblk)
        bg = zblk.shape[0]

        def blk_copy(b):
            return pltpu.make_async_copy(zblk, xs_ref.at[pl.ds(pl.multiple_of(b * bg, bg), bg)], zsem)

        def blk_start(b, c):
            blk_copy(b).start()
            return c

        def blk_wait(b, c):
            blk_copy(b).wait()
            return c

        n_blocks = xs_ref.shape[0] // bg
        lax.fori_loop(nv_ref[0], n_blocks, blk_start, 0)
        lax.fori_loop(nv_ref[0], n_blocks, blk_wait, 0)


def _dispatch(pad_lo, pad_hi, n_valid, dest_flat, h2, n_slots):
    N, D = h2.shape
    tm = TOKEN_TILE
    n_tiles = N // tm
    grid_spec = pltpu.PrefetchScalarGridSpec(
        num_scalar_prefetch=3,
        grid=(n_tiles,),
        in_specs=[pl.BlockSpec((tm * TOP_K,), lambda i, lo, hi, nv: (i,), memory_space=pltpu.SMEM),
                  pl.BlockSpec((tm // SUB, SUB, D), lambda i, lo, hi, nv: (i, 0, 0))],
        out_specs=pl.BlockSpec(memory_space=pl.ANY),
        scratch_shapes=[pltpu.VMEM((1, D), F32), pltpu.VMEM((MOE_ROWS // SUB, SUB, D), F32),
                        pltpu.SemaphoreType.DMA(()), pltpu.SemaphoreType.DMA(())],
    )
    xs = pl.pallas_call(
        functools.partial(_dispatch_kernel, n_tiles=n_tiles),
        grid_spec=grid_spec,
        out_shape=jax.ShapeDtypeStruct((n_slots // SUB, SUB, D), F32),
        compiler_params=_params(("arbitrary",)),
        name="moe_dispatch",
    )(pad_lo, pad_hi, n_valid, dest_flat, h2.reshape(N // SUB, SUB, D))
    return xs.reshape(n_slots, D)


def _moe_kernel(be_ref, nv_ref, xs_ref, wgu_ref, bgu_ref, wd_ref, bd_ref, y_ref):
    i = pl.program_id(0)

    @pl.when(i < nv_ref[0])
    def _():
        dff = wd_ref.shape[0]
        gu = _dot(xs_ref[...].astype(BF16), wgu_ref[...]) + bgu_ref[...]
        g = jnp.minimum(gu[:, :dff], SWIGLU_LIMIT)
        u = jnp.clip(gu[:, dff:], -SWIGLU_LIMIT, SWIGLU_LIMIT)
        act = g * _sigmoid(SWIGLU_ALPHA * g) * (u + 1.0)
        y_ref[...] = _dot(act.astype(BF16), wd_ref[...]) + bd_ref[...]

    @pl.when(i >= nv_ref[0])
    def _():
        y_ref[...] = jnp.zeros_like(y_ref)


def _moe_blocks(block_e, n_valid, xs, w_gu, b_gu, w_down, b_down):
    P, D = xs.shape
    E, _, F2 = w_gu.shape
    bm = MOE_ROWS
    grid_spec = pltpu.PrefetchScalarGridSpec(
        num_scalar_prefetch=2,
        grid=(P // bm,),
        in_specs=[pl.BlockSpec((bm, D), lambda i, be, nv: (i, 0)),
                  pl.BlockSpec((None, D, F2), lambda i, be, nv: (be[i], 0, 0)),
                  pl.BlockSpec((None, 1, F2), lambda i, be, nv: (be[i], 0, 0)),
                  pl.BlockSpec((None, F2 // 2, D), lambda i, be, nv: (be[i], 0, 0)),
                  pl.BlockSpec((None, 1, D), lambda i, be, nv: (be[i], 0, 0))],
        out_specs=pl.BlockSpec((bm, D), lambda i, be, nv: (i, 0)),
    )
    return pl.pallas_call(
        _moe_kernel,
        grid_spec=grid_spec,
        out_shape=jax.ShapeDtypeStruct((P, D), F32),
        compiler_params=_params(("arbitrary",)),
        name="moe_experts",
    )(block_e, n_valid, xs, w_gu, b_gu.reshape(E, 1, F2), w_down, b_down.reshape(E, 1, D))


def _combine_kernel(dcur_ref, dnxt_ref, x_ref, tg_ref, mod_ref, yb_ref, o_ref, ybuf, sem, *, n_tiles):
    i = pl.program_id(0)
    tm = x_ref.shape[0]
    slot = i % 2

    def issue(dref, s):
        def group(g, carry):
            base = g * (SUB * TOP_K)
            for u in range(SUB):
                for kk in range(TOP_K):
                    d = dref[base + u * TOP_K + kk]
                    pltpu.make_async_copy(_row_of(yb_ref, d), ybuf.at[s, kk, g, pl.ds(u, 1)],
                                          sem.at[s]).start(priority=kk % 2)
            return carry
        lax.fori_loop(0, tm // SUB, group, 0)

    @pl.when(i == 0)
    def _():
        issue(dcur_ref, 0)

    @pl.when(i + 1 < n_tiles)
    def _():
        issue(dnxt_ref, 1 - slot)

    for kk in range(TOP_K):
        pltpu.make_async_copy(yb_ref.at[pl.ds(0, tm // SUB)], ybuf.at[slot, kk], sem.at[slot]).wait()
    tg = tg_ref[...]
    rows = lambda kk: ybuf[slot, kk].reshape(tm, x_ref.shape[1])
    y = tg[:, 0:1] * rows(0)
    for kk in range(1, TOP_K):
        y = y + tg[:, kk:kk + 1] * rows(kk)
    o_ref[...] = x_ref[...] + mod_ref[5:6, :] * y


def _combine(dest_flat, x1, top_g, yb, mod, nb, tpb):
    N, D = x1.shape
    tm = TOKEN_TILE
    n_tiles = N // tm

    def mod_idx(i):
        return (jnp.where(i % tpb == 0, nb, i // tpb), 0, 0)

    kern = functools.partial(_combine_kernel, n_tiles=n_tiles)
    return pl.pallas_call(
        kern,
        grid=(n_tiles,),
        in_specs=[pl.BlockSpec((tm * TOP_K,), lambda i: (i,), memory_space=pltpu.SMEM),
                  pl.BlockSpec((tm * TOP_K,), lambda i: (jnp.minimum(i + 1, n_tiles - 1),),
                               memory_space=pltpu.SMEM),
                  pl.BlockSpec((tm, D), lambda i: (i, 0)),
                  pl.BlockSpec((tm, SUB), lambda i: (i, 0)),
                  pl.BlockSpec((None, 6, D), mod_idx),
                  pl.BlockSpec(memory_space=pl.ANY)],
        out_specs=pl.BlockSpec((tm, D), lambda i: (i, 0)),
        out_shape=jax.ShapeDtypeStruct((N, D), F32),
        scratch_shapes=[pltpu.VMEM((2, TOP_K, tm // SUB, SUB, D), F32), pltpu.SemaphoreType.DMA((2,))],
        compiler_params=_params(("arbitrary",)),
        name="moe_combine",
    )(dest_flat, dest_flat, x1, top_g, mod, yb.reshape(yb.shape[0] // SUB, SUB, D))


def _final_kernel(x_ref, g_ref, o_ref):
    x = x_ref[...]
    ms = jnp.mean(x * x, axis=-1, keepdims=True)
    o_ref[...] = x * lax.rsqrt(ms + EPS) * g_ref[...]


def _final_norm(x, g, nb, tpb, S):
    N, D = x.shape
    tm = TOKEN_TILE
    lt = S // tm
    ct = tpb - lt
    out = pl.pallas_call(
        _final_kernel,
        grid=(nb, lt),
        in_specs=[pl.BlockSpec((tm, D), lambda b, i: (b * tpb + ct + i, 0)),
                  pl.BlockSpec((1, D), lambda b, i: (0, 0))],
        out_specs=pl.BlockSpec((tm, D), lambda b, i: (b * lt + i, 0)),
        out_shape=jax.ShapeDtypeStruct((nb * S, D), F32),
        compiler_params=_params(("arbitrary", "arbitrary")),
        name="final_norm",
    )(x, g.reshape(1, D))
    return out.reshape(nb, S, D)


def _rope_tables(S, L):
    half = GLA_DK // 2
    nfreq = half // 2
    inv_freq = ROPE_THETA ** (-jnp.arange(0, half, 2, dtype=F32) / half)
    t = jnp.arange(S)
    pos = jnp.stack([(t // GRID_W).astype(F32), (t % GRID_W).astype(F32)], axis=0)
    d = np.arange(GLA_QK) % GLA_DK
    axis = d // half
    e = d % half
    fidx = e % nfreq
    sign = np.where(e < nfreq, -1.0, 1.0).astype(np.float32)
    ang = pos[axis].T * inv_freq[fidx][None, :]
    cos = jnp.cos(ang)
    sin = jnp.sin(ang) * sign[None, :]
    cos = jnp.concatenate([jnp.ones((L, GLA_QK), F32), cos], axis=0)
    sin = jnp.concatenate([jnp.zeros((L, GLA_QK), F32), sin], axis=0)
    pad = jnp.zeros((L + S, QK_PAD - GLA_QK), F32)
    qs = GLA_DK ** -0.5
    cos_t = jnp.concatenate([cos * qs, pad, cos, pad], axis=1)
    sin_t = jnp.concatenate([sin * qs, pad, sin, pad], axis=1)
    return cos_t, sin_t


def _partner_perm():
    d = np.arange(GLA_QK) % GLA_DK
    e = d % (GLA_DK // 2)
    nfreq = GLA_DK // 4
    return np.where(e < nfreq, np.arange(GLA_QK) + nfreq, np.arange(GLA_QK) - nfreq)


def _layout_w_in(w_in):
    D = w_in.shape[0]
    o = np.cumsum([0, GLA_QK, GLA_QK, GLA_V, GLA_RANK, GLA_RANK, GLA_V, NA_W, NA_W, NA_W, POOL_WIDTH])
    piece = lambda n: w_in[:, o[n]:o[n + 1]]
    perm = _partner_perm()
    z = lambda n: jnp.zeros((D, n), w_in.dtype)
    qpad = z(QK_PAD - GLA_QK)
    cols = [piece(0), qpad, piece(1), qpad,
            piece(0)[:, perm], qpad, piece(1)[:, perm], qpad,
            piece(2), piece(5),
            piece(3), piece(4), z(LANE - 2 * GLA_RANK),
            piece(6), piece(7), piece(8), piece(9)]
    return jnp.concatenate(cols, axis=1).astype(BF16)


def _layout_gate(wa2, ba):
    w = jnp.zeros((2, LANE, GLA_QK), F32)
    w = w.at[0, 0:GLA_RANK].set(wa2[0]).at[1, GLA_RANK:2 * GLA_RANK].set(wa2[1])
    return w.astype(BF16), ba.reshape(2, 1, GLA_QK)


def _block_diag(pool_w):
    G = pool_w.shape[0]
    w = jnp.zeros((POOL_WIDTH, POOL_WIDTH), F32)
    for g in range(G):
        w = w.at[g * POOL_GROUP:(g + 1) * POOL_GROUP, g * POOL_GROUP:(g + 1) * POOL_GROUP].set(pool_w[g])
    return w.astype(BF16)


def _routing(top_e, rank, counts, n_blocks):
    bm = MOE_ROWS
    counts = counts[:, 0]
    padded = (counts + bm - 1) // bm * bm
    pend = jnp.cumsum(padded)
    pstart = pend - padded
    tm = top_e.shape[1]
    te = top_e.reshape(-1, SUB, tm)[:, :TOP_K]
    rk = rank.reshape(-1, SUB, tm)[:, :TOP_K]
    start_of = jnp.zeros_like(te)
    for e in range(N_EXPERTS):
        start_of = jnp.where(te == e, pstart[e], start_of)
    dest = jnp.transpose(start_of + rk, (0, 2, 1)).astype(jnp.int32).reshape(-1)
    first_row = jnp.arange(n_blocks, dtype=jnp.int32) * bm
    block_e = jnp.minimum(jnp.sum((pend[None, :] <= first_row[:, None]).astype(jnp.int32), axis=1),
                          N_EXPERTS - 1).astype(jnp.int32)
    n_valid = (pend[-1] // bm).astype(jnp.int32).reshape(1)
    pad_lo = (pstart + counts).astype(jnp.int32)
    return dest, block_e, n_valid, pad_lo, pend.astype(jnp.int32)


def kernel(x, c, ctx, c_ctx, norm1_g, norm2_g, w_ada, b_ada, w_in, gla_wa2, gla_ba, gla_norm_g, na_rpb,
           pool_w, pool_scale, w_out, router_w, router_b, w_gu, b_gu, w_down, b_down, final_g):
    B, S, D = x.shape
    L = ctx.shape[1]
    assert L == TOKEN_TILE and S % TOKEN_TILE == 0 and S % GRID_W == 0
    T = L + S
    tpb = T // TOKEN_TILE
    N = B * T
    A = N * TOP_K
    n_blocks = (A + MOE_ROWS - 1) // MOE_ROWS + N_EXPERTS

    xa = jnp.concatenate([ctx, x], axis=1).reshape(N, D)
    cond_rows = 8 * ((B + 1 + 7) // 8)
    cond = jnp.zeros((cond_rows, D), F32).at[:B].set(c).at[B].set(c_ctx)
    cos_t, sin_t = _rope_tables(S, L)

    def layer(xa, p):
        (n1, n2, wada, bada, win, wa2, ba, gng, rpb, pw, psc, wout, rw, rb, wgu, bgu, wdn, bdn) = p
        mod = _modulation(cond, wada, bada).reshape(cond_rows, 6, D)
        wa, bap = _layout_gate(wa2, ba)
        q, k, v, g, bf, bb, nq, nk, nv, pu = _input_projection(xa, mod, n1, _layout_w_in(win), cos_t, sin_t,
                                                               wa, bap, B, tpb)
        gla_o = _gla(q, k, v, g, bf, bb, gng, B, T, L)
        na_o = _neighbourhood_attention(nq, nk, nv, _na_bias_table(rpb, S // GRID_W), B, T, L)
        pool_o = _pool(pu, _block_diag(pw), psc, B, T, L)
        rwp = jnp.zeros((D, LANE), F32).at[:, :N_EXPERTS].set(rw)
        rbp = jnp.full((1, LANE), NEG, F32).at[0, :N_EXPERTS].set(rb)
        x1, h2, top_e, top_g, rank, counts = _output_projection(gla_o, na_o, pool_o, xa, mod, n2,
                                                                wout.astype(BF16), rwp, rbp, B, tpb)
        dest, block_e, n_valid, pad_lo, pad_hi = _routing(top_e, rank, counts, n_blocks)
        xs = _dispatch(pad_lo, pad_hi, n_valid, dest, h2, n_blocks * MOE_ROWS)
        yb = _moe_blocks(block_e, n_valid, xs, wgu.astype(BF16), bgu, wdn.astype(BF16), bdn)
        gates = jnp.transpose(top_g.reshape(-1, SUB, TOKEN_TILE), (0, 2, 1)).reshape(N, SUB)
        return _combine(dest, x1, gates, yb, mod, B, tpb), None

    params = (norm1_g, norm2_g, w_ada, b_ada, w_in, gla_wa2, gla_ba, gla_norm_g, na_rpb, pool_w, pool_scale,
              w_out, router_w, router_b, w_gu, b_gu, w_down, b_down)
    xa, _ = lax.scan(layer, xa, params)
    return _final_norm(xa, final_g, B, tpb, S)
```

```python
import functools

import numpy as np
import jax
import jax.numpy as jnp
from jax import lax
from jax.experimental import pallas as pl
from jax.experimental.pallas import tpu as pltpu

F32 = jnp.float32
BF16 = jnp.bfloat16

EPS = 1e-6
ROPE_THETA = 10000.0
GRID_W = 64

GLA_HEADS = 4
GLA_DK = 48
GLA_DV = 96
GLA_QK = GLA_HEADS * GLA_DK
GLA_V = GLA_HEADS * GLA_DV
GLA_RANK = 16
GLA_TAU = 16.0
GLA_CHUNK = 64

NA_HEADS = 6
NA_DH = 64
NA_W = NA_HEADS * NA_DH
NA_KH = 8
NA_KW = 16

POOL_WINDOWS = (2, 4, 8, 16)
POOL_WIDTH = 256
POOL_GROUP = 64
POOL_HALO = 8

N_EXPERTS = 32
TOP_K = 4
SWIGLU_LIMIT = 7.0
SWIGLU_ALPHA = 1.702

LANE = 128
TOKEN_TILE = 256
MOE_ROWS = 256
NEG = -1e30
VMEM_LIMIT = 56 * 1024 * 1024

QK_PAD = 256
C_QK = 0
C_SW = C_QK + 2 * QK_PAD
C_V = C_SW + 2 * QK_PAD
C_G = C_V + GLA_V
C_A = C_G + GLA_V
C_NQ = C_A + LANE
C_NK = C_NQ + NA_W
C_NV = C_NK + NA_W
C_PU = C_NV + NA_W
C_END = C_PU + POOL_WIDTH


def _dot(a, b):
    return jnp.dot(a, b, preferred_element_type=F32)


def _dot_nt(a, b):
    return lax.dot_general(a, b, (((1,), (1,)), ((), ())), preferred_element_type=F32)


def _split(x):
    hi = x.astype(BF16)
    lo = (x - hi.astype(F32)).astype(BF16)
    return hi, lo


def _sigmoid(x):
    return 1.0 / (1.0 + jnp.exp(-x))


def _params(sem):
    return pltpu.CompilerParams(dimension_semantics=sem, vmem_limit_bytes=VMEM_LIMIT)


def _mod_kernel(c_ref, w_ref, b_ref, o_ref):
    c = c_ref[...]
    s = c * _sigmoid(c)
    sh, sl = _split(s)
    wh, wl = _split(w_ref[...])
    o_ref[...] = _dot(sh, wh) + _dot(sl, wh) + _dot(sh, wl) + b_ref[...]


def _modulation(cond, w_ada, b_ada):
    R, D = cond.shape
    N = w_ada.shape[1]
    bn = 1024
    return pl.pallas_call(
        _mod_kernel,
        grid=(N // bn,),
        in_specs=[pl.BlockSpec((R, D), lambda j: (0, 0)),
                  pl.BlockSpec((D, bn), lambda j: (0, j)),
                  pl.BlockSpec((1, bn), lambda j: (0, j))],
        out_specs=pl.BlockSpec((R, bn), lambda j: (0, j)),
        out_shape=jax.ShapeDtypeStruct((R, N), F32),
        compiler_params=_params(("arbitrary",)),
        name="adaln_mod",
    )(cond, w_ada, b_ada.reshape(1, N))


def _inproj_kernel(dcur_ref, dnxt_ref, x1_ref, tg_ref, pmod_ref, yb_ref,
                   mod_ref, g_ref, w_ref, cos_ref, sin_ref, wa_ref, ba_ref, tri_ref,
                   x_ref, q_ref, k_ref, v_ref, gg_ref, bf_ref, bb_ref, nq_ref, nk_ref, nv_ref, pu_ref,
                   ybuf, hb_ref, sem, *, n_tiles):
    i = pl.program_id(0)
    tm, D = x1_ref.shape
    slot = i % 2

    def row_copy(dref, s, g, u, kk, idx):
        return pltpu.make_async_copy(yb_ref.at[pl.ds(dref[idx], 1)], ybuf.at[s, kk, pl.ds(g * SUB + u, 1)], sem.at[s])

    @pl.when(i == 0)
    def _():
        def group(g, carry):
            for u in range(SUB):
                for kk in range(TOP_K):
                    row_copy(dcur_ref, 0, g, u, kk, g * (SUB * TOP_K) + u * TOP_K + kk).start()
            return carry
        lax.fori_loop(0, tm // SUB, group, 0)

    for kk in range(TOP_K):
        pltpu.make_async_copy(yb_ref.at[pl.ds(0, tm)], ybuf.at[slot, kk], sem.at[slot]).wait()

    def p_combine():
        tg = tg_ref[...]
        y = tg[:, 0:1] * ybuf[slot, 0]
        for kk in range(1, TOP_K):
            y = y + tg[:, kk:kk + 1] * ybuf[slot, kk]
        x = x1_ref[...] + pmod_ref[5:6, :] * y
        x_ref[...] = x
        ms = jnp.mean(x * x, axis=-1, keepdims=True)
        h = x * lax.rsqrt(ms + EPS) * g_ref[...]
        h = h * (1.0 + mod_ref[1:2, :]) + mod_ref[0:1, :]
        hb_ref[...] = h.astype(BF16)

    def proj(lo, hi):
        return _dot(hb_ref[...], w_ref[:, lo:hi])

    def p_qk():
        rot = proj(C_QK, C_SW) * cos_ref[...] + proj(C_SW, C_V) * sin_ref[...]
        q_ref[...] = rot[:, 0:GLA_QK].astype(BF16)
        k_ref[...] = rot[:, QK_PAD:QK_PAD + GLA_QK].astype(BF16)

    def p_v():
        v_ref[...] = proj(C_V, C_G).astype(BF16)

    def p_g():
        gg_ref[...] = proj(C_G, C_A).astype(BF16)

    def p_decay():
        a = proj(C_A, C_NQ).astype(BF16)
        for d, out_ref in enumerate((bf_ref, bb_ref)):
            z = _dot(a, wa_ref[d]) + ba_ref[d]
            la = (jnp.minimum(z, 0.0) - jnp.log(1.0 + jnp.exp(-jnp.abs(z)))) * (1.0 / GLA_TAU)
            lh, ll = _split(la)
            out_ref[...] = _dot(tri_ref[d], lh) + _dot(tri_ref[d], ll)

    def p_nq():
        nq_ref[...] = (proj(C_NQ, C_NK) * (NA_DH ** -0.5)).astype(BF16)

    def p_nk():
        nk_ref[...] = proj(C_NK, C_NV).astype(BF16)

    def p_nv():
        nv_ref[...] = proj(C_NV, C_PU).astype(BF16)

    def p_pu():
        pu_ref[...] = proj(C_PU, C_END)

    pieces = ((p_combine, 4), (p_qk, 6), (p_v, 3), (p_g, 3), (p_decay, 4), (p_nq, 3), (p_nk, 3), (p_nv, 3), (p_pu, 3))
    assert sum(share for _, share in pieces) == tm // SUB
    g0 = 0
    for n, (piece, share) in enumerate(pieces):
        @pl.when(dnxt_ref[n] >= 0)
        def _(piece=piece, g0=g0, share=share):
            piece()
            for g in range(g0, g0 + share):
                for u in range(SUB):
                    for kk in range(TOP_K):
                        row_copy(dnxt_ref, 1 - slot, g, u, kk,
                                 g * (SUB * TOP_K) + u * TOP_K + kk).start(priority=kk % 2)
        g0 += share

    @pl.when(i == n_tiles - 1)
    def _():
        for kk in range(TOP_K):
            pltpu.make_async_copy(yb_ref.at[pl.ds(0, tm)], ybuf.at[1 - slot, kk], sem.at[1 - slot]).wait()


def _chunk_triangles():
    r = np.arange(TOKEN_TILE)[:, None]
    c = np.arange(TOKEN_TILE)[None, :]
    same = (r // GLA_CHUNK) == (c // GLA_CHUNK)
    return jnp.asarray(np.stack([same & (c <= r), same & (c >= r)]).astype(np.float32), dtype=BF16)


def _input_projection(pending, mod, norm_g, w, cos_t, sin_t, wa, ba, nb, tpb):
    dest, x1, gates, prev_mod, yb = pending
    N, D = x1.shape
    tm = TOKEN_TILE
    n_tiles = N // tm

    def mod_idx(i):
        return (jnp.where(i % tpb == 0, nb, i // tpb), 0, 0)

    tok = lambda w_: pl.BlockSpec((tm, w_), lambda i: (i, 0))
    outs = [(D, F32), (GLA_QK, BF16), (GLA_QK, BF16), (GLA_V, BF16), (GLA_V, BF16), (GLA_QK, F32), (GLA_QK, F32),
            (NA_W, BF16), (NA_W, BF16), (NA_W, BF16), (POOL_WIDTH, F32)]
    return pl.pallas_call(
        functools.partial(_inproj_kernel, n_tiles=n_tiles),
        grid=(n_tiles,),
        in_specs=[pl.BlockSpec((tm * TOP_K,), lambda i: (i,), memory_space=pltpu.SMEM),
                  pl.BlockSpec((tm * TOP_K,), lambda i: (jnp.minimum(i + 1, n_tiles - 1),),
                               memory_space=pltpu.SMEM),
                  tok(D),
                  pl.BlockSpec((tm, SUB), lambda i: (i, 0)),
                  pl.BlockSpec((None, 6, D), mod_idx),
                  pl.BlockSpec(memory_space=pl.ANY),
                  pl.BlockSpec((None, 6, D), mod_idx),
                  pl.BlockSpec((1, D), lambda i: (0, 0)),
                  pl.BlockSpec((D, C_END), lambda i: (0, 0)),
                  pl.BlockSpec((tm, 2 * QK_PAD), lambda i: (i % tpb, 0)),
                  pl.BlockSpec((tm, 2 * QK_PAD), lambda i: (i % tpb, 0)),
                  pl.BlockSpec((2, LANE, GLA_QK), lambda i: (0, 0, 0)),
                  pl.BlockSpec((2, 1, GLA_QK), lambda i: (0, 0, 0)),
                  pl.BlockSpec((2, tm, tm), lambda i: (0, 0, 0))],
        out_specs=[tok(w_) for w_, _ in outs],
        out_shape=[jax.ShapeDtypeStruct((N, w_), dt) for w_, dt in outs],
        scratch_shapes=[pltpu.VMEM((2, TOP_K, tm, D), F32), pltpu.VMEM((tm, D), BF16),
                        pltpu.SemaphoreType.DMA((2,))],
        compiler_params=_params(("arbitrary",)),
        name="in_proj",
    )(dest, dest, x1, gates, prev_mod, yb,
      mod, norm_g.reshape(1, D), w, cos_t, sin_t, wa, ba, _chunk_triangles())


def _lane_head(idx, width):
    h = jnp.zeros_like(idx)
    for t in range(1, GLA_HEADS):
        h = h + (idx >= t * width).astype(jnp.int32)
    return h


def _gla_kernel(q_ref, k_ref, v_ref, g_ref, bf_ref, bb_ref, ng_ref, o_ref,
                stf_ref, stb_ref, of_ref, ob_ref, *, n_ctx, n_all):
    assert n_all % 2 == 0
    C = GLA_CHUNK
    i32 = jnp.int32
    kmask = (lax.broadcasted_iota(i32, (GLA_HEADS * C, GLA_QK), 0) // C
             == _lane_head(lax.broadcasted_iota(i32, (GLA_HEADS * C, GLA_QK), 1), GLA_DK))
    vmask = (lax.broadcasted_iota(i32, (GLA_HEADS * C, GLA_V), 0) // C
             == _lane_head(lax.broadcasted_iota(i32, (GLA_HEADS * C, GLA_V), 1), GLA_DV))
    smask = (_lane_head(lax.broadcasted_iota(i32, (GLA_V, GLA_QK), 0), GLA_DV)
             == _lane_head(lax.broadcasted_iota(i32, (GLA_V, GLA_QK), 1), GLA_DK))
    hind = jnp.where(_lane_head(lax.broadcasted_iota(i32, (GLA_V, GLA_V), 0), GLA_DV)
                     == _lane_head(lax.broadcasted_iota(i32, (GLA_V, GLA_V), 1), GLA_DV),
                     1.0, 0.0).astype(BF16)
    qi_ = lax.broadcasted_iota(i32, (C, GLA_HEADS * C), 0)
    kj_ = lax.broadcasted_iota(i32, (C, GLA_HEADS * C), 1) % C
    causal_f = qi_ >= kj_
    causal_b = qi_ <= kj_

    def chunk(ci, fwd):
        st_ref = stf_ref if fwd else stb_ref
        r0 = pl.multiple_of(ci * C, C)
        rows = pl.ds(r0, C)
        q = q_ref[rows, :].astype(F32)
        k = k_ref[rows, :].astype(F32)
        v = v_ref[rows, :]
        b = (bf_ref if fwd else bb_ref)[rows, :]
        if fwd:
            bmid = b[C // 2 - 1:C // 2, :]
            btot = b[C - 1:C, :]
        else:
            bmid = b[C // 2:C // 2 + 1, :]
            btot = b[0:1, :]
        qe = (q * jnp.exp(b - bmid)).astype(BF16)
        ke = (k * jnp.exp(bmid - b)).astype(BF16)
        qs = (q * jnp.exp(b)).astype(BF16)
        kt = (k * jnp.exp(btot - b)).astype(BF16)
        zero = jnp.zeros((), BF16)
        kbd = jnp.where(kmask, jnp.concatenate([ke] * GLA_HEADS, axis=0), zero)
        vbd = jnp.where(vmask, jnp.concatenate([v] * GLA_HEADS, axis=0), zero)
        sc = _dot_nt(qe, kbd)
        sc = jnp.where(causal_f if fwd else causal_b, sc, 0.0)
        st = st_ref[...]
        o = _dot(sc.astype(BF16), vbd) + _dot_nt(qs, st.astype(BF16))
        vt = v.astype(F32).T.astype(BF16)
        ds = _dot(vt, kt)
        st_ref[...] = st * jnp.exp(btot) + jnp.where(smask, ds, 0.0)
        (of_ref if fwd else ob_ref)[rows, :] = o

    def finish(ci):
        rows = pl.ds(pl.multiple_of(ci * C, C), C)
        tot = of_ref[rows, :] + ob_ref[rows, :]
        th, tl = _split(tot * tot)
        ms = (_dot(th, hind) + _dot(tl, hind)) * (1.0 / GLA_DV)
        g = g_ref[rows, :].astype(F32)
        y = tot * lax.rsqrt(ms + EPS) * ng_ref[...] * (g * _sigmoid(g))
        o_ref[rows, :] = y.astype(o_ref.dtype)

    def both(f0, b0, count):
        def body(i, carry):
            chunk(f0 + i, True)
            chunk(b0 - i, False)
            return carry
        lax.fori_loop(0, count, body, 0)

    stf_ref[...] = jnp.zeros_like(stf_ref)
    stb_ref[...] = jnp.zeros_like(stb_ref)
    both(0, n_ctx - 1, n_ctx)
    both(n_ctx, n_all - 1, n_all - n_ctx)

    def fin_body(i, carry):
        finish(2 * i)
        finish(2 * i + 1)
        return carry
    lax.fori_loop(0, n_all // 2, fin_body, 0)


def _gla(q, k, v, g, bf, bb, norm_g, nb, T, L):
    seq = lambda w_: pl.BlockSpec((T, w_), lambda b: (b, 0))
    once = lambda w_: pl.BlockSpec((T, w_), lambda b: (b, 0), pipeline_mode=pl.Buffered(1))
    kern = functools.partial(_gla_kernel, n_ctx=L // GLA_CHUNK, n_all=T // GLA_CHUNK)
    return pl.pallas_call(
        kern,
        grid=(nb,),
        in_specs=[seq(GLA_QK), seq(GLA_QK), seq(GLA_V), seq(GLA_V), once(GLA_QK), once(GLA_QK),
                  pl.BlockSpec((1, GLA_V), lambda b: (0, 0))],
        out_specs=seq(GLA_V),
        out_shape=jax.ShapeDtypeStruct((nb * T, GLA_V), BF16),
        scratch_shapes=[pltpu.VMEM((GLA_V, GLA_QK), F32), pltpu.VMEM((GLA_V, GLA_QK), F32),
                        pltpu.VMEM((T, GLA_V), F32), pltpu.VMEM((T, GLA_V), F32)],
        compiler_params=_params(("arbitrary",)),
        name="gla",
    )(q, k, v, g, bf, bb, norm_g.reshape(1, GLA_V))


NA_GROUP = TOKEN_TILE // GRID_W
NA_WIN = NA_KH + NA_GROUP - 1


def _na_window_start(group, n_rows):
    return jnp.clip(group * NA_GROUP - NA_KH // 2, 0, n_rows - NA_WIN)


def _na_kernel(q_ref, k_ref, v_ref, bias_ref, o_ref, *, n_rows, L):
    j = pl.program_id(1)
    W = GRID_W
    Q = q_ref.shape[0]
    lane = lax.broadcasted_iota(jnp.int32, (2 * Q, LANE), 1)
    row = lax.broadcasted_iota(jnp.int32, (2 * Q, LANE), 0)
    qmask = (row < Q) == (lane < NA_DH)
    out_lo = lax.broadcasted_iota(jnp.int32, (Q, LANE), 1) < NA_DH
    zero = jnp.zeros((), BF16)

    def pair_q(p):
        q2 = q_ref[:, p * LANE:(p + 1) * LANE]
        return jnp.where(qmask, jnp.concatenate([q2, q2], axis=0), zero)

    def finish(p, acc, l):
        o2 = acc / l
        o_ref[:, p * LANE:(p + 1) * LANE] = jnp.where(out_lo, o2[:Q], o2[Q:]).astype(o_ref.dtype)

    @pl.when(j == 0)
    def _():
        for p in range(NA_HEADS // 2):
            cols = slice(p * LANE, (p + 1) * LANE)
            s = _dot_nt(pair_q(p), k_ref[0:L, cols])
            m = jnp.max(s, axis=-1, keepdims=True)
            e = jnp.exp(s - m)
            l = jnp.sum(e, axis=-1, keepdims=True)
            finish(p, _dot(e.astype(BF16), v_ref[0:L, cols]), l)

    @pl.when(j > 0)
    def _():
        k0 = pl.multiple_of(L + _na_window_start(j - 1, n_rows) * W, W)
        win = pl.ds(k0, NA_WIN * W)
        for p in range(NA_HEADS // 2):
            cols = slice(p * LANE, (p + 1) * LANE)
            qb = pair_q(p)
            s_w = _dot_nt(qb, k_ref[win, cols]) + bias_ref[p]
            s_c = _dot_nt(qb, k_ref[0:L, cols])
            m = jnp.maximum(jnp.max(s_w, axis=-1, keepdims=True), jnp.max(s_c, axis=-1, keepdims=True))
            e_w = jnp.exp(s_w - m)
            e_c = jnp.exp(s_c - m)
            l = jnp.sum(e_w, axis=-1, keepdims=True) + jnp.sum(e_c, axis=-1, keepdims=True)
            acc = _dot(e_w.astype(BF16), v_ref[win, cols]) + _dot(e_c.astype(BF16), v_ref[0:L, cols])
            finish(p, acc, l)


def _na_bias_table(rpb, n_rows):
    W, G = GRID_W, NA_GROUP
    n_groups = n_rows // G
    assert n_rows >= NA_WIN and n_groups >= 3
    qc = np.arange(W)[:, None]
    kc = np.arange(W)[None, :]
    cs = np.clip(qc - NA_KW // 2, 0, W - NA_KW)
    cvalid = (kc >= cs) & (kc < cs + NA_KW)
    cidx = np.clip(kc - qc + NA_KW - 1, 0, 2 * NA_KW - 2)
    rvalid, ridx = [], []
    for grp in (0, 1, n_groups - 1):
        ws = int(np.clip(grp * G - NA_KH // 2, 0, n_rows - NA_WIN))
        r = (grp * G + np.arange(G))[:, None]
        rs = np.clip(r - NA_KH // 2, 0, n_rows - NA_KH)
        key_row = (ws + np.arange(NA_WIN))[None, :]
        rvalid.append((key_row >= rs) & (key_row < rs + NA_KH))
        ridx.append(np.clip(key_row - r + NA_KH - 1, 0, 2 * NA_KH - 2))
    e_r = jnp.asarray(np.eye(2 * NA_KH - 1, dtype=np.float32)[np.stack(ridx)])
    e_c = jnp.asarray(np.eye(2 * NA_KW - 1, dtype=np.float32)[cidx])
    t = jnp.einsum('vria,hab,qkb->vhrqik', e_r, rpb.astype(F32), e_c, precision=lax.Precision.HIGHEST)
    valid = np.stack(rvalid)[:, None, :, None, :, None] & cvalid[None, None, None, :, None, :]
    t = jnp.where(jnp.asarray(valid), t, NEG)
    return t.reshape(3, NA_HEADS // 2, 2 * G * W, NA_WIN * W)


def _neighbourhood_attention(nq, nk, nv, bias, nb, T, L):
    W = GRID_W
    n_rows = (T - L) // W
    n_groups = n_rows // NA_GROUP
    Q = NA_GROUP * W
    spb = T // Q

    def bias_idx(b, j):
        return (jnp.where(j <= 1, 0, jnp.where(j == n_groups, 2, 1)), 0, 0, 0)

    kern = functools.partial(_na_kernel, n_rows=n_rows, L=L)
    return pl.pallas_call(
        kern,
        grid=(nb, spb),
        in_specs=[pl.BlockSpec((Q, NA_W), lambda b, j: (b * spb + j, 0)),
                  pl.BlockSpec((T, NA_W), lambda b, j: (b, 0)),
                  pl.BlockSpec((T, NA_W), lambda b, j: (b, 0)),
                  pl.BlockSpec((None, NA_HEADS // 2, 2 * Q, NA_WIN * W), bias_idx)],
        out_specs=pl.BlockSpec((Q, NA_W), lambda b, j: (b * spb + j, 0)),
        out_shape=jax.ShapeDtypeStruct((nb * T, NA_W), BF16),
        compiler_params=_params(("arbitrary", "arbitrary")),
        name="neigh_attn",
    )(nq, nk, nv, bias)


def _pool_kernel(u_ref, w_ref, sc_ref, o_ref, *, T, L):
    i = pl.program_id(1)
    tm = TOKEN_TILE
    ext = tm + 2 * POOL_HALO
    t0 = i * tm
    e0 = pl.multiple_of(jnp.clip(t0 - POOL_HALO, 0, T - ext), 8)
    seg_lo = jnp.where(t0 < L, 0, L)
    seg_hi = jnp.where(t0 < L, L, T)
    uh, ul = _split(u_ref[pl.ds(e0, ext), :])
    t = t0 + lax.broadcasted_iota(jnp.int32, (tm, ext), 0)
    c = e0 + lax.broadcasted_iota(jnp.int32, (tm, ext), 1)
    tcol = t0 + lax.broadcasted_iota(jnp.int32, (tm, 1), 0)
    lane = lax.broadcasted_iota(jnp.int32, (tm, POOL_WIDTH), 1)
    y = jnp.zeros((tm, POOL_WIDTH), F32)
    for gi, w in enumerate(POOL_WINDOWS):
        lo = jnp.maximum(t - w // 2, seg_lo)
        hi = jnp.minimum(t + w - w // 2, seg_hi)
        band = jnp.where((c >= lo) & (c < hi), 1.0, 0.0).astype(BF16)
        cnt = (jnp.minimum(tcol + w - w // 2, seg_hi) - jnp.maximum(tcol - w // 2, seg_lo)).astype(F32)
        s = (_dot(band, uh) + _dot(band, ul)) / cnt
        y = jnp.where((lane >= gi * POOL_GROUP) & (lane < (gi + 1) * POOL_GROUP), s, y)
    y = y - u_ref[pl.ds(pl.multiple_of(t0, tm), tm), :]
    o_ref[...] = (_dot(y.astype(BF16), w_ref[...]) * sc_ref[...]).astype(o_ref.dtype)


def _pool(u, w_bd, scale, nb, T, L):
    tpb = T // TOKEN_TILE
    kern = functools.partial(_pool_kernel, T=T, L=L)
    return pl.pallas_call(
        kern,
        grid=(nb, tpb),
        in_specs=[pl.BlockSpec((T, POOL_WIDTH), lambda b, i: (b, 0)),
                  pl.BlockSpec((POOL_WIDTH, POOL_WIDTH), lambda b, i: (0, 0)),
                  pl.BlockSpec((1, POOL_WIDTH), lambda b, i: (0, 0))],
        out_specs=pl.BlockSpec((TOKEN_TILE, POOL_WIDTH), lambda b, i: (b * tpb + i, 0)),
        out_shape=jax.ShapeDtypeStruct((nb * T, POOL_WIDTH), BF16),
        compiler_params=_params(("arbitrary", "arbitrary")),
        name="pool_mix",
    )(u, w_bd, scale.reshape(1, POOL_WIDTH))


def _outproj_kernel(gla_ref, na_ref, pm_ref, x_ref, mod_ref, g_ref, wo_ref, rw_ref, rb_ref,
                    x1_ref, h2_ref, te_ref, tg_ref, rk_ref, cnt_ref, carry_ref):
    @pl.when(pl.program_id(0) == 0)
    def _():
        carry_ref[...] = jnp.zeros_like(carry_ref)

    mx = (_dot(gla_ref[...], wo_ref[0:GLA_V, :])
          + _dot(na_ref[...], wo_ref[GLA_V:GLA_V + NA_W, :])
          + _dot(pm_ref[...], wo_ref[GLA_V + NA_W:, :]))
    x1 = x_ref[...] + mod_ref[2:3, :] * mx
    x1_ref[...] = x1
    ms = jnp.mean(x1 * x1, axis=-1, keepdims=True)
    h = x1 * lax.rsqrt(ms + EPS) * g_ref[...]
    h = h * (1.0 + mod_ref[4:5, :]) + mod_ref[3:4, :]
    h2_ref[...] = h
    hh, hl = _split(h)
    wh, wl = _split(rw_ref[...])
    logits = _dot(hh, wh) + _dot(hl, wh) + _dot(hh, wl) + rb_ref[...]
    tm = logits.shape[0]
    E = N_EXPERTS
    lt = logits.T[0:E, :]
    erow = lax.broadcasted_iota(jnp.int32, (E, tm), 0)
    cur = lt
    vals, idxs = [], []
    for kk in range(TOP_K):
        m = jnp.max(cur, axis=0, keepdims=True)
        idx = jnp.min(jnp.where(cur == m, erow, E), axis=0, keepdims=True)
        vals.append(m)
        idxs.append(idx)
        cur = jnp.where(erow == idx, NEG, cur)
    expv = [jnp.exp(v - vals[0]) for v in vals]
    den = sum(expv)
    hot = sum(jnp.where(erow == idx, 1.0, 0.0) for idx in idxs)
    before = (lax.broadcasted_iota(jnp.int32, (tm, tm), 0) < lax.broadcasted_iota(jnp.int32, (tm, tm), 1))
    base = _dot(hot.astype(BF16), jnp.where(before, 1.0, 0.0).astype(BF16)) + carry_ref[...]
    krow = lax.broadcasted_iota(jnp.int32, (SUB, tm), 0)
    te = jnp.zeros((SUB, tm), jnp.int32)
    rk = jnp.zeros((SUB, tm), jnp.int32)
    tv = jnp.zeros((SUB, tm), F32)
    for kk in range(TOP_K):
        r = jnp.sum(jnp.where(erow == idxs[kk], base, 0.0), axis=0, keepdims=True)
        te = jnp.where(krow == kk, idxs[kk], te)
        rk = jnp.where(krow == kk, r.astype(jnp.int32), rk)
        tv = jnp.where(krow == kk, expv[kk] / den, tv)
    te_ref[...] = te
    rk_ref[...] = rk
    tg_ref[...] = tv
    carry_ref[...] = carry_ref[...] + jnp.sum(hot, axis=1, keepdims=True)
    cnt_ref[...] = jnp.broadcast_to(carry_ref[...], cnt_ref.shape).astype(jnp.int32)


def _output_projection(gla, na, pm, x, mod, norm_g, w_out, rw, rb, nb, tpb):
    N, D = x.shape
    tm = TOKEN_TILE

    def mod_idx(i):
        return (jnp.where(i % tpb == 0, nb, i // tpb), 0, 0)

    tok = lambda w_: pl.BlockSpec((tm, w_), lambda i: (i, 0))
    n_tiles = N // tm
    route = pl.BlockSpec((SUB, tm), lambda i: (i, 0))
    return pl.pallas_call(
        _outproj_kernel,
        grid=(n_tiles,),
        in_specs=[tok(GLA_V), tok(NA_W), tok(POOL_WIDTH), tok(D),
                  pl.BlockSpec((None, 6, D), mod_idx),
                  pl.BlockSpec((1, D), lambda i: (0, 0)),
                  pl.BlockSpec((D, D), lambda i: (0, 0)),
                  pl.BlockSpec((D, LANE), lambda i: (0, 0)),
                  pl.BlockSpec((1, LANE), lambda i: (0, 0))],
        out_specs=[tok(D), tok(D), route, route, route,
                   pl.BlockSpec((N_EXPERTS, LANE), lambda i: (0, 0))],
        out_shape=[jax.ShapeDtypeStruct((N, D), F32), jax.ShapeDtypeStruct((N, D), F32),
                   jax.ShapeDtypeStruct((n_tiles * SUB, tm), jnp.int32),
                   jax.ShapeDtypeStruct((n_tiles * SUB, tm), F32),
                   jax.ShapeDtypeStruct((n_tiles * SUB, tm), jnp.int32),
                   jax.ShapeDtypeStruct((N_EXPERTS, LANE), jnp.int32)],
        scratch_shapes=[pltpu.VMEM((N_EXPERTS, 1), F32)],
        compiler_params=_params(("arbitrary",)),
        name="out_proj_router",
    )(gla, na, pm, x, mod, norm_g.reshape(1, D), w_out, rw, rb)


SUB = 8


def _row_of(ref, r):
    return ref.at[lax.shift_right_logical(r, 3), pl.ds(jnp.bitwise_and(r, SUB - 1), 1)]


def _dispatch_kernel(plo_ref, phi_ref, nv_ref, dest_ref, h_ref, xs_ref, zrow, zblk, sem, zsem, *, n_tiles):
    tm = h_ref.shape[0]
    for t in range(tm):
        for kk in range(TOP_K):
            d = dest_ref[t * TOP_K + kk]
            pltpu.make_async_copy(h_ref.at[pl.ds(t, 1)], xs_ref.at[pl.ds(d, 1)], sem).start(priority=kk % 2)
    for kk in range(TOP_K):
        pltpu.make_async_copy(h_ref, xs_ref.at[pl.ds(0, tm)], sem).wait()

    @pl.when(pl.program_id(0) == n_tiles - 1)
    def _():
        zrow[...] = jnp.zeros_like(zrow)

        def pad_copy(r):
            return pltpu.make_async_copy(zrow, xs_ref.at[pl.ds(r, 1)], zsem)

        def per_expert(e, carry):
            def start(r, c):
                pad_copy(r).start()
                return c

            def wait(r, c):
                pad_copy(r).wait()
                return c

            lax.fori_loop(plo_ref[e], phi_ref[e], start, 0)
            lax.fori_loop(plo_ref[e], phi_ref[e], wait, 0)
            return carry

        lax.fori_loop(0, N_EXPERTS, per_expert, 0)

        zblk[...] = jnp.zeros_like(zblk)
        bg = zblk.shape[0]

        def blk_copy(b):
            return pltpu.make_async_copy(zblk, xs_ref.at[pl.ds(pl.multiple_of(b * bg, SUB), bg)], zsem)

        def blk_start(b, c):
            blk_copy(b).start()
            return c

        def blk_wait(b, c):
            blk_copy(b).wait()
            return c

        n_blocks = xs_ref.shape[0] // bg
        lax.fori_loop(nv_ref[0], n_blocks, blk_start, 0)
        lax.fori_loop(nv_ref[0], n_blocks, blk_wait, 0)


def _dispatch(pad_lo, pad_hi, n_valid, dest_flat, h2, n_slots):
    N, D = h2.shape
    tm = TOKEN_TILE
    n_tiles = N // tm
    grid_spec = pltpu.PrefetchScalarGridSpec(
        num_scalar_prefetch=3,
        grid=(n_tiles,),
        in_specs=[pl.BlockSpec((tm * TOP_K,), lambda i, lo, hi, nv: (i,), memory_space=pltpu.SMEM),
                  pl.BlockSpec((tm, D), lambda i, lo, hi, nv: (i, 0))],
        out_specs=pl.BlockSpec(memory_space=pl.ANY),
        scratch_shapes=[pltpu.VMEM((1, D), F32), pltpu.VMEM((MOE_ROWS, D), F32),
                        pltpu.SemaphoreType.DMA(()), pltpu.SemaphoreType.DMA(())],
    )
    return pl.pallas_call(
        functools.partial(_dispatch_kernel, n_tiles=n_tiles),
        grid_spec=grid_spec,
        out_shape=jax.ShapeDtypeStruct((n_slots, D), F32),
        compiler_params=_params(("arbitrary",)),
        name="moe_dispatch",
    )(pad_lo, pad_hi, n_valid, dest_flat, h2)


def _moe_kernel(be_ref, nv_ref, xs_ref, wgu_ref, bgu_ref, wd_ref, bd_ref, y_ref):
    i = pl.program_id(0)

    @pl.when(i < nv_ref[0])
    def _():
        dff = wd_ref.shape[0]
        gu = _dot(xs_ref[...].astype(BF16), wgu_ref[...]) + bgu_ref[...]
        g = jnp.minimum(gu[:, :dff], SWIGLU_LIMIT)
        u = jnp.clip(gu[:, dff:], -SWIGLU_LIMIT, SWIGLU_LIMIT)
        act = g * _sigmoid(SWIGLU_ALPHA * g) * (u + 1.0)
        y_ref[...] = _dot(act.astype(BF16), wd_ref[...]) + bd_ref[...]

    @pl.when(i >= nv_ref[0])
    def _():
        y_ref[...] = jnp.zeros_like(y_ref)


def _moe_blocks(block_e, n_valid, xs, w_gu, b_gu, w_down, b_down):
    P, D = xs.shape
    E, _, F2 = w_gu.shape
    bm = MOE_ROWS
    grid_spec = pltpu.PrefetchScalarGridSpec(
        num_scalar_prefetch=2,
        grid=(P // bm,),
        in_specs=[pl.BlockSpec((bm, D), lambda i, be, nv: (i, 0)),
                  pl.BlockSpec((None, D, F2), lambda i, be, nv: (be[i], 0, 0)),
                  pl.BlockSpec((None, 1, F2), lambda i, be, nv: (be[i], 0, 0)),
                  pl.BlockSpec((None, F2 // 2, D), lambda i, be, nv: (be[i], 0, 0)),
                  pl.BlockSpec((None, 1, D), lambda i, be, nv: (be[i], 0, 0))],
        out_specs=pl.BlockSpec((bm, D), lambda i, be, nv: (i, 0)),
    )
    return pl.pallas_call(
        _moe_kernel,
        grid_spec=grid_spec,
        out_shape=jax.ShapeDtypeStruct((P, D), F32),
        compiler_params=_params(("arbitrary",)),
        name="moe_experts",
    )(block_e, n_valid, xs, w_gu, b_gu.reshape(E, 1, F2), w_down, b_down.reshape(E, 1, D))


def _combine_kernel(dcur_ref, dnxt_ref, x_ref, tg_ref, mod_ref, yb_ref, o_ref, ybuf, sem, *, n_tiles):
    i = pl.program_id(0)
    tm = x_ref.shape[0]
    slot = i % 2

    def issue(dref, s):
        def group(g, carry):
            base = g * (SUB * TOP_K)
            for u in range(SUB):
                for kk in range(TOP_K):
                    d = dref[base + u * TOP_K + kk]
                    pltpu.make_async_copy(_row_of(yb_ref, d), ybuf.at[s, kk, g, pl.ds(u, 1)],
                                          sem.at[s]).start(priority=kk % 2)
            return carry
        lax.fori_loop(0, tm // SUB, group, 0)

    @pl.when(i == 0)
    def _():
        issue(dcur_ref, 0)

    @pl.when(i + 1 < n_tiles)
    def _():
        issue(dnxt_ref, 1 - slot)

    for kk in range(TOP_K):
        pltpu.make_async_copy(yb_ref.at[pl.ds(0, tm // SUB)], ybuf.at[slot, kk], sem.at[slot]).wait()
    tg = tg_ref[...]
    rows = lambda kk: ybuf[slot, kk].reshape(tm, x_ref.shape[1])
    y = tg[:, 0:1] * rows(0)
    for kk in range(1, TOP_K):
        y = y + tg[:, kk:kk + 1] * rows(kk)
    o_ref[...] = x_ref[...] + mod_ref[5:6, :] * y


def _combine(dest_flat, x1, top_g, yb, mod, nb, tpb):
    N, D = x1.shape
    tm = TOKEN_TILE
    n_tiles = N // tm

    def mod_idx(i):
        return (jnp.where(i % tpb == 0, nb, i // tpb), 0, 0)

    kern = functools.partial(_combine_kernel, n_tiles=n_tiles)
    return pl.pallas_call(
        kern,
        grid=(n_tiles,),
        in_specs=[pl.BlockSpec((tm * TOP_K,), lambda i: (i,), memory_space=pltpu.SMEM),
                  pl.BlockSpec((tm * TOP_K,), lambda i: (jnp.minimum(i + 1, n_tiles - 1),),
                               memory_space=pltpu.SMEM),
                  pl.BlockSpec((tm, D), lambda i: (i, 0)),
                  pl.BlockSpec((tm, SUB), lambda i: (i, 0)),
                  pl.BlockSpec((None, 6, D), mod_idx),
                  pl.BlockSpec(memory_space=pl.ANY)],
        out_specs=pl.BlockSpec((tm, D), lambda i: (i, 0)),
        out_shape=jax.ShapeDtypeStruct((N, D), F32),
        scratch_shapes=[pltpu.VMEM((2, TOP_K, tm // SUB, SUB, D), F32), pltpu.SemaphoreType.DMA((2,))],
        compiler_params=_params(("arbitrary",)),
        name="moe_combine",
    )(dest_flat, dest_flat, x1, top_g, mod, yb.reshape(yb.shape[0] // SUB, SUB, D))


def _final_kernel(x_ref, g_ref, o_ref):
    x = x_ref[...]
    ms = jnp.mean(x * x, axis=-1, keepdims=True)
    o_ref[...] = x * lax.rsqrt(ms + EPS) * g_ref[...]


def _final_norm(x, g, nb, tpb, S):
    N, D = x.shape
    tm = TOKEN_TILE
    lt = S // tm
    ct = tpb - lt
    out = pl.pallas_call(
        _final_kernel,
        grid=(nb, lt),
        in_specs=[pl.BlockSpec((tm, D), lambda b, i: (b * tpb + ct + i, 0)),
                  pl.BlockSpec((1, D), lambda b, i: (0, 0))],
        out_specs=pl.BlockSpec((tm, D), lambda b, i: (b * lt + i, 0)),
        out_shape=jax.ShapeDtypeStruct((nb * S, D), F32),
        compiler_params=_params(("arbitrary", "arbitrary")),
        name="final_norm",
    )(x, g.reshape(1, D))
    return out.reshape(nb, S, D)


def _rope_tables(S, L):
    half = GLA_DK // 2
    nfreq = half // 2
    inv_freq = ROPE_THETA ** (-jnp.arange(0, half, 2, dtype=F32) / half)
    t = jnp.arange(S)
    pos = jnp.stack([(t // GRID_W).astype(F32), (t % GRID_W).astype(F32)], axis=0)
    d = np.arange(GLA_QK) % GLA_DK
    axis = d // half
    e = d % half
    fidx = e % nfreq
    sign = np.where(e < nfreq, -1.0, 1.0).astype(np.float32)
    ang = pos[axis].T * inv_freq[fidx][None, :]
    cos = jnp.cos(ang)
    sin = jnp.sin(ang) * sign[None, :]
    cos = jnp.concatenate([jnp.ones((L, GLA_QK), F32), cos], axis=0)
    sin = jnp.concatenate([jnp.zeros((L, GLA_QK), F32), sin], axis=0)
    pad = jnp.zeros((L + S, QK_PAD - GLA_QK), F32)
    qs = GLA_DK ** -0.5
    cos_t = jnp.concatenate([cos * qs, pad, cos, pad], axis=1)
    sin_t = jnp.concatenate([sin * qs, pad, sin, pad], axis=1)
    return cos_t, sin_t


def _partner_perm():
    d = np.arange(GLA_QK) % GLA_DK
    e = d % (GLA_DK // 2)
    nfreq = GLA_DK // 4
    return np.where(e < nfreq, np.arange(GLA_QK) + nfreq, np.arange(GLA_QK) - nfreq)


def _layout_w_in(w_in):
    D = w_in.shape[0]
    o = np.cumsum([0, GLA_QK, GLA_QK, GLA_V, GLA_RANK, GLA_RANK, GLA_V, NA_W, NA_W, NA_W, POOL_WIDTH])
    piece = lambda n: w_in[:, o[n]:o[n + 1]]
    perm = _partner_perm()
    z = lambda n: jnp.zeros((D, n), w_in.dtype)
    qpad = z(QK_PAD - GLA_QK)
    cols = [piece(0), qpad, piece(1), qpad,
            piece(0)[:, perm], qpad, piece(1)[:, perm], qpad,
            piece(2), piece(5),
            piece(3), piece(4), z(LANE - 2 * GLA_RANK),
            piece(6), piece(7), piece(8), piece(9)]
    return jnp.concatenate(cols, axis=1).astype(BF16)


def _layout_gate(wa2, ba):
    w = jnp.zeros((2, LANE, GLA_QK), F32)
    w = w.at[0, 0:GLA_RANK].set(wa2[0]).at[1, GLA_RANK:2 * GLA_RANK].set(wa2[1])
    return w.astype(BF16), ba.reshape(2, 1, GLA_QK)


def _block_diag(pool_w):
    G = pool_w.shape[0]
    w = jnp.zeros((POOL_WIDTH, POOL_WIDTH), F32)
    for g in range(G):
        w = w.at[g * POOL_GROUP:(g + 1) * POOL_GROUP, g * POOL_GROUP:(g + 1) * POOL_GROUP].set(pool_w[g])
    return w.astype(BF16)


def _routing(top_e, rank, counts, n_blocks):
    bm = MOE_ROWS
    counts = counts[:, 0]
    padded = (counts + bm - 1) // bm * bm
    pend = jnp.cumsum(padded)
    pstart = pend - padded
    tm = top_e.shape[1]
    te = top_e.reshape(-1, SUB, tm)[:, :TOP_K]
    rk = rank.reshape(-1, SUB, tm)[:, :TOP_K]
    start_of = jnp.zeros_like(te)
    for e in range(N_EXPERTS):
        start_of = jnp.where(te == e, pstart[e], start_of)
    dest = jnp.transpose(start_of + rk, (0, 2, 1)).astype(jnp.int32).reshape(-1)
    first_row = jnp.arange(n_blocks, dtype=jnp.int32) * bm
    block_e = jnp.minimum(jnp.sum((pend[None, :] <= first_row[:, None]).astype(jnp.int32), axis=1),
                          N_EXPERTS - 1).astype(jnp.int32)
    n_valid = (pend[-1] // bm).astype(jnp.int32).reshape(1)
    pad_lo = (pstart + counts).astype(jnp.int32)
    return dest, block_e, n_valid, pad_lo, pend.astype(jnp.int32)


def kernel(x, c, ctx, c_ctx, norm1_g, norm2_g, w_ada, b_ada, w_in, gla_wa2, gla_ba, gla_norm_g, na_rpb,
           pool_w, pool_scale, w_out, router_w, router_b, w_gu, b_gu, w_down, b_down, final_g):
    B, S, D = x.shape
    L = ctx.shape[1]
    assert L == TOKEN_TILE and S % TOKEN_TILE == 0 and S % GRID_W == 0
    T = L + S
    tpb = T // TOKEN_TILE
    N = B * T
    A = N * TOP_K
    n_blocks = (A + MOE_ROWS - 1) // MOE_ROWS + N_EXPERTS

    xa = jnp.concatenate([ctx, x], axis=1).reshape(N, D)
    cond_rows = 8 * ((B + 1 + 7) // 8)
    cond = jnp.zeros((cond_rows, D), F32).at[:B].set(c).at[B].set(c_ctx)
    cos_t, sin_t = _rope_tables(S, L)

    def layer(pending, p):
        (n1, n2, wada, bada, win, wa2, ba, gng, rpb, pw, psc, wout, rw, rb, wgu, bgu, wdn, bdn) = p
        mod = _modulation(cond, wada, bada).reshape(cond_rows, 6, D)
        wa, bap = _layout_gate(wa2, ba)
        xa, q, k, v, g, bf, bb, nq, nk, nv, pu = _input_projection(pending, mod, n1, _layout_w_in(win),
                                                                   cos_t, sin_t, wa, bap, B, tpb)
        gla_o = _gla(q, k, v, g, bf, bb, gng, B, T, L)
        na_o = _neighbourhood_attention(nq, nk, nv, _na_bias_table(rpb, S // GRID_W), B, T, L)
        pool_o = _pool(pu, _block_diag(pw), psc, B, T, L)
        rwp = jnp.zeros((D, LANE), F32).at[:, :N_EXPERTS].set(rw)
        rbp = jnp.full((1, LANE), NEG, F32).at[0, :N_EXPERTS].set(rb)
        x1, h2, top_e, top_g, rank, counts = _output_projection(gla_o, na_o, pool_o, xa, mod, n2,
                                                                wout.astype(BF16), rwp, rbp, B, tpb)
        dest, block_e, n_valid, pad_lo, pad_hi = _routing(top_e, rank, counts, n_blocks)
        xs = _dispatch(pad_lo, pad_hi, n_valid, dest, h2, n_blocks * MOE_ROWS)
        yb = _moe_blocks(block_e, n_valid, xs, wgu.astype(BF16), bgu, wdn.astype(BF16), bdn)
        gates = jnp.transpose(top_g.reshape(-1, SUB, TOKEN_TILE), (0, 2, 1)).reshape(N, SUB)
        return (dest, x1, gates, mod, yb), None

    params = (norm1_g, norm2_g, w_ada, b_ada, w_in, gla_wa2, gla_ba, gla_norm_g, na_rpb, pool_w, pool_scale,
              w_out, router_w, router_b, w_gu, b_gu, w_down, b_down)
    nothing = (jnp.zeros((A,), jnp.int32), xa, jnp.zeros((N, SUB), F32), jnp.zeros((cond_rows, 6, D), F32),
               jnp.zeros((n_blocks * MOE_ROWS, D), F32))
    (dest, x1, gates, mod, yb), _ = lax.scan(layer, nothing, params)
    xa = _combine(dest, x1, gates, yb, mod, B, tpb)
    return _final_norm(xa, final_g, B, tpb, S)
```

```python
import functools

import numpy as np
import jax
import jax.numpy as jnp
from jax import lax
from jax.experimental import pallas as pl
from jax.experimental.pallas import tpu as pltpu

F32 = jnp.float32
BF16 = jnp.bfloat16

EPS = 1e-6
ROPE_THETA = 10000.0
GRID_W = 64

GLA_HEADS = 4
GLA_DK = 48
GLA_DV = 96
GLA_QK = GLA_HEADS * GLA_DK
GLA_V = GLA_HEADS * GLA_DV
GLA_RANK = 16
GLA_TAU = 16.0
GLA_CHUNK = 64

NA_HEADS = 6
NA_DH = 64
NA_W = NA_HEADS * NA_DH
NA_KH = 8
NA_KW = 16

POOL_WINDOWS = (2, 4, 8, 16)
POOL_WIDTH = 256
POOL_GROUP = 64
POOL_HALO = 8

N_EXPERTS = 32
TOP_K = 4
SWIGLU_LIMIT = 7.0
SWIGLU_ALPHA = 1.702

LANE = 128
TOKEN_TILE = 256
MOE_ROWS = 256
NEG = -1e30
VMEM_LIMIT = 56 * 1024 * 1024

QK_PAD = 256
C_QK = 0
C_SW = C_QK + 2 * QK_PAD
C_V = C_SW + 2 * QK_PAD
C_G = C_V + GLA_V
C_A = C_G + GLA_V
C_NQ = C_A + LANE
C_NK = C_NQ + NA_W
C_NV = C_NK + NA_W
C_PU = C_NV + NA_W
C_END = C_PU + POOL_WIDTH


def _dot(a, b):
    return jnp.dot(a, b, preferred_element_type=F32)


def _dot_nt(a, b):
    return lax.dot_general(a, b, (((1,), (1,)), ((), ())), preferred_element_type=F32)


def _split(x):
    hi = x.astype(BF16)
    lo = (x - hi.astype(F32)).astype(BF16)
    return hi, lo


def _sigmoid(x):
    return 1.0 / (1.0 + jnp.exp(-x))


def _params(sem):
    return pltpu.CompilerParams(dimension_semantics=sem, vmem_limit_bytes=VMEM_LIMIT)


def _mod_kernel(c_ref, w_ref, b_ref, o_ref):
    c = c_ref[...]
    s = c * _sigmoid(c)
    sh, sl = _split(s)
    wh, wl = _split(w_ref[...])
    o_ref[...] = _dot(sh, wh) + _dot(sl, wh) + _dot(sh, wl) + b_ref[...]


def _modulation(cond, w_ada, b_ada):
    R, D = cond.shape
    N = w_ada.shape[1]
    bn = 1024
    return pl.pallas_call(
        _mod_kernel,
        grid=(N // bn,),
        in_specs=[pl.BlockSpec((R, D), lambda j: (0, 0)),
                  pl.BlockSpec((D, bn), lambda j: (0, j)),
                  pl.BlockSpec((1, bn), lambda j: (0, j))],
        out_specs=pl.BlockSpec((R, bn), lambda j: (0, j)),
        out_shape=jax.ShapeDtypeStruct((R, N), F32),
        compiler_params=_params(("arbitrary",)),
        name="adaln_mod",
    )(cond, w_ada, b_ada.reshape(1, N))


def _inproj_kernel(x_ref, mod_ref, g_ref, w_ref, cos_ref, sin_ref, wa_ref, ba_ref, tri_ref,
                   q_ref, k_ref, v_ref, gg_ref, bf_ref, bb_ref, nq_ref, nk_ref, nv_ref, pu_ref):
    x = x_ref[...]
    ms = jnp.mean(x * x, axis=-1, keepdims=True)
    h = x * lax.rsqrt(ms + EPS) * g_ref[...]
    h = h * (1.0 + mod_ref[1:2, :]) + mod_ref[0:1, :]
    hb = h.astype(BF16)

    def proj(lo, hi):
        return _dot(hb, w_ref[:, lo:hi])

    rot = proj(C_QK, C_SW) * cos_ref[...] + proj(C_SW, C_V) * sin_ref[...]
    q_ref[...] = rot[:, 0:GLA_QK].astype(BF16)
    k_ref[...] = rot[:, QK_PAD:QK_PAD + GLA_QK].astype(BF16)
    v_ref[...] = proj(C_V, C_G).astype(BF16)
    gg_ref[...] = proj(C_G, C_A).astype(BF16)
    a = proj(C_A, C_NQ).astype(BF16)
    for d, out_ref in enumerate((bf_ref, bb_ref)):
        z = _dot(a, wa_ref[d]) + ba_ref[d]
        la = (jnp.minimum(z, 0.0) - jnp.log(1.0 + jnp.exp(-jnp.abs(z)))) * (1.0 / GLA_TAU)
        lh, ll = _split(la)
        out_ref[...] = _dot(tri_ref[d], lh) + _dot(tri_ref[d], ll)
    nq_ref[...] = (proj(C_NQ, C_NK) * (NA_DH ** -0.5)).astype(BF16)
    nk_ref[...] = proj(C_NK, C_NV).astype(BF16)
    nv_ref[...] = proj(C_NV, C_PU).astype(BF16)
    pu_ref[...] = proj(C_PU, C_END)


def _chunk_triangles():
    r = np.arange(TOKEN_TILE)[:, None]
    c = np.arange(TOKEN_TILE)[None, :]
    same = (r // GLA_CHUNK) == (c // GLA_CHUNK)
    return jnp.asarray(np.stack([same & (c <= r), same & (c >= r)]).astype(np.float32), dtype=BF16)


def _input_projection(x, mod, norm_g, w, cos_t, sin_t, wa, ba, nb, tpb):
    N, D = x.shape
    tm = TOKEN_TILE

    def mod_idx(i):
        return (jnp.where(i % tpb == 0, nb, i // tpb), 0, 0)

    tok = lambda w_: pl.BlockSpec((tm, w_), lambda i: (i, 0))
    outs = [(GLA_QK, BF16), (GLA_QK, BF16), (GLA_V, BF16), (GLA_V, BF16), (GLA_QK, F32), (GLA_QK, F32),
            (NA_W, BF16), (NA_W, BF16), (NA_W, BF16), (POOL_WIDTH, F32)]
    return pl.pallas_call(
        _inproj_kernel,
        grid=(N // tm,),
        in_specs=[tok(D),
                  pl.BlockSpec((None, 6, D), mod_idx),
                  pl.BlockSpec((1, D), lambda i: (0, 0)),
                  pl.BlockSpec((D, C_END), lambda i: (0, 0)),
                  pl.BlockSpec((tm, 2 * QK_PAD), lambda i: (i % tpb, 0)),
                  pl.BlockSpec((tm, 2 * QK_PAD), lambda i: (i % tpb, 0)),
                  pl.BlockSpec((2, LANE, GLA_QK), lambda i: (0, 0, 0)),
                  pl.BlockSpec((2, 1, GLA_QK), lambda i: (0, 0, 0)),
                  pl.BlockSpec((2, tm, tm), lambda i: (0, 0, 0))],
        out_specs=[tok(w_) for w_, _ in outs],
        out_shape=[jax.ShapeDtypeStruct((N, w_), dt) for w_, dt in outs],
        compiler_params=_params(("arbitrary",)),
        name="in_proj",
    )(x, mod, norm_g.reshape(1, D), w, cos_t, sin_t, wa, ba, _chunk_triangles())


def _lane_head(idx, width):
    h = jnp.zeros_like(idx)
    for t in range(1, GLA_HEADS):
        h = h + (idx >= t * width).astype(jnp.int32)
    return h


def _gla_kernel(q_ref, k_ref, v_ref, g_ref, bf_ref, bb_ref, ng_ref, o_ref,
                stf_ref, stb_ref, of_ref, ob_ref, *, n_ctx, n_all):
    assert n_all % 2 == 0
    C = GLA_CHUNK
    i32 = jnp.int32
    kmask = (lax.broadcasted_iota(i32, (GLA_HEADS * C, GLA_QK), 0) // C
             == _lane_head(lax.broadcasted_iota(i32, (GLA_HEADS * C, GLA_QK), 1), GLA_DK))
    vmask = (lax.broadcasted_iota(i32, (GLA_HEADS * C, GLA_V), 0) // C
             == _lane_head(lax.broadcasted_iota(i32, (GLA_HEADS * C, GLA_V), 1), GLA_DV))
    smask = (_lane_head(lax.broadcasted_iota(i32, (GLA_V, GLA_QK), 0), GLA_DV)
             == _lane_head(lax.broadcasted_iota(i32, (GLA_V, GLA_QK), 1), GLA_DK))
    hind = jnp.where(_lane_head(lax.broadcasted_iota(i32, (GLA_V, GLA_V), 0), GLA_DV)
                     == _lane_head(lax.broadcasted_iota(i32, (GLA_V, GLA_V), 1), GLA_DV),
                     1.0, 0.0).astype(BF16)
    qi_ = lax.broadcasted_iota(i32, (C, GLA_HEADS * C), 0)
    kj_ = lax.broadcasted_iota(i32, (C, GLA_HEADS * C), 1) % C
    causal_f = qi_ >= kj_
    causal_b = qi_ <= kj_

    def chunk(ci, fwd):
        st_ref = stf_ref if fwd else stb_ref
        r0 = pl.multiple_of(ci * C, C)
        rows = pl.ds(r0, C)
        q = q_ref[rows, :].astype(F32)
        k = k_ref[rows, :].astype(F32)
        v = v_ref[rows, :]
        b = (bf_ref if fwd else bb_ref)[rows, :]
        if fwd:
            bmid = b[C // 2 - 1:C // 2, :]
            btot = b[C - 1:C, :]
        else:
            bmid = b[C // 2:C // 2 + 1, :]
            btot = b[0:1, :]
        qe = (q * jnp.exp(b - bmid)).astype(BF16)
        ke = (k * jnp.exp(bmid - b)).astype(BF16)
        qs = (q * jnp.exp(b)).astype(BF16)
        kt = (k * jnp.exp(btot - b)).astype(BF16)
        zero = jnp.zeros((), BF16)
        kbd = jnp.where(kmask, jnp.concatenate([ke] * GLA_HEADS, axis=0), zero)
        vbd = jnp.where(vmask, jnp.concatenate([v] * GLA_HEADS, axis=0), zero)
        sc = _dot_nt(qe, kbd)
        sc = jnp.where(causal_f if fwd else causal_b, sc, 0.0)
        st = st_ref[...]
        o = _dot(sc.astype(BF16), vbd) + _dot_nt(qs, st.astype(BF16))
        vt = v.astype(F32).T.astype(BF16)
        ds = _dot(vt, kt)
        st_ref[...] = st * jnp.exp(btot) + jnp.where(smask, ds, 0.0)
        (of_ref if fwd else ob_ref)[rows, :] = o

    def finish(ci):
        rows = pl.ds(pl.multiple_of(ci * C, C), C)
        tot = of_ref[rows, :] + ob_ref[rows, :]
        th, tl = _split(tot * tot)
        ms = (_dot(th, hind) + _dot(tl, hind)) * (1.0 / GLA_DV)
        g = g_ref[rows, :].astype(F32)
        y = tot * lax.rsqrt(ms + EPS) * ng_ref[...] * (g * _sigmoid(g))
        o_ref[rows, :] = y.astype(o_ref.dtype)

    def both(f0, b0, count):
        def body(i, carry):
            chunk(f0 + i, True)
            chunk(b0 - i, False)
            return carry
        lax.fori_loop(0, count, body, 0)

    stf_ref[...] = jnp.zeros_like(stf_ref)
    stb_ref[...] = jnp.zeros_like(stb_ref)
    both(0, n_ctx - 1, n_ctx)
    both(n_ctx, n_all - 1, n_all - n_ctx)

    def fin_body(i, carry):
        finish(2 * i)
        finish(2 * i + 1)
        return carry
    lax.fori_loop(0, n_all // 2, fin_body, 0)


def _gla(q, k, v, g, bf, bb, norm_g, nb, T, L):
    seq = lambda w_: pl.BlockSpec((T, w_), lambda b: (b, 0))
    once = lambda w_: pl.BlockSpec((T, w_), lambda b: (b, 0), pipeline_mode=pl.Buffered(1))
    kern = functools.partial(_gla_kernel, n_ctx=L // GLA_CHUNK, n_all=T // GLA_CHUNK)
    return pl.pallas_call(
        kern,
        grid=(nb,),
        in_specs=[seq(GLA_QK), seq(GLA_QK), seq(GLA_V), seq(GLA_V), once(GLA_QK), once(GLA_QK),
                  pl.BlockSpec((1, GLA_V), lambda b: (0, 0))],
        out_specs=seq(GLA_V),
        out_shape=jax.ShapeDtypeStruct((nb * T, GLA_V), BF16),
        scratch_shapes=[pltpu.VMEM((GLA_V, GLA_QK), F32), pltpu.VMEM((GLA_V, GLA_QK), F32),
                        pltpu.VMEM((T, GLA_V), F32), pltpu.VMEM((T, GLA_V), F32)],
        compiler_params=_params(("arbitrary",)),
        name="gla",
    )(q, k, v, g, bf, bb, norm_g.reshape(1, GLA_V))


NA_GROUP = TOKEN_TILE // GRID_W
NA_WIN = NA_KH + NA_GROUP - 1


def _na_window_start(group, n_rows):
    return jnp.clip(group * NA_GROUP - NA_KH // 2, 0, n_rows - NA_WIN)


def _na_kernel(q_ref, k_ref, v_ref, bias_ref, o_ref, *, n_rows, L):
    j = pl.program_id(1)
    W = GRID_W
    Q = q_ref.shape[0]
    lane = lax.broadcasted_iota(jnp.int32, (2 * Q, LANE), 1)
    row = lax.broadcasted_iota(jnp.int32, (2 * Q, LANE), 0)
    qmask = (row < Q) == (lane < NA_DH)
    out_lo = lax.broadcasted_iota(jnp.int32, (Q, LANE), 1) < NA_DH
    zero = jnp.zeros((), BF16)

    def pair_q(p):
        q2 = q_ref[:, p * LANE:(p + 1) * LANE]
        return jnp.where(qmask, jnp.concatenate([q2, q2], axis=0), zero)

    def finish(p, acc, l):
        o2 = acc / l
        o_ref[:, p * LANE:(p + 1) * LANE] = jnp.where(out_lo, o2[:Q], o2[Q:]).astype(o_ref.dtype)

    @pl.when(j == 0)
    def _():
        for p in range(NA_HEADS // 2):
            cols = slice(p * LANE, (p + 1) * LANE)
            s = _dot_nt(pair_q(p), k_ref[0:L, cols])
            m = jnp.max(s, axis=-1, keepdims=True)
            e = jnp.exp(s - m)
            l = jnp.sum(e, axis=-1, keepdims=True)
            finish(p, _dot(e.astype(BF16), v_ref[0:L, cols]), l)

    @pl.when(j > 0)
    def _():
        k0 = pl.multiple_of(L + _na_window_start(j - 1, n_rows) * W, W)
        win = pl.ds(k0, NA_WIN * W)
        for p in range(NA_HEADS // 2):
            cols = slice(p * LANE, (p + 1) * LANE)
            qb = pair_q(p)
            s_w = _dot_nt(qb, k_ref[win, cols]) + bias_ref[p]
            s_c = _dot_nt(qb, k_ref[0:L, cols])
            m = jnp.maximum(jnp.max(s_w, axis=-1, keepdims=True), jnp.max(s_c, axis=-1, keepdims=True))
            e_w = jnp.exp(s_w - m)
            e_c = jnp.exp(s_c - m)
            l = jnp.sum(e_w, axis=-1, keepdims=True) + jnp.sum(e_c, axis=-1, keepdims=True)
            acc = _dot(e_w.astype(BF16), v_ref[win, cols]) + _dot(e_c.astype(BF16), v_ref[0:L, cols])
            finish(p, acc, l)


def _na_bias_table(rpb, n_rows):
    W, G = GRID_W, NA_GROUP
    n_groups = n_rows // G
    assert n_rows >= NA_WIN and n_groups >= 3
    qc = np.arange(W)[:, None]
    kc = np.arange(W)[None, :]
    cs = np.clip(qc - NA_KW // 2, 0, W - NA_KW)
    cvalid = (kc >= cs) & (kc < cs + NA_KW)
    cidx = np.clip(kc - qc + NA_KW - 1, 0, 2 * NA_KW - 2)
    rvalid, ridx = [], []
    for grp in (0, 1, n_groups - 1):
        ws = int(np.clip(grp * G - NA_KH // 2, 0, n_rows - NA_WIN))
        r = (grp * G + np.arange(G))[:, None]
        rs = np.clip(r - NA_KH // 2, 0, n_rows - NA_KH)
        key_row = (ws + np.arange(NA_WIN))[None, :]
        rvalid.append((key_row >= rs) & (key_row < rs + NA_KH))
        ridx.append(np.clip(key_row - r + NA_KH - 1, 0, 2 * NA_KH - 2))
    e_r = jnp.asarray(np.eye(2 * NA_KH - 1, dtype=np.float32)[np.stack(ridx)])
    e_c = jnp.asarray(np.eye(2 * NA_KW - 1, dtype=np.float32)[cidx])
    t = jnp.einsum('vria,hab,qkb->vhrqik', e_r, rpb.astype(F32), e_c, precision=lax.Precision.HIGHEST)
    valid = np.stack(rvalid)[:, None, :, None, :, None] & cvalid[None, None, None, :, None, :]
    t = jnp.where(jnp.asarray(valid), t, NEG)
    return t.reshape(3, NA_HEADS // 2, 2 * G * W, NA_WIN * W)


def _neighbourhood_attention(nq, nk, nv, bias, nb, T, L):
    W = GRID_W
    n_rows = (T - L) // W
    n_groups = n_rows // NA_GROUP
    Q = NA_GROUP * W
    spb = T // Q

    def bias_idx(b, j):
        return (jnp.where(j <= 1, 0, jnp.where(j == n_groups, 2, 1)), 0, 0, 0)

    kern = functools.partial(_na_kernel, n_rows=n_rows, L=L)
    return pl.pallas_call(
        kern,
        grid=(nb, spb),
        in_specs=[pl.BlockSpec((Q, NA_W), lambda b, j: (b * spb + j, 0)),
                  pl.BlockSpec((T, NA_W), lambda b, j: (b, 0)),
                  pl.BlockSpec((T, NA_W), lambda b, j: (b, 0)),
                  pl.BlockSpec((None, NA_HEADS // 2, 2 * Q, NA_WIN * W), bias_idx)],
        out_specs=pl.BlockSpec((Q, NA_W), lambda b, j: (b * spb + j, 0)),
        out_shape=jax.ShapeDtypeStruct((nb * T, NA_W), BF16),
        compiler_params=_params(("arbitrary", "arbitrary")),
        name="neigh_attn",
    )(nq, nk, nv, bias)


def _pool_kernel(u_ref, w_ref, sc_ref, o_ref, *, T, L):
    i = pl.program_id(1)
    tm = TOKEN_TILE
    ext = tm + 2 * POOL_HALO
    t0 = i * tm
    e0 = pl.multiple_of(jnp.clip(t0 - POOL_HALO, 0, T - ext), 8)
    seg_lo = jnp.where(t0 < L, 0, L)
    seg_hi = jnp.where(t0 < L, L, T)
    uh, ul = _split(u_ref[pl.ds(e0, ext), :])
    t = t0 + lax.broadcasted_iota(jnp.int32, (tm, ext), 0)
    c = e0 + lax.broadcasted_iota(jnp.int32, (tm, ext), 1)
    tcol = t0 + lax.broadcasted_iota(jnp.int32, (tm, 1), 0)
    lane = lax.broadcasted_iota(jnp.int32, (tm, POOL_WIDTH), 1)
    y = jnp.zeros((tm, POOL_WIDTH), F32)
    for gi, w in enumerate(POOL_WINDOWS):
        lo = jnp.maximum(t - w // 2, seg_lo)
        hi = jnp.minimum(t + w - w // 2, seg_hi)
        band = jnp.where((c >= lo) & (c < hi), 1.0, 0.0).astype(BF16)
        cnt = (jnp.minimum(tcol + w - w // 2, seg_hi) - jnp.maximum(tcol - w // 2, seg_lo)).astype(F32)
        s = (_dot(band, uh) + _dot(band, ul)) / cnt
        y = jnp.where((lane >= gi * POOL_GROUP) & (lane < (gi + 1) * POOL_GROUP), s, y)
    y = y - u_ref[pl.ds(pl.multiple_of(t0, tm), tm), :]
    o_ref[...] = (_dot(y.astype(BF16), w_ref[...]) * sc_ref[...]).astype(o_ref.dtype)


def _pool(u, w_bd, scale, nb, T, L):
    tpb = T // TOKEN_TILE
    kern = functools.partial(_pool_kernel, T=T, L=L)
    return pl.pallas_call(
        kern,
        grid=(nb, tpb),
        in_specs=[pl.BlockSpec((T, POOL_WIDTH), lambda b, i: (b, 0)),
                  pl.BlockSpec((POOL_WIDTH, POOL_WIDTH), lambda b, i: (0, 0)),
                  pl.BlockSpec((1, POOL_WIDTH), lambda b, i: (0, 0))],
        out_specs=pl.BlockSpec((TOKEN_TILE, POOL_WIDTH), lambda b, i: (b * tpb + i, 0)),
        out_shape=jax.ShapeDtypeStruct((nb * T, POOL_WIDTH), BF16),
        compiler_params=_params(("arbitrary", "arbitrary")),
        name="pool_mix",
    )(u, w_bd, scale.reshape(1, POOL_WIDTH))


def _outproj_kernel(gla_ref, na_ref, pm_ref, x_ref, mod_ref, g_ref, wo_ref, rw_ref, rb_ref,
                    x1_ref, h2_ref, te_ref, tg_ref, rk_ref, cnt_ref, carry_ref):
    @pl.when(pl.program_id(0) == 0)
    def _():
        carry_ref[...] = jnp.zeros_like(carry_ref)

    mx = (_dot(gla_ref[...], wo_ref[0:GLA_V, :])
          + _dot(na_ref[...], wo_ref[GLA_V:GLA_V + NA_W, :])
          + _dot(pm_ref[...], wo_ref[GLA_V + NA_W:, :]))
    x1 = x_ref[...] + mod_ref[2:3, :] * mx
    x1_ref[...] = x1
    ms = jnp.mean(x1 * x1, axis=-1, keepdims=True)
    h = x1 * lax.rsqrt(ms + EPS) * g_ref[...]
    h = h * (1.0 + mod_ref[4:5, :]) + mod_ref[3:4, :]
    h2_ref[...] = h
    hh, hl = _split(h)
    wh, wl = _split(rw_ref[...])
    logits = _dot(hh, wh) + _dot(hl, wh) + _dot(hh, wl) + rb_ref[...]
    tm = logits.shape[0]
    E = N_EXPERTS
    lt = logits.T[0:E, :]
    erow = lax.broadcasted_iota(jnp.int32, (E, tm), 0)
    cur = lt
    vals, idxs = [], []
    for kk in range(TOP_K):
        m = jnp.max(cur, axis=0, keepdims=True)
        idx = jnp.min(jnp.where(cur == m, erow, E), axis=0, keepdims=True)
        vals.append(m)
        idxs.append(idx)
        cur = jnp.where(erow == idx, NEG, cur)
    expv = [jnp.exp(v - vals[0]) for v in vals]
    den = sum(expv)
    hot = sum(jnp.where(erow == idx, 1.0, 0.0) for idx in idxs)
    before = (lax.broadcasted_iota(jnp.int32, (tm, tm), 0) < lax.broadcasted_iota(jnp.int32, (tm, tm), 1))
    base = _dot(hot.astype(BF16), jnp.where(before, 1.0, 0.0).astype(BF16)) + carry_ref[...]
    krow = lax.broadcasted_iota(jnp.int32, (SUB, tm), 0)
    te = jnp.zeros((SUB, tm), jnp.int32)
    rk = jnp.zeros((SUB, tm), jnp.int32)
    tv = jnp.zeros((SUB, tm), F32)
    for kk in range(TOP_K):
        r = jnp.sum(jnp.where(erow == idxs[kk], base, 0.0), axis=0, keepdims=True)
        te = jnp.where(krow == kk, idxs[kk], te)
        rk = jnp.where(krow == kk, r.astype(jnp.int32), rk)
        tv = jnp.where(krow == kk, expv[kk] / den, tv)
    te_ref[...] = te
    rk_ref[...] = rk
    tg_ref[...] = tv
    carry_ref[...] = carry_ref[...] + jnp.sum(hot, axis=1, keepdims=True)
    cnt_ref[...] = jnp.broadcast_to(carry_ref[...], cnt_ref.shape).astype(jnp.int32)


def _output_projection(gla, na, pm, x, mod, norm_g, w_out, rw, rb, nb, tpb):
    N, D = x.shape
    tm = TOKEN_TILE

    def mod_idx(i):
        return (jnp.where(i % tpb == 0, nb, i // tpb), 0, 0)

    tok = lambda w_: pl.BlockSpec((tm, w_), lambda i: (i, 0))
    n_tiles = N // tm
    route = pl.BlockSpec((SUB, tm), lambda i: (i, 0))
    return pl.pallas_call(
        _outproj_kernel,
        grid=(n_tiles,),
        in_specs=[tok(GLA_V), tok(NA_W), tok(POOL_WIDTH), tok(D),
                  pl.BlockSpec((None, 6, D), mod_idx),
                  pl.BlockSpec((1, D), lambda i: (0, 0)),
                  pl.BlockSpec((D, D), lambda i: (0, 0)),
                  pl.BlockSpec((D, LANE), lambda i: (0, 0)),
                  pl.BlockSpec((1, LANE), lambda i: (0, 0))],
        out_specs=[tok(D), tok(D), route, route, route,
                   pl.BlockSpec((N_EXPERTS, LANE), lambda i: (0, 0))],
        out_shape=[jax.ShapeDtypeStruct((N, D), F32), jax.ShapeDtypeStruct((N, D), F32),
                   jax.ShapeDtypeStruct((n_tiles * SUB, tm), jnp.int32),
                   jax.ShapeDtypeStruct((n_tiles * SUB, tm), F32),
                   jax.ShapeDtypeStruct((n_tiles * SUB, tm), jnp.int32),
                   jax.ShapeDtypeStruct((N_EXPERTS, LANE), jnp.int32)],
        scratch_shapes=[pltpu.VMEM((N_EXPERTS, 1), F32)],
        compiler_params=_params(("arbitrary",)),
        name="out_proj_router",
    )(gla, na, pm, x, mod, norm_g.reshape(1, D), w_out, rw, rb)


SUB = 8


def _dispatch_kernel(plo_ref, phi_ref, nv_ref, dest_ref, h_ref, xs_ref, zrow, zblk, sem, zsem, *, n_tiles):
    tm = h_ref.shape[0]
    for t in range(tm):
        for kk in range(TOP_K):
            d = dest_ref[t * TOP_K + kk]
            pltpu.make_async_copy(h_ref.at[pl.ds(t, 1)], xs_ref.at[pl.ds(d, 1)], sem).start(priority=kk % 2)
    for kk in range(TOP_K):
        pltpu.make_async_copy(h_ref, xs_ref.at[pl.ds(0, tm)], sem).wait()

    @pl.when(pl.program_id(0) == n_tiles - 1)
    def _():
        zrow[...] = jnp.zeros_like(zrow)

        def pad_copy(r):
            return pltpu.make_async_copy(zrow, xs_ref.at[pl.ds(r, 1)], zsem)

        def per_expert(e, carry):
            def start(r, c):
                pad_copy(r).start()
                return c

            def wait(r, c):
                pad_copy(r).wait()
                return c

            lax.fori_loop(plo_ref[e], phi_ref[e], start, 0)
            lax.fori_loop(plo_ref[e], phi_ref[e], wait, 0)
            return carry

        lax.fori_loop(0, N_EXPERTS, per_expert, 0)

        zblk[...] = jnp.zeros_like(zblk)
        bg = zblk.shape[0]

        def blk_copy(b):
            return pltpu.make_async_copy(zblk, xs_ref.at[pl.ds(pl.multiple_of(b * bg, SUB), bg)], zsem)

        def blk_start(b, c):
            blk_copy(b).start()
            return c

        def blk_wait(b, c):
            blk_copy(b).wait()
            return c

        n_blocks = xs_ref.shape[0] // bg
        lax.fori_loop(nv_ref[0], n_blocks, blk_start, 0)
        lax.fori_loop(nv_ref[0], n_blocks, blk_wait, 0)


def _dispatch(pad_lo, pad_hi, n_valid, dest_flat, h2, n_slots):
    N, D = h2.shape
    tm = TOKEN_TILE
    n_tiles = N // tm
    grid_spec = pltpu.PrefetchScalarGridSpec(
        num_scalar_prefetch=3,
        grid=(n_tiles,),
        in_specs=[pl.BlockSpec((tm * TOP_K,), lambda i, lo, hi, nv: (i,), memory_space=pltpu.SMEM),
                  pl.BlockSpec((tm, D), lambda i, lo, hi, nv: (i, 0))],
        out_specs=pl.BlockSpec(memory_space=pl.ANY),
        scratch_shapes=[pltpu.VMEM((1, D), F32), pltpu.VMEM((MOE_ROWS, D), F32),
                        pltpu.SemaphoreType.DMA(()), pltpu.SemaphoreType.DMA(())],
    )
    return pl.pallas_call(
        functools.partial(_dispatch_kernel, n_tiles=n_tiles),
        grid_spec=grid_spec,
        out_shape=jax.ShapeDtypeStruct((n_slots, D), F32),
        compiler_params=_params(("arbitrary",)),
        name="moe_dispatch",
    )(pad_lo, pad_hi, n_valid, dest_flat, h2)


def _moe_kernel(be_ref, nv_ref, xs_ref, wgu_ref, bgu_ref, wd_ref, bd_ref, y_ref):
    i = pl.program_id(0)

    @pl.when(i < nv_ref[0])
    def _():
        dff = wd_ref.shape[0]
        gu = _dot(xs_ref[...].astype(BF16), wgu_ref[...]) + bgu_ref[...]
        g = jnp.minimum(gu[:, :dff], SWIGLU_LIMIT)
        u = jnp.clip(gu[:, dff:], -SWIGLU_LIMIT, SWIGLU_LIMIT)
        act = g * _sigmoid(SWIGLU_ALPHA * g) * (u + 1.0)
        y_ref[...] = _dot(act.astype(BF16), wd_ref[...]) + bd_ref[...]

    @pl.when(i >= nv_ref[0])
    def _():
        y_ref[...] = jnp.zeros_like(y_ref)


def _moe_blocks(block_e, n_valid, xs, w_gu, b_gu, w_down, b_down):
    P, D = xs.shape
    E, _, F2 = w_gu.shape
    bm = MOE_ROWS
    grid_spec = pltpu.PrefetchScalarGridSpec(
        num_scalar_prefetch=2,
        grid=(P // bm,),
        in_specs=[pl.BlockSpec((bm, D), lambda i, be, nv: (i, 0)),
                  pl.BlockSpec((None, D, F2), lambda i, be, nv: (be[i], 0, 0)),
                  pl.BlockSpec((None, 1, F2), lambda i, be, nv: (be[i], 0, 0)),
                  pl.BlockSpec((None, F2 // 2, D), lambda i, be, nv: (be[i], 0, 0)),
                  pl.BlockSpec((None, 1, D), lambda i, be, nv: (be[i], 0, 0))],
        out_specs=pl.BlockSpec((bm, D), lambda i, be, nv: (i, 0)),
    )
    return pl.pallas_call(
        _moe_kernel,
        grid_spec=grid_spec,
        out_shape=jax.ShapeDtypeStruct((P, D), F32),
        compiler_params=_params(("arbitrary",)),
        name="moe_experts",
    )(block_e, n_valid, xs, w_gu, b_gu.reshape(E, 1, F2), w_down, b_down.reshape(E, 1, D))


def _combine_kernel(dcur_ref, dnxt_ref, x_ref, tg_ref, mod_ref, yb_ref, o_ref, ybuf, sem, *, n_tiles):
    i = pl.program_id(0)
    tm = x_ref.shape[0]
    slot = i % 2

    def issue(dref, s):
        for t in range(tm):
            for kk in range(TOP_K):
                pltpu.make_async_copy(yb_ref.at[pl.ds(dref[t * TOP_K + kk], 1)], ybuf.at[s, kk, pl.ds(t, 1)],
                                      sem.at[s]).start(priority=kk % 2)

    @pl.when(i == 0)
    def _():
        issue(dcur_ref, 0)

    @pl.when(i + 1 < n_tiles)
    def _():
        issue(dnxt_ref, 1 - slot)

    for kk in range(TOP_K):
        pltpu.make_async_copy(yb_ref.at[pl.ds(0, tm)], ybuf.at[slot, kk], sem.at[slot]).wait()
    tg = tg_ref[...]
    rows = lambda kk: ybuf[slot, kk]
    y = tg[:, 0:1] * rows(0)
    for kk in range(1, TOP_K):
        y = y + tg[:, kk:kk + 1] * rows(kk)
    o_ref[...] = x_ref[...] + mod_ref[5:6, :] * y


def _combine(dest_flat, x1, top_g, yb, mod, nb, tpb):
    N, D = x1.shape
    tm = TOKEN_TILE
    n_tiles = N // tm

    def mod_idx(i):
        return (jnp.where(i % tpb == 0, nb, i // tpb), 0, 0)

    kern = functools.partial(_combine_kernel, n_tiles=n_tiles)
    return pl.pallas_call(
        kern,
        grid=(n_tiles,),
        in_specs=[pl.BlockSpec((tm * TOP_K,), lambda i: (i,), memory_space=pltpu.SMEM),
                  pl.BlockSpec((tm * TOP_K,), lambda i: (jnp.minimum(i + 1, n_tiles - 1),),
                               memory_space=pltpu.SMEM),
                  pl.BlockSpec((tm, D), lambda i: (i, 0)),
                  pl.BlockSpec((tm, SUB), lambda i: (i, 0)),
                  pl.BlockSpec((None, 6, D), mod_idx),
                  pl.BlockSpec(memory_space=pl.ANY)],
        out_specs=pl.BlockSpec((tm, D), lambda i: (i, 0)),
        out_shape=jax.ShapeDtypeStruct((N, D), F32),
        scratch_shapes=[pltpu.VMEM((2, TOP_K, tm, D), F32), pltpu.SemaphoreType.DMA((2,))],
        compiler_params=_params(("arbitrary",)),
        name="moe_combine",
    )(dest_flat, dest_flat, x1, top_g, mod, yb)


def _final_kernel(x_ref, g_ref, o_ref):
    x = x_ref[...]
    ms = jnp.mean(x * x, axis=-1, keepdims=True)
    o_ref[...] = x * lax.rsqrt(ms + EPS) * g_ref[...]


def _final_norm(x, g, nb, tpb, S):
    N, D = x.shape
    tm = TOKEN_TILE
    lt = S // tm
    ct = tpb - lt
    out = pl.pallas_call(
        _final_kernel,
        grid=(nb, lt),
        in_specs=[pl.BlockSpec((tm, D), lambda b, i: (b * tpb + ct + i, 0)),
                  pl.BlockSpec((1, D), lambda b, i: (0, 0))],
        out_specs=pl.BlockSpec((tm, D), lambda b, i: (b * lt + i, 0)),
        out_shape=jax.ShapeDtypeStruct((nb * S, D), F32),
        compiler_params=_params(("arbitrary", "arbitrary")),
        name="final_norm",
    )(x, g.reshape(1, D))
    return out.reshape(nb, S, D)


def _rope_tables(S, L):
    half = GLA_DK // 2
    nfreq = half // 2
    inv_freq = ROPE_THETA ** (-jnp.arange(0, half, 2, dtype=F32) / half)
    t = jnp.arange(S)
    pos = jnp.stack([(t // GRID_W).astype(F32), (t % GRID_W).astype(F32)], axis=0)
    d = np.arange(GLA_QK) % GLA_DK
    axis = d // half
    e = d % half
    fidx = e % nfreq
    sign = np.where(e < nfreq, -1.0, 1.0).astype(np.float32)
    ang = pos[axis].T * inv_freq[fidx][None, :]
    cos = jnp.cos(ang)
    sin = jnp.sin(ang) * sign[None, :]
    cos = jnp.concatenate([jnp.ones((L, GLA_QK), F32), cos], axis=0)
    sin = jnp.concatenate([jnp.zeros((L, GLA_QK), F32), sin], axis=0)
    pad = jnp.zeros((L + S, QK_PAD - GLA_QK), F32)
    qs = GLA_DK ** -0.5
    cos_t = jnp.concatenate([cos * qs, pad, cos, pad], axis=1)
    sin_t = jnp.concatenate([sin * qs, pad, sin, pad], axis=1)
    return cos_t, sin_t


def _partner_perm():
    d = np.arange(GLA_QK) % GLA_DK
    e = d % (GLA_DK // 2)
    nfreq = GLA_DK // 4
    return np.where(e < nfreq, np.arange(GLA_QK) + nfreq, np.arange(GLA_QK) - nfreq)


def _layout_w_in(w_in):
    D = w_in.shape[0]
    o = np.cumsum([0, GLA_QK, GLA_QK, GLA_V, GLA_RANK, GLA_RANK, GLA_V, NA_W, NA_W, NA_W, POOL_WIDTH])
    piece = lambda n: w_in[:, o[n]:o[n + 1]]
    perm = _partner_perm()
    z = lambda n: jnp.zeros((D, n), w_in.dtype)
    qpad = z(QK_PAD - GLA_QK)
    cols = [piece(0), qpad, piece(1), qpad,
            piece(0)[:, perm], qpad, piece(1)[:, perm], qpad,
            piece(2), piece(5),
            piece(3), piece(4), z(LANE - 2 * GLA_RANK),
            piece(6), piece(7), piece(8), piece(9)]
    return jnp.concatenate(cols, axis=1).astype(BF16)


def _layout_gate(wa2, ba):
    w = jnp.zeros((2, LANE, GLA_QK), F32)
    w = w.at[0, 0:GLA_RANK].set(wa2[0]).at[1, GLA_RANK:2 * GLA_RANK].set(wa2[1])
    return w.astype(BF16), ba.reshape(2, 1, GLA_QK)


def _block_diag(pool_w):
    G = pool_w.shape[0]
    w = jnp.zeros((POOL_WIDTH, POOL_WIDTH), F32)
    for g in range(G):
        w = w.at[g * POOL_GROUP:(g + 1) * POOL_GROUP, g * POOL_GROUP:(g + 1) * POOL_GROUP].set(pool_w[g])
    return w.astype(BF16)


def _routing(top_e, rank, counts, n_blocks):
    bm = MOE_ROWS
    counts = counts[:, 0]
    padded = (counts + bm - 1) // bm * bm
    pend = jnp.cumsum(padded)
    pstart = pend - padded
    tm = top_e.shape[1]
    te = top_e.reshape(-1, SUB, tm)[:, :TOP_K]
    rk = rank.reshape(-1, SUB, tm)[:, :TOP_K]
    start_of = jnp.zeros_like(te)
    for e in range(N_EXPERTS):
        start_of = jnp.where(te == e, pstart[e], start_of)
    dest = jnp.transpose(start_of + rk, (0, 2, 1)).astype(jnp.int32).reshape(-1)
    first_row = jnp.arange(n_blocks, dtype=jnp.int32) * bm
    block_e = jnp.minimum(jnp.sum((pend[None, :] <= first_row[:, None]).astype(jnp.int32), axis=1),
                          N_EXPERTS - 1).astype(jnp.int32)
    n_valid = (pend[-1] // bm).astype(jnp.int32).reshape(1)
    pad_lo = (pstart + counts).astype(jnp.int32)
    return dest, block_e, n_valid, pad_lo, pend.astype(jnp.int32)


def kernel(x, c, ctx, c_ctx, norm1_g, norm2_g, w_ada, b_ada, w_in, gla_wa2, gla_ba, gla_norm_g, na_rpb,
           pool_w, pool_scale, w_out, router_w, router_b, w_gu, b_gu, w_down, b_down, final_g):
    B, S, D = x.shape
    L = ctx.shape[1]
    assert L == TOKEN_TILE and S % TOKEN_TILE == 0 and S % GRID_W == 0
    T = L + S
    tpb = T // TOKEN_TILE
    N = B * T
    A = N * TOP_K
    n_blocks = (A + MOE_ROWS - 1) // MOE_ROWS + N_EXPERTS

    xa = jnp.concatenate([ctx, x], axis=1).reshape(N, D)
    cond_rows = 8 * ((B + 1 + 7) // 8)
    cond = jnp.zeros((cond_rows, D), F32).at[:B].set(c).at[B].set(c_ctx)
    cos_t, sin_t = _rope_tables(S, L)

    def layer(xa, p):
        (n1, n2, wada, bada, win, wa2, ba, gng, rpb, pw, psc, wout, rw, rb, wgu, bgu, wdn, bdn) = p
        mod = _modulation(cond, wada, bada).reshape(cond_rows, 6, D)
        wa, bap = _layout_gate(wa2, ba)
        q, k, v, g, bf, bb, nq, nk, nv, pu = _input_projection(xa, mod, n1, _layout_w_in(win), cos_t, sin_t,
                                                               wa, bap, B, tpb)
        gla_o = _gla(q, k, v, g, bf, bb, gng, B, T, L)
        na_o = _neighbourhood_attention(nq, nk, nv, _na_bias_table(rpb, S // GRID_W), B, T, L)
        pool_o = _pool(pu, _block_diag(pw), psc, B, T, L)
        rwp = jnp.zeros((D, LANE), F32).at[:, :N_EXPERTS].set(rw)
        rbp = jnp.full((1, LANE), NEG, F32).at[0, :N_EXPERTS].set(rb)
        x1, h2, top_e, top_g, rank, counts = _output_projection(gla_o, na_o, pool_o, xa, mod, n2,
                                                                wout.astype(BF16), rwp, rbp, B, tpb)
        dest, block_e, n_valid, pad_lo, pad_hi = _routing(top_e, rank, counts, n_blocks)
        xs = _dispatch(pad_lo, pad_hi, n_valid, dest, h2, n_blocks * MOE_ROWS)
        yb = _moe_blocks(block_e, n_valid, xs, wgu.astype(BF16), bgu, wdn.astype(BF16), bdn)
        gates = jnp.transpose(top_g.reshape(-1, SUB, TOKEN_TILE), (0, 2, 1)).reshape(N, SUB)
        return _combine(dest, x1, gates, yb, mod, B, tpb), None

    params = (norm1_g, norm2_g, w_ada, b_ada, w_in, gla_wa2, gla_ba, gla_norm_g, na_rpb, pool_w, pool_scale,
              w_out, router_w, router_b, w_gu, b_gu, w_down, b_down)
    xa, _ = lax.scan(layer, xa, params)
    return _final_norm(xa, final_g, B, tpb, S)
```

```python
import functools

import numpy as np
import jax
import jax.numpy as jnp
from jax import lax
from jax.experimental import pallas as pl
from jax.experimental.pallas import tpu as pltpu

F32 = jnp.float32
BF16 = jnp.bfloat16

EPS = 1e-6
ROPE_THETA = 10000.0
GRID_W = 64

GLA_HEADS = 4
GLA_DK = 48
GLA_DV = 96
GLA_QK = GLA_HEADS * GLA_DK
GLA_V = GLA_HEADS * GLA_DV
GLA_RANK = 16
GLA_TAU = 16.0
GLA_CHUNK = 64

NA_HEADS = 6
NA_DH = 64
NA_W = NA_HEADS * NA_DH
NA_KH = 8
NA_KW = 16

POOL_WINDOWS = (2, 4, 8, 16)
POOL_WIDTH = 256
POOL_GROUP = 64
POOL_HALO = 8

N_EXPERTS = 32
TOP_K = 4
SWIGLU_LIMIT = 7.0
SWIGLU_ALPHA = 1.702

LANE = 128
TOKEN_TILE = 256
MOE_ROWS = 512
NEG = -1e30
VMEM_LIMIT = 56 * 1024 * 1024

QK_PAD = 256
C_QK = 0
C_SW = C_QK + 2 * QK_PAD
C_V = C_SW + 2 * QK_PAD
C_G = C_V + GLA_V
C_A = C_G + GLA_V
C_NQ = C_A + LANE
C_NK = C_NQ + NA_W
C_NV = C_NK + NA_W
C_PU = C_NV + NA_W
C_END = C_PU + POOL_WIDTH


def _dot(a, b):
    return jnp.dot(a, b, preferred_element_type=F32)


def _dot_nt(a, b):
    return lax.dot_general(a, b, (((1,), (1,)), ((), ())), preferred_element_type=F32)


def _split(x):
    hi = x.astype(BF16)
    lo = (x - hi.astype(F32)).astype(BF16)
    return hi, lo


def _sigmoid(x):
    return 1.0 / (1.0 + jnp.exp(-x))


def _params(sem):
    return pltpu.CompilerParams(dimension_semantics=sem, vmem_limit_bytes=VMEM_LIMIT)


def _mod_kernel(c_ref, w_ref, b_ref, o_ref):
    c = c_ref[...]
    s = c * _sigmoid(c)
    sh, sl = _split(s)
    wh, wl = _split(w_ref[...])
    o_ref[...] = _dot(sh, wh) + _dot(sl, wh) + _dot(sh, wl) + b_ref[...]


def _modulation(cond, w_ada, b_ada):
    R, D = cond.shape
    N = w_ada.shape[1]
    bn = 1024
    return pl.pallas_call(
        _mod_kernel,
        grid=(N // bn,),
        in_specs=[pl.BlockSpec((R, D), lambda j: (0, 0)),
                  pl.BlockSpec((D, bn), lambda j: (0, j)),
                  pl.BlockSpec((1, bn), lambda j: (0, j))],
        out_specs=pl.BlockSpec((R, bn), lambda j: (0, j)),
        out_shape=jax.ShapeDtypeStruct((R, N), F32),
        compiler_params=_params(("arbitrary",)),
        name="adaln_mod",
    )(cond, w_ada, b_ada.reshape(1, N))


def _inproj_kernel(x_ref, mod_ref, g_ref, w_ref, cos_ref, sin_ref, wa_ref, ba_ref, tri_ref,
                   q_ref, k_ref, v_ref, gg_ref, bf_ref, bb_ref, nq_ref, nk_ref, nv_ref, pu_ref):
    x = x_ref[...]
    ms = jnp.mean(x * x, axis=-1, keepdims=True)
    h = x * lax.rsqrt(ms + EPS) * g_ref[...]
    h = h * (1.0 + mod_ref[1:2, :]) + mod_ref[0:1, :]
    hb = h.astype(BF16)

    def proj(lo, hi):
        return _dot(hb, w_ref[:, lo:hi])

    rot = proj(C_QK, C_SW) * cos_ref[...] + proj(C_SW, C_V) * sin_ref[...]
    q_ref[...] = rot[:, 0:GLA_QK].astype(BF16)
    k_ref[...] = rot[:, QK_PAD:QK_PAD + GLA_QK].astype(BF16)
    v_ref[...] = proj(C_V, C_G).astype(BF16)
    gg_ref[...] = proj(C_G, C_A).astype(BF16)
    a = proj(C_A, C_NQ).astype(BF16)
    for d, out_ref in enumerate((bf_ref, bb_ref)):
        z = _dot(a, wa_ref[d]) + ba_ref[d]
        la = (jnp.minimum(z, 0.0) - jnp.log(1.0 + jnp.exp(-jnp.abs(z)))) * (1.0 / GLA_TAU)
        lh, ll = _split(la)
        out_ref[...] = _dot(tri_ref[d], lh) + _dot(tri_ref[d], ll)
    nq_ref[...] = (proj(C_NQ, C_NK) * (NA_DH ** -0.5)).astype(BF16)
    nk_ref[...] = proj(C_NK, C_NV).astype(BF16)
    nv_ref[...] = proj(C_NV, C_PU).astype(BF16)
    pu_ref[...] = proj(C_PU, C_END)


def _chunk_triangles():
    r = np.arange(TOKEN_TILE)[:, None]
    c = np.arange(TOKEN_TILE)[None, :]
    same = (r // GLA_CHUNK) == (c // GLA_CHUNK)
    return jnp.asarray(np.stack([same & (c <= r), same & (c >= r)]).astype(np.float32), dtype=BF16)


def _input_projection(x, mod, norm_g, w, cos_t, sin_t, wa, ba, nb, tpb):
    N, D = x.shape
    tm = TOKEN_TILE

    def mod_idx(i):
        return (jnp.where(i % tpb == 0, nb, i // tpb), 0, 0)

    tok = lambda w_: pl.BlockSpec((tm, w_), lambda i: (i, 0))
    outs = [(GLA_QK, BF16), (GLA_QK, BF16), (GLA_V, BF16), (GLA_V, BF16), (GLA_QK, F32), (GLA_QK, F32),
            (NA_W, BF16), (NA_W, BF16), (NA_W, BF16), (POOL_WIDTH, F32)]
    return pl.pallas_call(
        _inproj_kernel,
        grid=(N // tm,),
        in_specs=[tok(D),
                  pl.BlockSpec((None, 6, D), mod_idx),
                  pl.BlockSpec((1, D), lambda i: (0, 0)),
                  pl.BlockSpec((D, C_END), lambda i: (0, 0)),
                  pl.BlockSpec((tm, 2 * QK_PAD), lambda i: (i % tpb, 0)),
                  pl.BlockSpec((tm, 2 * QK_PAD), lambda i: (i % tpb, 0)),
                  pl.BlockSpec((2, LANE, GLA_QK), lambda i: (0, 0, 0)),
                  pl.BlockSpec((2, 1, GLA_QK), lambda i: (0, 0, 0)),
                  pl.BlockSpec((2, tm, tm), lambda i: (0, 0, 0))],
        out_specs=[tok(w_) for w_, _ in outs],
        out_shape=[jax.ShapeDtypeStruct((N, w_), dt) for w_, dt in outs],
        compiler_params=_params(("arbitrary",)),
        name="in_proj",
    )(x, mod, norm_g.reshape(1, D), w, cos_t, sin_t, wa, ba, _chunk_triangles())


def _lane_head(idx, width):
    h = jnp.zeros_like(idx)
    for t in range(1, GLA_HEADS):
        h = h + (idx >= t * width).astype(jnp.int32)
    return h


def _gla_kernel(q_ref, k_ref, v_ref, g_ref, bf_ref, bb_ref, ng_ref, o_ref,
                stf_ref, stb_ref, of_ref, ob_ref, *, n_ctx, n_all):
    assert n_all % 2 == 0
    C = GLA_CHUNK
    i32 = jnp.int32
    kmask = (lax.broadcasted_iota(i32, (GLA_HEADS * C, GLA_QK), 0) // C
             == _lane_head(lax.broadcasted_iota(i32, (GLA_HEADS * C, GLA_QK), 1), GLA_DK))
    vmask = (lax.broadcasted_iota(i32, (GLA_HEADS * C, GLA_V), 0) // C
             == _lane_head(lax.broadcasted_iota(i32, (GLA_HEADS * C, GLA_V), 1), GLA_DV))
    smask = (_lane_head(lax.broadcasted_iota(i32, (GLA_V, GLA_QK), 0), GLA_DV)
             == _lane_head(lax.broadcasted_iota(i32, (GLA_V, GLA_QK), 1), GLA_DK))
    hind = jnp.where(_lane_head(lax.broadcasted_iota(i32, (GLA_V, GLA_V), 0), GLA_DV)
                     == _lane_head(lax.broadcasted_iota(i32, (GLA_V, GLA_V), 1), GLA_DV),
                     1.0, 0.0).astype(BF16)
    qi_ = lax.broadcasted_iota(i32, (C, GLA_HEADS * C), 0)
    kj_ = lax.broadcasted_iota(i32, (C, GLA_HEADS * C), 1) % C
    causal_f = qi_ >= kj_
    causal_b = qi_ <= kj_

    def chunk(ci, fwd):
        st_ref = stf_ref if fwd else stb_ref
        r0 = pl.multiple_of(ci * C, C)
        rows = pl.ds(r0, C)
        q = q_ref[rows, :].astype(F32)
        k = k_ref[rows, :].astype(F32)
        v = v_ref[rows, :]
        b = (bf_ref if fwd else bb_ref)[rows, :]
        if fwd:
            bmid = b[C // 2 - 1:C // 2, :]
            btot = b[C - 1:C, :]
        else:
            bmid = b[C // 2:C // 2 + 1, :]
            btot = b[0:1, :]
        qe = (q * jnp.exp(b - bmid)).astype(BF16)
        ke = (k * jnp.exp(bmid - b)).astype(BF16)
        qs = (q * jnp.exp(b)).astype(BF16)
        kt = (k * jnp.exp(btot - b)).astype(BF16)
        zero = jnp.zeros((), BF16)
        kbd = jnp.where(kmask, jnp.concatenate([ke] * GLA_HEADS, axis=0), zero)
        vbd = jnp.where(vmask, jnp.concatenate([v] * GLA_HEADS, axis=0), zero)
        sc = _dot_nt(qe, kbd)
        sc = jnp.where(causal_f if fwd else causal_b, sc, 0.0)
        st = st_ref[...]
        o = _dot(sc.astype(BF16), vbd) + _dot_nt(qs, st.astype(BF16))
        vt = v.astype(F32).T.astype(BF16)
        ds = _dot(vt, kt)
        st_ref[...] = st * jnp.exp(btot) + jnp.where(smask, ds, 0.0)
        (of_ref if fwd else ob_ref)[rows, :] = o

    def finish(ci):
        rows = pl.ds(pl.multiple_of(ci * C, C), C)
        tot = of_ref[rows, :] + ob_ref[rows, :]
        th, tl = _split(tot * tot)
        ms = (_dot(th, hind) + _dot(tl, hind)) * (1.0 / GLA_DV)
        g = g_ref[rows, :].astype(F32)
        y = tot * lax.rsqrt(ms + EPS) * ng_ref[...] * (g * _sigmoid(g))
        o_ref[rows, :] = y.astype(o_ref.dtype)

    def both(f0, b0, count):
        def body(i, carry):
            chunk(f0 + i, True)
            chunk(b0 - i, False)
            return carry
        lax.fori_loop(0, count, body, 0)

    stf_ref[...] = jnp.zeros_like(stf_ref)
    stb_ref[...] = jnp.zeros_like(stb_ref)
    both(0, n_ctx - 1, n_ctx)
    both(n_ctx, n_all - 1, n_all - n_ctx)

    def fin_body(i, carry):
        finish(2 * i)
        finish(2 * i + 1)
        return carry
    lax.fori_loop(0, n_all // 2, fin_body, 0)


def _gla(q, k, v, g, bf, bb, norm_g, nb, T, L):
    seq = lambda w_: pl.BlockSpec((T, w_), lambda b: (b, 0))
    once = lambda w_: pl.BlockSpec((T, w_), lambda b: (b, 0), pipeline_mode=pl.Buffered(1))
    kern = functools.partial(_gla_kernel, n_ctx=L // GLA_CHUNK, n_all=T // GLA_CHUNK)
    return pl.pallas_call(
        kern,
        grid=(nb,),
        in_specs=[seq(GLA_QK), seq(GLA_QK), seq(GLA_V), seq(GLA_V), once(GLA_QK), once(GLA_QK),
                  pl.BlockSpec((1, GLA_V), lambda b: (0, 0))],
        out_specs=seq(GLA_V),
        out_shape=jax.ShapeDtypeStruct((nb * T, GLA_V), BF16),
        scratch_shapes=[pltpu.VMEM((GLA_V, GLA_QK), F32), pltpu.VMEM((GLA_V, GLA_QK), F32),
                        pltpu.VMEM((T, GLA_V), F32), pltpu.VMEM((T, GLA_V), F32)],
        compiler_params=_params(("arbitrary",)),
        name="gla",
    )(q, k, v, g, bf, bb, norm_g.reshape(1, GLA_V))


NA_GROUP = TOKEN_TILE // GRID_W
NA_WIN = NA_KH + NA_GROUP - 1


def _na_window_start(group, n_rows):
    return jnp.clip(group * NA_GROUP - NA_KH // 2, 0, n_rows - NA_WIN)


def _na_kernel(q_ref, k_ref, v_ref, bias_ref, o_ref, *, n_rows, L):
    j = pl.program_id(1)
    W = GRID_W
    Q = q_ref.shape[0]
    lane = lax.broadcasted_iota(jnp.int32, (2 * Q, LANE), 1)
    row = lax.broadcasted_iota(jnp.int32, (2 * Q, LANE), 0)
    qmask = (row < Q) == (lane < NA_DH)
    out_lo = lax.broadcasted_iota(jnp.int32, (Q, LANE), 1) < NA_DH
    zero = jnp.zeros((), BF16)

    def pair_q(p):
        q2 = q_ref[:, p * LANE:(p + 1) * LANE]
        return jnp.where(qmask, jnp.concatenate([q2, q2], axis=0), zero)

    def finish(p, acc, l):
        o2 = acc / l
        o_ref[:, p * LANE:(p + 1) * LANE] = jnp.where(out_lo, o2[:Q], o2[Q:]).astype(o_ref.dtype)

    @pl.when(j == 0)
    def _():
        for p in range(NA_HEADS // 2):
            cols = slice(p * LANE, (p + 1) * LANE)
            s = _dot_nt(pair_q(p), k_ref[0:L, cols])
            m = jnp.max(s, axis=-1, keepdims=True)
            e = jnp.exp(s - m)
            l = jnp.sum(e, axis=-1, keepdims=True)
            finish(p, _dot(e.astype(BF16), v_ref[0:L, cols]), l)

    @pl.when(j > 0)
    def _():
        k0 = pl.multiple_of(L + _na_window_start(j - 1, n_rows) * W, W)
        win = pl.ds(k0, NA_WIN * W)
        for p in range(NA_HEADS // 2):
            cols = slice(p * LANE, (p + 1) * LANE)
            qb = pair_q(p)
            s_w = _dot_nt(qb, k_ref[win, cols]) + bias_ref[p]
            s_c = _dot_nt(qb, k_ref[0:L, cols])
            m = jnp.maximum(jnp.max(s_w, axis=-1, keepdims=True), jnp.max(s_c, axis=-1, keepdims=True))
            e_w = jnp.exp(s_w - m)
            e_c = jnp.exp(s_c - m)
            l = jnp.sum(e_w, axis=-1, keepdims=True) + jnp.sum(e_c, axis=-1, keepdims=True)
            acc = _dot(e_w.astype(BF16), v_ref[win, cols]) + _dot(e_c.astype(BF16), v_ref[0:L, cols])
            finish(p, acc, l)


def _na_bias_table(rpb, n_rows):
    W, G = GRID_W, NA_GROUP
    n_groups = n_rows // G
    assert n_rows >= NA_WIN and n_groups >= 3
    qc = np.arange(W)[:, None]
    kc = np.arange(W)[None, :]
    cs = np.clip(qc - NA_KW // 2, 0, W - NA_KW)
    cvalid = (kc >= cs) & (kc < cs + NA_KW)
    cidx = np.clip(kc - qc + NA_KW - 1, 0, 2 * NA_KW - 2)
    rvalid, ridx = [], []
    for grp in (0, 1, n_groups - 1):
        ws = int(np.clip(grp * G - NA_KH // 2, 0, n_rows - NA_WIN))
        r = (grp * G + np.arange(G))[:, None]
        rs = np.clip(r - NA_KH // 2, 0, n_rows - NA_KH)
        key_row = (ws + np.arange(NA_WIN))[None, :]
        rvalid.append((key_row >= rs) & (key_row < rs + NA_KH))
        ridx.append(np.clip(key_row - r + NA_KH - 1, 0, 2 * NA_KH - 2))
    e_r = jnp.asarray(np.eye(2 * NA_KH - 1, dtype=np.float32)[np.stack(ridx)])
    e_c = jnp.asarray(np.eye(2 * NA_KW - 1, dtype=np.float32)[cidx])
    t = jnp.einsum('vria,hab,qkb->vhrqik', e_r, rpb.astype(F32), e_c, precision=lax.Precision.HIGHEST)
    valid = np.stack(rvalid)[:, None, :, None, :, None] & cvalid[None, None, None, :, None, :]
    t = jnp.where(jnp.asarray(valid), t, NEG)
    return t.reshape(3, NA_HEADS // 2, 2 * G * W, NA_WIN * W)


def _neighbourhood_attention(nq, nk, nv, bias, nb, T, L):
    W = GRID_W
    n_rows = (T - L) // W
    n_groups = n_rows // NA_GROUP
    Q = NA_GROUP * W
    spb = T // Q

    def bias_idx(b, j):
        return (jnp.where(j <= 1, 0, jnp.where(j == n_groups, 2, 1)), 0, 0, 0)

    kern = functools.partial(_na_kernel, n_rows=n_rows, L=L)
    return pl.pallas_call(
        kern,
        grid=(nb, spb),
        in_specs=[pl.BlockSpec((Q, NA_W), lambda b, j: (b * spb + j, 0)),
                  pl.BlockSpec((T, NA_W), lambda b, j: (b, 0)),
                  pl.BlockSpec((T, NA_W), lambda b, j: (b, 0)),
                  pl.BlockSpec((None, NA_HEADS // 2, 2 * Q, NA_WIN * W), bias_idx)],
        out_specs=pl.BlockSpec((Q, NA_W), lambda b, j: (b * spb + j, 0)),
        out_shape=jax.ShapeDtypeStruct((nb * T, NA_W), BF16),
        compiler_params=_params(("arbitrary", "arbitrary")),
        name="neigh_attn",
    )(nq, nk, nv, bias)


def _pool_kernel(u_ref, w_ref, sc_ref, o_ref, *, T, L):
    i = pl.program_id(1)
    tm = TOKEN_TILE
    ext = tm + 2 * POOL_HALO
    t0 = i * tm
    e0 = pl.multiple_of(jnp.clip(t0 - POOL_HALO, 0, T - ext), 8)
    seg_lo = jnp.where(t0 < L, 0, L)
    seg_hi = jnp.where(t0 < L, L, T)
    uh, ul = _split(u_ref[pl.ds(e0, ext), :])
    t = t0 + lax.broadcasted_iota(jnp.int32, (tm, ext), 0)
    c = e0 + lax.broadcasted_iota(jnp.int32, (tm, ext), 1)
    tcol = t0 + lax.broadcasted_iota(jnp.int32, (tm, 1), 0)
    lane = lax.broadcasted_iota(jnp.int32, (tm, POOL_WIDTH), 1)
    y = jnp.zeros((tm, POOL_WIDTH), F32)
    for gi, w in enumerate(POOL_WINDOWS):
        lo = jnp.maximum(t - w // 2, seg_lo)
        hi = jnp.minimum(t + w - w // 2, seg_hi)
        band = jnp.where((c >= lo) & (c < hi), 1.0, 0.0).astype(BF16)
        cnt = (jnp.minimum(tcol + w - w // 2, seg_hi) - jnp.maximum(tcol - w // 2, seg_lo)).astype(F32)
        s = (_dot(band, uh) + _dot(band, ul)) / cnt
        y = jnp.where((lane >= gi * POOL_GROUP) & (lane < (gi + 1) * POOL_GROUP), s, y)
    y = y - u_ref[pl.ds(pl.multiple_of(t0, tm), tm), :]
    o_ref[...] = (_dot(y.astype(BF16), w_ref[...]) * sc_ref[...]).astype(o_ref.dtype)


def _pool(u, w_bd, scale, nb, T, L):
    tpb = T // TOKEN_TILE
    kern = functools.partial(_pool_kernel, T=T, L=L)
    return pl.pallas_call(
        kern,
        grid=(nb, tpb),
        in_specs=[pl.BlockSpec((T, POOL_WIDTH), lambda b, i: (b, 0)),
                  pl.BlockSpec((POOL_WIDTH, POOL_WIDTH), lambda b, i: (0, 0)),
                  pl.BlockSpec((1, POOL_WIDTH), lambda b, i: (0, 0))],
        out_specs=pl.BlockSpec((TOKEN_TILE, POOL_WIDTH), lambda b, i: (b * tpb + i, 0)),
        out_shape=jax.ShapeDtypeStruct((nb * T, POOL_WIDTH), BF16),
        compiler_params=_params(("arbitrary", "arbitrary")),
        name="pool_mix",
    )(u, w_bd, scale.reshape(1, POOL_WIDTH))


def _outproj_kernel(gla_ref, na_ref, pm_ref, x_ref, mod_ref, g_ref, wo_ref, rw_ref, rb_ref,
                    x1_ref, h2_ref, te_ref, tg_ref, rk_ref, cnt_ref, carry_ref):
    @pl.when(pl.program_id(0) == 0)
    def _():
        carry_ref[...] = jnp.zeros_like(carry_ref)

    mx = (_dot(gla_ref[...], wo_ref[0:GLA_V, :])
          + _dot(na_ref[...], wo_ref[GLA_V:GLA_V + NA_W, :])
          + _dot(pm_ref[...], wo_ref[GLA_V + NA_W:, :]))
    x1 = x_ref[...] + mod_ref[2:3, :] * mx
    x1_ref[...] = x1
    ms = jnp.mean(x1 * x1, axis=-1, keepdims=True)
    h = x1 * lax.rsqrt(ms + EPS) * g_ref[...]
    h = h * (1.0 + mod_ref[4:5, :]) + mod_ref[3:4, :]
    h2_ref[...] = h
    hh, hl = _split(h)
    wh, wl = _split(rw_ref[...])
    logits = _dot(hh, wh) + _dot(hl, wh) + _dot(hh, wl) + rb_ref[...]
    tm = logits.shape[0]
    E = N_EXPERTS
    lt = logits.T[0:E, :]
    erow = lax.broadcasted_iota(jnp.int32, (E, tm), 0)
    cur = lt
    vals, idxs = [], []
    for kk in range(TOP_K):
        m = jnp.max(cur, axis=0, keepdims=True)
        idx = jnp.min(jnp.where(cur == m, erow, E), axis=0, keepdims=True)
        vals.append(m)
        idxs.append(idx)
        cur = jnp.where(erow == idx, NEG, cur)
    expv = [jnp.exp(v - vals[0]) for v in vals]
    den = sum(expv)
    hot = sum(jnp.where(erow == idx, 1.0, 0.0) for idx in idxs)
    before = (lax.broadcasted_iota(jnp.int32, (tm, tm), 0) < lax.broadcasted_iota(jnp.int32, (tm, tm), 1))
    base = _dot(hot.astype(BF16), jnp.where(before, 1.0, 0.0).astype(BF16)) + carry_ref[...]
    krow = lax.broadcasted_iota(jnp.int32, (SUB, tm), 0)
    te = jnp.zeros((SUB, tm), jnp.int32)
    rk = jnp.zeros((SUB, tm), jnp.int32)
    tv = jnp.zeros((SUB, tm), F32)
    for kk in range(TOP_K):
        r = jnp.sum(jnp.where(erow == idxs[kk], base, 0.0), axis=0, keepdims=True)
        te = jnp.where(krow == kk, idxs[kk], te)
        rk = jnp.where(krow == kk, r.astype(jnp.int32), rk)
        tv = jnp.where(krow == kk, expv[kk] / den, tv)
    te_ref[...] = te
    rk_ref[...] = rk
    tg_ref[...] = tv
    carry_ref[...] = carry_ref[...] + jnp.sum(hot, axis=1, keepdims=True)
    cnt_ref[...] = jnp.broadcast_to(carry_ref[...], cnt_ref.shape).astype(jnp.int32)


def _output_projection(gla, na, pm, x, mod, norm_g, w_out, rw, rb, nb, tpb):
    N, D = x.shape
    tm = TOKEN_TILE

    def mod_idx(i):
        return (jnp.where(i % tpb == 0, nb, i // tpb), 0, 0)

    tok = lambda w_: pl.BlockSpec((tm, w_), lambda i: (i, 0))
    n_tiles = N // tm
    route = pl.BlockSpec((SUB, tm), lambda i: (i, 0))
    return pl.pallas_call(
        _outproj_kernel,
        grid=(n_tiles,),
        in_specs=[tok(GLA_V), tok(NA_W), tok(POOL_WIDTH), tok(D),
                  pl.BlockSpec((None, 6, D), mod_idx),
                  pl.BlockSpec((1, D), lambda i: (0, 0)),
                  pl.BlockSpec((D, D), lambda i: (0, 0)),
                  pl.BlockSpec((D, LANE), lambda i: (0, 0)),
                  pl.BlockSpec((1, LANE), lambda i: (0, 0))],
        out_specs=[tok(D), tok(D), route, route, route,
                   pl.BlockSpec((N_EXPERTS, LANE), lambda i: (0, 0))],
        out_shape=[jax.ShapeDtypeStruct((N, D), F32), jax.ShapeDtypeStruct((N, D), F32),
                   jax.ShapeDtypeStruct((n_tiles * SUB, tm), jnp.int32),
                   jax.ShapeDtypeStruct((n_tiles * SUB, tm), F32),
                   jax.ShapeDtypeStruct((n_tiles * SUB, tm), jnp.int32),
                   jax.ShapeDtypeStruct((N_EXPERTS, LANE), jnp.int32)],
        scratch_shapes=[pltpu.VMEM((N_EXPERTS, 1), F32)],
        compiler_params=_params(("arbitrary",)),
        name="out_proj_router",
    )(gla, na, pm, x, mod, norm_g.reshape(1, D), w_out, rw, rb)


SUB = 8


def _dispatch_kernel(plo_ref, phi_ref, nv_ref, dest_ref, h_ref, xs_ref, zrow, zblk, sem, zsem, *, n_tiles):
    tm = h_ref.shape[0]
    for t in range(tm):
        for kk in range(TOP_K):
            d = dest_ref[t * TOP_K + kk]
            pltpu.make_async_copy(h_ref.at[pl.ds(t, 1)], xs_ref.at[pl.ds(d, 1)], sem).start(priority=kk % 2)
    for kk in range(TOP_K):
        pltpu.make_async_copy(h_ref, xs_ref.at[pl.ds(0, tm)], sem).wait()

    @pl.when(pl.program_id(0) == n_tiles - 1)
    def _():
        zrow[...] = jnp.zeros_like(zrow)

        def pad_copy(r):
            return pltpu.make_async_copy(zrow, xs_ref.at[pl.ds(r, 1)], zsem)

        def per_expert(e, carry):
            def start(r, c):
                pad_copy(r).start()
                return c

            def wait(r, c):
                pad_copy(r).wait()
                return c

            lax.fori_loop(plo_ref[e], phi_ref[e], start, 0)
            lax.fori_loop(plo_ref[e], phi_ref[e], wait, 0)
            return carry

        lax.fori_loop(0, N_EXPERTS, per_expert, 0)

        zblk[...] = jnp.zeros_like(zblk)
        bg = zblk.shape[0]

        def blk_copy(b):
            return pltpu.make_async_copy(zblk, xs_ref.at[pl.ds(pl.multiple_of(b * bg, SUB), bg)], zsem)

        def blk_start(b, c):
            blk_copy(b).start()
            return c

        def blk_wait(b, c):
            blk_copy(b).wait()
            return c

        n_blocks = xs_ref.shape[0] // bg
        lax.fori_loop(nv_ref[0], n_blocks, blk_start, 0)
        lax.fori_loop(nv_ref[0], n_blocks, blk_wait, 0)


def _dispatch(pad_lo, pad_hi, n_valid, dest_flat, h2, n_slots):
    N, D = h2.shape
    tm = TOKEN_TILE
    n_tiles = N // tm
    grid_spec = pltpu.PrefetchScalarGridSpec(
        num_scalar_prefetch=3,
        grid=(n_tiles,),
        in_specs=[pl.BlockSpec((tm * TOP_K,), lambda i, lo, hi, nv: (i,), memory_space=pltpu.SMEM),
                  pl.BlockSpec((tm, D), lambda i, lo, hi, nv: (i, 0))],
        out_specs=pl.BlockSpec(memory_space=pl.ANY),
        scratch_shapes=[pltpu.VMEM((1, D), F32), pltpu.VMEM((MOE_ROWS, D), F32),
                        pltpu.SemaphoreType.DMA(()), pltpu.SemaphoreType.DMA(())],
    )
    return pl.pallas_call(
        functools.partial(_dispatch_kernel, n_tiles=n_tiles),
        grid_spec=grid_spec,
        out_shape=jax.ShapeDtypeStruct((n_slots, D), F32),
        compiler_params=_params(("arbitrary",)),
        name="moe_dispatch",
    )(pad_lo, pad_hi, n_valid, dest_flat, h2)


def _moe_kernel(be_ref, nv_ref, xs_ref, wgu_ref, bgu_ref, wd_ref, bd_ref, y_ref, wgu_bf, wd_bf):
    i = pl.program_id(0)
    valid = i < nv_ref[0]
    new_expert = jnp.logical_or(i == 0, be_ref[i] != be_ref[jnp.maximum(i - 1, 0)])

    @pl.when(jnp.logical_and(valid, new_expert))
    def _():
        wgu_bf[...] = wgu_ref[...].astype(BF16)
        wd_bf[...] = wd_ref[...].astype(BF16)

    @pl.when(valid)
    def _():
        dff = wd_ref.shape[0]
        gu = _dot(xs_ref[...].astype(BF16), wgu_bf[...]) + bgu_ref[...]
        g = jnp.minimum(gu[:, :dff], SWIGLU_LIMIT)
        u = jnp.clip(gu[:, dff:], -SWIGLU_LIMIT, SWIGLU_LIMIT)
        act = g * _sigmoid(SWIGLU_ALPHA * g) * (u + 1.0)
        y_ref[...] = _dot(act.astype(BF16), wd_bf[...]) + bd_ref[...]

    @pl.when(i >= nv_ref[0])
    def _():
        y_ref[...] = jnp.zeros_like(y_ref)


def _moe_blocks(block_e, n_valid, xs, w_gu, b_gu, w_down, b_down):
    P, D = xs.shape
    E, _, F2 = w_gu.shape
    bm = MOE_ROWS
    grid_spec = pltpu.PrefetchScalarGridSpec(
        num_scalar_prefetch=2,
        grid=(P // bm,),
        in_specs=[pl.BlockSpec((bm, D), lambda i, be, nv: (i, 0)),
                  pl.BlockSpec((None, D, F2), lambda i, be, nv: (be[i], 0, 0)),
                  pl.BlockSpec((None, 1, F2), lambda i, be, nv: (be[i], 0, 0)),
                  pl.BlockSpec((None, F2 // 2, D), lambda i, be, nv: (be[i], 0, 0)),
                  pl.BlockSpec((None, 1, D), lambda i, be, nv: (be[i], 0, 0))],
        out_specs=pl.BlockSpec((bm, D), lambda i, be, nv: (i, 0)),
        scratch_shapes=[pltpu.VMEM((D, F2), BF16), pltpu.VMEM((F2 // 2, D), BF16)],
    )
    return pl.pallas_call(
        _moe_kernel,
        grid_spec=grid_spec,
        out_shape=jax.ShapeDtypeStruct((P, D), F32),
        compiler_params=_params(("arbitrary",)),
        name="moe_experts",
    )(block_e, n_valid, xs, w_gu, b_gu.reshape(E, 1, F2), w_down, b_down.reshape(E, 1, D))


def _combine_kernel(dcur_ref, dnxt_ref, x_ref, tg_ref, mod_ref, yb_ref, o_ref, ybuf, sem, *, n_tiles):
    i = pl.program_id(0)
    tm = x_ref.shape[0]
    slot = i % 2

    def issue(dref, s):
        for t in range(tm):
            for kk in range(TOP_K):
                pltpu.make_async_copy(yb_ref.at[pl.ds(dref[t * TOP_K + kk], 1)], ybuf.at[s, kk, pl.ds(t, 1)],
                                      sem.at[s]).start(priority=kk % 2)

    @pl.when(i == 0)
    def _():
        issue(dcur_ref, 0)

    @pl.when(i + 1 < n_tiles)
    def _():
        issue(dnxt_ref, 1 - slot)

    for kk in range(TOP_K):
        pltpu.make_async_copy(yb_ref.at[pl.ds(0, tm)], ybuf.at[slot, kk], sem.at[slot]).wait()
    tg = tg_ref[...]
    rows = lambda kk: ybuf[slot, kk]
    y = tg[:, 0:1] * rows(0)
    for kk in range(1, TOP_K):
        y = y + tg[:, kk:kk + 1] * rows(kk)
    o_ref[...] = x_ref[...] + mod_ref[5:6, :] * y


def _combine(dest_flat, x1, top_g, yb, mod, nb, tpb):
    N, D = x1.shape
    tm = TOKEN_TILE
    n_tiles = N // tm

    def mod_idx(i):
        return (jnp.where(i % tpb == 0, nb, i // tpb), 0, 0)

    kern = functools.partial(_combine_kernel, n_tiles=n_tiles)
    return pl.pallas_call(
        kern,
        grid=(n_tiles,),
        in_specs=[pl.BlockSpec((tm * TOP_K,), lambda i: (i,), memory_space=pltpu.SMEM),
                  pl.BlockSpec((tm * TOP_K,), lambda i: (jnp.minimum(i + 1, n_tiles - 1),),
                               memory_space=pltpu.SMEM),
                  pl.BlockSpec((tm, D), lambda i: (i, 0)),
                  pl.BlockSpec((tm, SUB), lambda i: (i, 0)),
                  pl.BlockSpec((None, 6, D), mod_idx),
                  pl.BlockSpec(memory_space=pl.ANY)],
        out_specs=pl.BlockSpec((tm, D), lambda i: (i, 0)),
        out_shape=jax.ShapeDtypeStruct((N, D), F32),
        scratch_shapes=[pltpu.VMEM((2, TOP_K, tm, D), F32), pltpu.SemaphoreType.DMA((2,))],
        compiler_params=_params(("arbitrary",)),
        name="moe_combine",
    )(dest_flat, dest_flat, x1, top_g, mod, yb)


def _final_kernel(x_ref, g_ref, o_ref):
    x = x_ref[...]
    ms = jnp.mean(x * x, axis=-1, keepdims=True)
    o_ref[...] = x * lax.rsqrt(ms + EPS) * g_ref[...]


def _final_norm(x, g, nb, tpb, S):
    N, D = x.shape
    tm = TOKEN_TILE
    lt = S // tm
    ct = tpb - lt
    out = pl.pallas_call(
        _final_kernel,
        grid=(nb, lt),
        in_specs=[pl.BlockSpec((tm, D), lambda b, i: (b * tpb + ct + i, 0)),
                  pl.BlockSpec((1, D), lambda b, i: (0, 0))],
        out_specs=pl.BlockSpec((tm, D), lambda b, i: (b * lt + i, 0)),
        out_shape=jax.ShapeDtypeStruct((nb * S, D), F32),
        compiler_params=_params(("arbitrary", "arbitrary")),
        name="final_norm",
    )(x, g.reshape(1, D))
    return out.reshape(nb, S, D)


def _rope_tables(S, L):
    half = GLA_DK // 2
    nfreq = half // 2
    inv_freq = ROPE_THETA ** (-jnp.arange(0, half, 2, dtype=F32) / half)
    t = jnp.arange(S)
    pos = jnp.stack([(t // GRID_W).astype(F32), (t % GRID_W).astype(F32)], axis=0)
    d = np.arange(GLA_QK) % GLA_DK
    axis = d // half
    e = d % half
    fidx = e % nfreq
    sign = np.where(e < nfreq, -1.0, 1.0).astype(np.float32)
    ang = pos[axis].T * inv_freq[fidx][None, :]
    cos = jnp.cos(ang)
    sin = jnp.sin(ang) * sign[None, :]
    cos = jnp.concatenate([jnp.ones((L, GLA_QK), F32), cos], axis=0)
    sin = jnp.concatenate([jnp.zeros((L, GLA_QK), F32), sin], axis=0)
    pad = jnp.zeros((L + S, QK_PAD - GLA_QK), F32)
    qs = GLA_DK ** -0.5
    cos_t = jnp.concatenate([cos * qs, pad, cos, pad], axis=1)
    sin_t = jnp.concatenate([sin * qs, pad, sin, pad], axis=1)
    return cos_t, sin_t


def _partner_perm():
    d = np.arange(GLA_QK) % GLA_DK
    e = d % (GLA_DK // 2)
    nfreq = GLA_DK // 4
    return np.where(e < nfreq, np.arange(GLA_QK) + nfreq, np.arange(GLA_QK) - nfreq)


def _layout_w_in(w_in):
    D = w_in.shape[0]
    o = np.cumsum([0, GLA_QK, GLA_QK, GLA_V, GLA_RANK, GLA_RANK, GLA_V, NA_W, NA_W, NA_W, POOL_WIDTH])
    piece = lambda n: w_in[:, o[n]:o[n + 1]]
    perm = _partner_perm()
    z = lambda n: jnp.zeros((D, n), w_in.dtype)
    qpad = z(QK_PAD - GLA_QK)
    cols = [piece(0), qpad, piece(1), qpad,
            piece(0)[:, perm], qpad, piece(1)[:, perm], qpad,
            piece(2), piece(5),
            piece(3), piece(4), z(LANE - 2 * GLA_RANK),
            piece(6), piece(7), piece(8), piece(9)]
    return jnp.concatenate(cols, axis=1).astype(BF16)


def _layout_gate(wa2, ba):
    w = jnp.zeros((2, LANE, GLA_QK), F32)
    w = w.at[0, 0:GLA_RANK].set(wa2[0]).at[1, GLA_RANK:2 * GLA_RANK].set(wa2[1])
    return w.astype(BF16), ba.reshape(2, 1, GLA_QK)


def _block_diag(pool_w):
    G = pool_w.shape[0]
    w = jnp.zeros((POOL_WIDTH, POOL_WIDTH), F32)
    for g in range(G):
        w = w.at[g * POOL_GROUP:(g + 1) * POOL_GROUP, g * POOL_GROUP:(g + 1) * POOL_GROUP].set(pool_w[g])
    return w.astype(BF16)


def _routing(top_e, rank, counts, n_blocks):
    bm = MOE_ROWS
    counts = counts[:, 0]
    padded = (counts + bm - 1) // bm * bm
    pend = jnp.cumsum(padded)
    pstart = pend - padded
    tm = top_e.shape[1]
    te = top_e.reshape(-1, SUB, tm)[:, :TOP_K]
    rk = rank.reshape(-1, SUB, tm)[:, :TOP_K]
    start_of = jnp.zeros_like(te)
    for e in range(N_EXPERTS):
        start_of = jnp.where(te == e, pstart[e], start_of)
    dest = jnp.transpose(start_of + rk, (0, 2, 1)).astype(jnp.int32).reshape(-1)
    first_row = jnp.arange(n_blocks, dtype=jnp.int32) * bm
    block_e = jnp.minimum(jnp.sum((pend[None, :] <= first_row[:, None]).astype(jnp.int32), axis=1),
                          N_EXPERTS - 1).astype(jnp.int32)
    n_valid = (pend[-1] // bm).astype(jnp.int32).reshape(1)
    pad_lo = (pstart + counts).astype(jnp.int32)
    return dest, block_e, n_valid, pad_lo, pend.astype(jnp.int32)


def kernel(x, c, ctx, c_ctx, norm1_g, norm2_g, w_ada, b_ada, w_in, gla_wa2, gla_ba, gla_norm_g, na_rpb,
           pool_w, pool_scale, w_out, router_w, router_b, w_gu, b_gu, w_down, b_down, final_g):
    B, S, D = x.shape
    L = ctx.shape[1]
    assert L == TOKEN_TILE and S % TOKEN_TILE == 0 and S % GRID_W == 0
    T = L + S
    tpb = T // TOKEN_TILE
    N = B * T
    A = N * TOP_K
    n_blocks = (A + MOE_ROWS - 1) // MOE_ROWS + N_EXPERTS

    xa = jnp.concatenate([ctx, x], axis=1).reshape(N, D)
    cond_rows = 8 * ((B + 1 + 7) // 8)
    cond = jnp.zeros((cond_rows, D), F32).at[:B].set(c).at[B].set(c_ctx)
    cos_t, sin_t = _rope_tables(S, L)

    def layer(xa, p):
        (n1, n2, wada, bada, win, wa2, ba, gng, rpb, pw, psc, wout, rw, rb, wgu, bgu, wdn, bdn) = p
        mod = _modulation(cond, wada, bada).reshape(cond_rows, 6, D)
        wa, bap = _layout_gate(wa2, ba)
        q, k, v, g, bf, bb, nq, nk, nv, pu = _input_projection(xa, mod, n1, _layout_w_in(win), cos_t, sin_t,
                                                               wa, bap, B, tpb)
        gla_o = _gla(q, k, v, g, bf, bb, gng, B, T, L)
        na_o = _neighbourhood_attention(nq, nk, nv, _na_bias_table(rpb, S // GRID_W), B, T, L)
        pool_o = _pool(pu, _block_diag(pw), psc, B, T, L)
        rwp = jnp.zeros((D, LANE), F32).at[:, :N_EXPERTS].set(rw)
        rbp = jnp.full((1, LANE), NEG, F32).at[0, :N_EXPERTS].set(rb)
        x1, h2, top_e, top_g, rank, counts = _output_projection(gla_o, na_o, pool_o, xa, mod, n2,
                                                                wout.astype(BF16), rwp, rbp, B, tpb)
        dest, block_e, n_valid, pad_lo, pad_hi = _routing(top_e, rank, counts, n_blocks)
        xs = _dispatch(pad_lo, pad_hi, n_valid, dest, h2, n_blocks * MOE_ROWS)
        yb = _moe_blocks(block_e, n_valid, xs, wgu, bgu, wdn, bdn)
        gates = jnp.transpose(top_g.reshape(-1, SUB, TOKEN_TILE), (0, 2, 1)).reshape(N, SUB)
        return _combine(dest, x1, gates, yb, mod, B, tpb), None

    params = (norm1_g, norm2_g, w_ada, b_ada, w_in, gla_wa2, gla_ba, gla_norm_g, na_rpb, pool_w, pool_scale,
              w_out, router_w, router_b, w_gu, b_gu, w_down, b_down)
    xa, _ = lax.scan(layer, xa, params)
    return _final_norm(xa, final_g, B, tpb, S)
```

```python
import functools

import numpy as np
import jax
import jax.numpy as jnp
from jax import lax
from jax.experimental import pallas as pl
from jax.experimental.pallas import tpu as pltpu

F32 = jnp.float32
BF16 = jnp.bfloat16

EPS = 1e-6
ROPE_THETA = 10000.0
GRID_W = 64

GLA_HEADS = 4
GLA_DK = 48
GLA_DV = 96
GLA_QK = GLA_HEADS * GLA_DK
GLA_V = GLA_HEADS * GLA_DV
GLA_RANK = 16
GLA_TAU = 16.0
GLA_CHUNK = 64

NA_HEADS = 6
NA_DH = 64
NA_W = NA_HEADS * NA_DH
NA_KH = 8
NA_KW = 16

POOL_WINDOWS = (2, 4, 8, 16)
POOL_WIDTH = 256
POOL_GROUP = 64
POOL_HALO = 8

N_EXPERTS = 32
TOP_K = 4
SWIGLU_LIMIT = 7.0
SWIGLU_ALPHA = 1.702

LANE = 128
TOKEN_TILE = 256
MOE_ROWS = 512
NEG = -1e30
VMEM_LIMIT = 56 * 1024 * 1024

QK_PAD = 256
C_QK = 0
C_SW = C_QK + 2 * QK_PAD
C_V = C_SW + 2 * QK_PAD
C_G = C_V + GLA_V
C_A = C_G + GLA_V
C_NQ = C_A + LANE
C_NK = C_NQ + NA_W
C_NV = C_NK + NA_W
C_PU = C_NV + NA_W
C_END = C_PU + POOL_WIDTH


def _dot(a, b):
    return jnp.dot(a, b, preferred_element_type=F32)


def _dot_nt(a, b):
    return lax.dot_general(a, b, (((1,), (1,)), ((), ())), preferred_element_type=F32)


def _split(x):
    hi = x.astype(BF16)
    lo = (x - hi.astype(F32)).astype(BF16)
    return hi, lo


def _sigmoid(x):
    return 1.0 / (1.0 + jnp.exp(-x))


def _params(sem):
    return pltpu.CompilerParams(dimension_semantics=sem, vmem_limit_bytes=VMEM_LIMIT)


def _mod_kernel(c_ref, w_ref, b_ref, o_ref):
    c = c_ref[...]
    s = c * _sigmoid(c)
    sh, sl = _split(s)
    wh, wl = _split(w_ref[...])
    o_ref[...] = _dot(sh, wh) + _dot(sl, wh) + _dot(sh, wl) + b_ref[...]


def _modulation(cond, w_ada, b_ada):
    R, D = cond.shape
    N = w_ada.shape[1]
    bn = 1024
    return pl.pallas_call(
        _mod_kernel,
        grid=(N // bn,),
        in_specs=[pl.BlockSpec((R, D), lambda j: (0, 0)),
                  pl.BlockSpec((D, bn), lambda j: (0, j)),
                  pl.BlockSpec((1, bn), lambda j: (0, j))],
        out_specs=pl.BlockSpec((R, bn), lambda j: (0, j)),
        out_shape=jax.ShapeDtypeStruct((R, N), F32),
        compiler_params=_params(("arbitrary",)),
        name="adaln_mod",
    )(cond, w_ada, b_ada.reshape(1, N))


def _inproj_kernel(x_ref, mod_ref, g_ref, w_ref, cos_ref, sin_ref, wa_ref, ba_ref, tri_ref,
                   q_ref, k_ref, v_ref, gg_ref, bf_ref, bb_ref, nq_ref, nk_ref, nv_ref, pu_ref):
    x = x_ref[...]
    ms = jnp.mean(x * x, axis=-1, keepdims=True)
    h = x * lax.rsqrt(ms + EPS) * g_ref[...]
    h = h * (1.0 + mod_ref[1:2, :]) + mod_ref[0:1, :]
    hb = h.astype(BF16)

    def proj(lo, hi):
        return _dot(hb, w_ref[:, lo:hi])

    rot = proj(C_QK, C_SW) * cos_ref[...] + proj(C_SW, C_V) * sin_ref[...]
    q_ref[...] = rot[:, 0:GLA_QK].astype(BF16)
    k_ref[...] = rot[:, QK_PAD:QK_PAD + GLA_QK].astype(BF16)
    v_ref[...] = proj(C_V, C_G).astype(BF16)
    gg_ref[...] = proj(C_G, C_A).astype(BF16)
    a = proj(C_A, C_NQ).astype(BF16)
    for d, out_ref in enumerate((bf_ref, bb_ref)):
        z = _dot(a, wa_ref[d]) + ba_ref[d]
        la = (jnp.minimum(z, 0.0) - jnp.log(1.0 + jnp.exp(-jnp.abs(z)))) * (1.0 / GLA_TAU)
        lh, ll = _split(la)
        out_ref[...] = _dot(tri_ref[d], lh) + _dot(tri_ref[d], ll)
    nq_ref[...] = (proj(C_NQ, C_NK) * (NA_DH ** -0.5)).astype(BF16)
    nk_ref[...] = proj(C_NK, C_NV).astype(BF16)
    nv_ref[...] = proj(C_NV, C_PU).astype(BF16)
    pu_ref[...] = proj(C_PU, C_END)


def _chunk_triangles():
    r = np.arange(TOKEN_TILE)[:, None]
    c = np.arange(TOKEN_TILE)[None, :]
    same = (r // GLA_CHUNK) == (c // GLA_CHUNK)
    return jnp.asarray(np.stack([same & (c <= r), same & (c >= r)]).astype(np.float32), dtype=BF16)


def _input_projection(x, mod, norm_g, w, cos_t, sin_t, wa, ba, nb, tpb):
    N, D = x.shape
    tm = TOKEN_TILE

    def mod_idx(i):
        return (jnp.where(i % tpb == 0, nb, i // tpb), 0, 0)

    tok = lambda w_: pl.BlockSpec((tm, w_), lambda i: (i, 0))
    outs = [(GLA_QK, BF16), (GLA_QK, BF16), (GLA_V, BF16), (GLA_V, BF16), (GLA_QK, F32), (GLA_QK, F32),
            (NA_W, BF16), (NA_W, BF16), (NA_W, BF16), (POOL_WIDTH, F32)]
    return pl.pallas_call(
        _inproj_kernel,
        grid=(N // tm,),
        in_specs=[tok(D),
                  pl.BlockSpec((None, 6, D), mod_idx),
                  pl.BlockSpec((1, D), lambda i: (0, 0)),
                  pl.BlockSpec((D, C_END), lambda i: (0, 0)),
                  pl.BlockSpec((tm, 2 * QK_PAD), lambda i: (i % tpb, 0)),
                  pl.BlockSpec((tm, 2 * QK_PAD), lambda i: (i % tpb, 0)),
                  pl.BlockSpec((2, LANE, GLA_QK), lambda i: (0, 0, 0)),
                  pl.BlockSpec((2, 1, GLA_QK), lambda i: (0, 0, 0)),
                  pl.BlockSpec((2, tm, tm), lambda i: (0, 0, 0))],
        out_specs=[tok(w_) for w_, _ in outs],
        out_shape=[jax.ShapeDtypeStruct((N, w_), dt) for w_, dt in outs],
        compiler_params=_params(("arbitrary",)),
        name="in_proj",
    )(x, mod, norm_g.reshape(1, D), w, cos_t, sin_t, wa, ba, _chunk_triangles())


def _lane_head(idx, width):
    h = jnp.zeros_like(idx)
    for t in range(1, GLA_HEADS):
        h = h + (idx >= t * width).astype(jnp.int32)
    return h


def _gla_kernel(q_ref, k_ref, v_ref, g_ref, bf_ref, bb_ref, ng_ref, o_ref,
                stf_ref, stb_ref, of_ref, ob_ref, *, n_ctx, n_all):
    assert n_all % 2 == 0
    C = GLA_CHUNK
    i32 = jnp.int32
    kmask = (lax.broadcasted_iota(i32, (GLA_HEADS * C, GLA_QK), 0) // C
             == _lane_head(lax.broadcasted_iota(i32, (GLA_HEADS * C, GLA_QK), 1), GLA_DK))
    vmask = (lax.broadcasted_iota(i32, (GLA_HEADS * C, GLA_V), 0) // C
             == _lane_head(lax.broadcasted_iota(i32, (GLA_HEADS * C, GLA_V), 1), GLA_DV))
    smask = (_lane_head(lax.broadcasted_iota(i32, (GLA_V, GLA_QK), 0), GLA_DV)
             == _lane_head(lax.broadcasted_iota(i32, (GLA_V, GLA_QK), 1), GLA_DK))
    hind = jnp.where(_lane_head(lax.broadcasted_iota(i32, (GLA_V, GLA_V), 0), GLA_DV)
                     == _lane_head(lax.broadcasted_iota(i32, (GLA_V, GLA_V), 1), GLA_DV),
                     1.0, 0.0).astype(BF16)
    qi_ = lax.broadcasted_iota(i32, (C, GLA_HEADS * C), 0)
    kj_ = lax.broadcasted_iota(i32, (C, GLA_HEADS * C), 1) % C
    causal_f = qi_ >= kj_
    causal_b = qi_ <= kj_

    def chunk(ci, fwd):
        st_ref = stf_ref if fwd else stb_ref
        r0 = pl.multiple_of(ci * C, C)
        rows = pl.ds(r0, C)
        q = q_ref[rows, :].astype(F32)
        k = k_ref[rows, :].astype(F32)
        v = v_ref[rows, :]
        b = (bf_ref if fwd else bb_ref)[rows, :]
        if fwd:
            bmid = b[C // 2 - 1:C // 2, :]
            btot = b[C - 1:C, :]
        else:
            bmid = b[C // 2:C // 2 + 1, :]
            btot = b[0:1, :]
        qe = (q * jnp.exp(b - bmid)).astype(BF16)
        ke = (k * jnp.exp(bmid - b)).astype(BF16)
        qs = (q * jnp.exp(b)).astype(BF16)
        kt = (k * jnp.exp(btot - b)).astype(BF16)
        zero = jnp.zeros((), BF16)
        kbd = jnp.where(kmask, jnp.concatenate([ke] * GLA_HEADS, axis=0), zero)
        vbd = jnp.where(vmask, jnp.concatenate([v] * GLA_HEADS, axis=0), zero)
        sc = _dot_nt(qe, kbd)
        sc = jnp.where(causal_f if fwd else causal_b, sc, 0.0)
        st = st_ref[...]
        o = _dot(sc.astype(BF16), vbd) + _dot_nt(qs, st.astype(BF16))
        vt = v.astype(F32).T.astype(BF16)
        ds = _dot(vt, kt)
        st_ref[...] = st * jnp.exp(btot) + jnp.where(smask, ds, 0.0)
        (of_ref if fwd else ob_ref)[rows, :] = o

    def finish(ci):
        rows = pl.ds(pl.multiple_of(ci * C, C), C)
        tot = of_ref[rows, :] + ob_ref[rows, :]
        th, tl = _split(tot * tot)
        ms = (_dot(th, hind) + _dot(tl, hind)) * (1.0 / GLA_DV)
        g = g_ref[rows, :].astype(F32)
        y = tot * lax.rsqrt(ms + EPS) * ng_ref[...] * (g * _sigmoid(g))
        o_ref[rows, :] = y.astype(o_ref.dtype)

    def both(f0, b0, count):
        def body(i, carry):
            chunk(f0 + i, True)
            chunk(b0 - i, False)
            return carry
        lax.fori_loop(0, count, body, 0)

    stf_ref[...] = jnp.zeros_like(stf_ref)
    stb_ref[...] = jnp.zeros_like(stb_ref)
    both(0, n_ctx - 1, n_ctx)
    both(n_ctx, n_all - 1, n_all - n_ctx)

    def fin_body(i, carry):
        finish(2 * i)
        finish(2 * i + 1)
        return carry
    lax.fori_loop(0, n_all // 2, fin_body, 0)


def _gla(q, k, v, g, bf, bb, norm_g, nb, T, L):
    seq = lambda w_: pl.BlockSpec((T, w_), lambda b: (b, 0))
    once = lambda w_: pl.BlockSpec((T, w_), lambda b: (b, 0), pipeline_mode=pl.Buffered(1))
    kern = functools.partial(_gla_kernel, n_ctx=L // GLA_CHUNK, n_all=T // GLA_CHUNK)
    return pl.pallas_call(
        kern,
        grid=(nb,),
        in_specs=[seq(GLA_QK), seq(GLA_QK), seq(GLA_V), seq(GLA_V), once(GLA_QK), once(GLA_QK),
                  pl.BlockSpec((1, GLA_V), lambda b: (0, 0))],
        out_specs=seq(GLA_V),
        out_shape=jax.ShapeDtypeStruct((nb * T, GLA_V), BF16),
        scratch_shapes=[pltpu.VMEM((GLA_V, GLA_QK), F32), pltpu.VMEM((GLA_V, GLA_QK), F32),
                        pltpu.VMEM((T, GLA_V), F32), pltpu.VMEM((T, GLA_V), F32)],
        compiler_params=_params(("arbitrary",)),
        name="gla",
    )(q, k, v, g, bf, bb, norm_g.reshape(1, GLA_V))


NA_GROUP = TOKEN_TILE // GRID_W
NA_WIN = NA_KH + NA_GROUP - 1


def _na_window_start(group, n_rows):
    return jnp.clip(group * NA_GROUP - NA_KH // 2, 0, n_rows - NA_WIN)


def _na_kernel(q_ref, k_ref, v_ref, bias_ref, o_ref, *, n_rows, L):
    j = pl.program_id(1)
    W = GRID_W
    Q = q_ref.shape[0]
    lane = lax.broadcasted_iota(jnp.int32, (2 * Q, LANE), 1)
    row = lax.broadcasted_iota(jnp.int32, (2 * Q, LANE), 0)
    qmask = (row < Q) == (lane < NA_DH)
    out_lo = lax.broadcasted_iota(jnp.int32, (Q, LANE), 1) < NA_DH
    zero = jnp.zeros((), BF16)

    def pair_q(p):
        q2 = q_ref[:, p * LANE:(p + 1) * LANE]
        return jnp.where(qmask, jnp.concatenate([q2, q2], axis=0), zero)

    def finish(p, acc, l):
        o2 = acc / l
        o_ref[:, p * LANE:(p + 1) * LANE] = jnp.where(out_lo, o2[:Q], o2[Q:]).astype(o_ref.dtype)

    @pl.when(j == 0)
    def _():
        for p in range(NA_HEADS // 2):
            cols = slice(p * LANE, (p + 1) * LANE)
            s = _dot_nt(pair_q(p), k_ref[0:L, cols])
            m = jnp.max(s, axis=-1, keepdims=True)
            e = jnp.exp(s - m)
            l = jnp.sum(e, axis=-1, keepdims=True)
            finish(p, _dot(e.astype(BF16), v_ref[0:L, cols]), l)

    @pl.when(j > 0)
    def _():
        k0 = pl.multiple_of(L + _na_window_start(j - 1, n_rows) * W, W)
        win = pl.ds(k0, NA_WIN * W)
        for p in range(NA_HEADS // 2):
            cols = slice(p * LANE, (p + 1) * LANE)
            qb = pair_q(p)
            s_w = _dot_nt(qb, k_ref[win, cols]) + bias_ref[p]
            s_c = _dot_nt(qb, k_ref[0:L, cols])
            m = jnp.maximum(jnp.max(s_w, axis=-1, keepdims=True), jnp.max(s_c, axis=-1, keepdims=True))
            e_w = jnp.exp(s_w - m)
            e_c = jnp.exp(s_c - m)
            l = jnp.sum(e_w, axis=-1, keepdims=True) + jnp.sum(e_c, axis=-1, keepdims=True)
            acc = _dot(e_w.astype(BF16), v_ref[win, cols]) + _dot(e_c.astype(BF16), v_ref[0:L, cols])
            finish(p, acc, l)


def _na_bias_table(rpb, n_rows):
    W, G = GRID_W, NA_GROUP
    n_groups = n_rows // G
    assert n_rows >= NA_WIN and n_groups >= 3
    qc = np.arange(W)[:, None]
    kc = np.arange(W)[None, :]
    cs = np.clip(qc - NA_KW // 2, 0, W - NA_KW)
    cvalid = (kc >= cs) & (kc < cs + NA_KW)
    cidx = np.clip(kc - qc + NA_KW - 1, 0, 2 * NA_KW - 2)
    rvalid, ridx = [], []
    for grp in (0, 1, n_groups - 1):
        ws = int(np.clip(grp * G - NA_KH // 2, 0, n_rows - NA_WIN))
        r = (grp * G + np.arange(G))[:, None]
        rs = np.clip(r - NA_KH // 2, 0, n_rows - NA_KH)
        key_row = (ws + np.arange(NA_WIN))[None, :]
        rvalid.append((key_row >= rs) & (key_row < rs + NA_KH))
        ridx.append(np.clip(key_row - r + NA_KH - 1, 0, 2 * NA_KH - 2))
    e_r = jnp.asarray(np.eye(2 * NA_KH - 1, dtype=np.float32)[np.stack(ridx)])
    e_c = jnp.asarray(np.eye(2 * NA_KW - 1, dtype=np.float32)[cidx])
    t = jnp.einsum('vria,hab,qkb->vhrqik', e_r, rpb.astype(F32), e_c, precision=lax.Precision.HIGHEST)
    valid = np.stack(rvalid)[:, None, :, None, :, None] & cvalid[None, None, None, :, None, :]
    t = jnp.where(jnp.asarray(valid), t, NEG)
    return t.reshape(3, NA_HEADS // 2, 2 * G * W, NA_WIN * W)


def _neighbourhood_attention(nq, nk, nv, bias, nb, T, L):
    W = GRID_W
    n_rows = (T - L) // W
    n_groups = n_rows // NA_GROUP
    Q = NA_GROUP * W
    spb = T // Q

    def bias_idx(b, j):
        return (jnp.where(j <= 1, 0, jnp.where(j == n_groups, 2, 1)), 0, 0, 0)

    kern = functools.partial(_na_kernel, n_rows=n_rows, L=L)
    return pl.pallas_call(
        kern,
        grid=(nb, spb),
        in_specs=[pl.BlockSpec((Q, NA_W), lambda b, j: (b * spb + j, 0)),
                  pl.BlockSpec((T, NA_W), lambda b, j: (b, 0)),
                  pl.BlockSpec((T, NA_W), lambda b, j: (b, 0)),
                  pl.BlockSpec((None, NA_HEADS // 2, 2 * Q, NA_WIN * W), bias_idx)],
        out_specs=pl.BlockSpec((Q, NA_W), lambda b, j: (b * spb + j, 0)),
        out_shape=jax.ShapeDtypeStruct((nb * T, NA_W), BF16),
        compiler_params=_params(("arbitrary", "arbitrary")),
        name="neigh_attn",
    )(nq, nk, nv, bias)


def _pool_kernel(u_ref, w_ref, sc_ref, o_ref, *, T, L):
    i = pl.program_id(1)
    tm = TOKEN_TILE
    ext = tm + 2 * POOL_HALO
    t0 = i * tm
    e0 = pl.multiple_of(jnp.clip(t0 - POOL_HALO, 0, T - ext), 8)
    seg_lo = jnp.where(t0 < L, 0, L)
    seg_hi = jnp.where(t0 < L, L, T)
    uh, ul = _split(u_ref[pl.ds(e0, ext), :])
    t = t0 + lax.broadcasted_iota(jnp.int32, (tm, ext), 0)
    c = e0 + lax.broadcasted_iota(jnp.int32, (tm, ext), 1)
    tcol = t0 + lax.broadcasted_iota(jnp.int32, (tm, 1), 0)
    lane = lax.broadcasted_iota(jnp.int32, (tm, POOL_WIDTH), 1)
    y = jnp.zeros((tm, POOL_WIDTH), F32)
    for gi, w in enumerate(POOL_WINDOWS):
        lo = jnp.maximum(t - w // 2, seg_lo)
        hi = jnp.minimum(t + w - w // 2, seg_hi)
        band = jnp.where((c >= lo) & (c < hi), 1.0, 0.0).astype(BF16)
        cnt = (jnp.minimum(tcol + w - w // 2, seg_hi) - jnp.maximum(tcol - w // 2, seg_lo)).astype(F32)
        s = (_dot(band, uh) + _dot(band, ul)) / cnt
        y = jnp.where((lane >= gi * POOL_GROUP) & (lane < (gi + 1) * POOL_GROUP), s, y)
    y = y - u_ref[pl.ds(pl.multiple_of(t0, tm), tm), :]
    o_ref[...] = (_dot(y.astype(BF16), w_ref[...]) * sc_ref[...]).astype(o_ref.dtype)


def _pool(u, w_bd, scale, nb, T, L):
    tpb = T // TOKEN_TILE
    kern = functools.partial(_pool_kernel, T=T, L=L)
    return pl.pallas_call(
        kern,
        grid=(nb, tpb),
        in_specs=[pl.BlockSpec((T, POOL_WIDTH), lambda b, i: (b, 0)),
                  pl.BlockSpec((POOL_WIDTH, POOL_WIDTH), lambda b, i: (0, 0)),
                  pl.BlockSpec((1, POOL_WIDTH), lambda b, i: (0, 0))],
        out_specs=pl.BlockSpec((TOKEN_TILE, POOL_WIDTH), lambda b, i: (b * tpb + i, 0)),
        out_shape=jax.ShapeDtypeStruct((nb * T, POOL_WIDTH), BF16),
        compiler_params=_params(("arbitrary", "arbitrary")),
        name="pool_mix",
    )(u, w_bd, scale.reshape(1, POOL_WIDTH))


def _outproj_kernel(gla_ref, na_ref, pm_ref, x_ref, mod_ref, g_ref, wo_ref, rw_ref, rb_ref,
                    x1_ref, h2_ref, te_ref, tg_ref, rk_ref, cnt_ref, carry_ref):
    @pl.when(pl.program_id(0) == 0)
    def _():
        carry_ref[...] = jnp.zeros_like(carry_ref)

    mx = (_dot(gla_ref[...], wo_ref[0:GLA_V, :])
          + _dot(na_ref[...], wo_ref[GLA_V:GLA_V + NA_W, :])
          + _dot(pm_ref[...], wo_ref[GLA_V + NA_W:, :]))
    x1 = x_ref[...] + mod_ref[2:3, :] * mx
    x1_ref[...] = x1
    ms = jnp.mean(x1 * x1, axis=-1, keepdims=True)
    h = x1 * lax.rsqrt(ms + EPS) * g_ref[...]
    h = h * (1.0 + mod_ref[4:5, :]) + mod_ref[3:4, :]
    h2_ref[...] = h
    hh, hl = _split(h)
    wh, wl = _split(rw_ref[...])
    logits = _dot(hh, wh) + _dot(hl, wh) + _dot(hh, wl) + rb_ref[...]
    tm = logits.shape[0]
    E = N_EXPERTS
    lt = logits.T[0:E, :]
    erow = lax.broadcasted_iota(jnp.int32, (E, tm), 0)
    cur = lt
    vals, idxs = [], []
    for kk in range(TOP_K):
        m = jnp.max(cur, axis=0, keepdims=True)
        idx = jnp.min(jnp.where(cur == m, erow, E), axis=0, keepdims=True)
        vals.append(m)
        idxs.append(idx)
        cur = jnp.where(erow == idx, NEG, cur)
    expv = [jnp.exp(v - vals[0]) for v in vals]
    den = sum(expv)
    hot = sum(jnp.where(erow == idx, 1.0, 0.0) for idx in idxs)
    before = (lax.broadcasted_iota(jnp.int32, (tm, tm), 0) < lax.broadcasted_iota(jnp.int32, (tm, tm), 1))
    base = _dot(hot.astype(BF16), jnp.where(before, 1.0, 0.0).astype(BF16)) + carry_ref[...]
    krow = lax.broadcasted_iota(jnp.int32, (SUB, tm), 0)
    te = jnp.zeros((SUB, tm), jnp.int32)
    rk = jnp.zeros((SUB, tm), jnp.int32)
    tv = jnp.zeros((SUB, tm), F32)
    for kk in range(TOP_K):
        r = jnp.sum(jnp.where(erow == idxs[kk], base, 0.0), axis=0, keepdims=True)
        te = jnp.where(krow == kk, idxs[kk], te)
        rk = jnp.where(krow == kk, r.astype(jnp.int32), rk)
        tv = jnp.where(krow == kk, expv[kk] / den, tv)
    te_ref[...] = te
    rk_ref[...] = rk
    tg_ref[...] = tv
    carry_ref[...] = carry_ref[...] + jnp.sum(hot, axis=1, keepdims=True)
    cnt_ref[...] = jnp.broadcast_to(carry_ref[...], cnt_ref.shape).astype(jnp.int32)


def _output_projection(gla, na, pm, x, mod, norm_g, w_out, rw, rb, nb, tpb):
    N, D = x.shape
    tm = TOKEN_TILE

    def mod_idx(i):
        return (jnp.where(i % tpb == 0, nb, i // tpb), 0, 0)

    tok = lambda w_: pl.BlockSpec((tm, w_), lambda i: (i, 0))
    n_tiles = N // tm
    route = pl.BlockSpec((SUB, tm), lambda i: (i, 0))
    return pl.pallas_call(
        _outproj_kernel,
        grid=(n_tiles,),
        in_specs=[tok(GLA_V), tok(NA_W), tok(POOL_WIDTH), tok(D),
                  pl.BlockSpec((None, 6, D), mod_idx),
                  pl.BlockSpec((1, D), lambda i: (0, 0)),
                  pl.BlockSpec((D, D), lambda i: (0, 0)),
                  pl.BlockSpec((D, LANE), lambda i: (0, 0)),
                  pl.BlockSpec((1, LANE), lambda i: (0, 0))],
        out_specs=[tok(D), tok(D), route, route, route,
                   pl.BlockSpec((N_EXPERTS, LANE), lambda i: (0, 0))],
        out_shape=[jax.ShapeDtypeStruct((N, D), F32), jax.ShapeDtypeStruct((N, D), F32),
                   jax.ShapeDtypeStruct((n_tiles * SUB, tm), jnp.int32),
                   jax.ShapeDtypeStruct((n_tiles * SUB, tm), F32),
                   jax.ShapeDtypeStruct((n_tiles * SUB, tm), jnp.int32),
                   jax.ShapeDtypeStruct((N_EXPERTS, LANE), jnp.int32)],
        scratch_shapes=[pltpu.VMEM((N_EXPERTS, 1), F32)],
        compiler_params=_params(("arbitrary",)),
        name="out_proj_router",
    )(gla, na, pm, x, mod, norm_g.reshape(1, D), w_out, rw, rb)


SUB = 8


def _dispatch_kernel(plo_ref, phi_ref, nv_ref, dest_ref, h_ref, xs_ref, zrow, zblk, sem, zsem, *, n_tiles):
    tm = h_ref.shape[0]
    for t in range(tm):
        for kk in range(TOP_K):
            d = dest_ref[t * TOP_K + kk]
            pltpu.make_async_copy(h_ref.at[pl.ds(t, 1)], xs_ref.at[pl.ds(d, 1)], sem).start(priority=kk % 2)
    for kk in range(TOP_K):
        pltpu.make_async_copy(h_ref, xs_ref.at[pl.ds(0, tm)], sem).wait()

    @pl.when(pl.program_id(0) == n_tiles - 1)
    def _():
        zrow[...] = jnp.zeros_like(zrow)

        def pad_copy(r):
            return pltpu.make_async_copy(zrow, xs_ref.at[pl.ds(r, 1)], zsem)

        zblk[...] = jnp.zeros_like(zblk)

        def pad_copy8(r8):
            return pltpu.make_async_copy(zblk.at[pl.ds(0, SUB)], xs_ref.at[pl.ds(pl.multiple_of(r8, SUB), SUB)], zsem)

        def per_expert(e, carry):
            lo = plo_ref[e]
            hi = phi_ref[e]
            mid = jnp.minimum(lax.shift_left(lax.shift_right_logical(lo + (SUB - 1), 3), 3), hi)
            pieces = lax.shift_right_logical(hi - mid, 3)

            def start(r, c):
                pad_copy(r).start()
                return c

            def wait(r, c):
                pad_copy(r).wait()
                return c

            def start8(j, c):
                pad_copy8(mid + j * SUB).start()
                return c

            def wait8(j, c):
                pad_copy8(mid + j * SUB).wait()
                return c

            lax.fori_loop(lo, mid, start, 0)
            lax.fori_loop(0, pieces, start8, 0)
            lax.fori_loop(lo, mid, wait, 0)
            lax.fori_loop(0, pieces, wait8, 0)
            return carry

        lax.fori_loop(0, N_EXPERTS, per_expert, 0)

        bg = zblk.shape[0]

        def blk_copy(b):
            return pltpu.make_async_copy(zblk, xs_ref.at[pl.ds(pl.multiple_of(b * bg, SUB), bg)], zsem)

        def blk_start(b, c):
            blk_copy(b).start()
            return c

        def blk_wait(b, c):
            blk_copy(b).wait()
            return c

        n_blocks = xs_ref.shape[0] // bg
        lax.fori_loop(nv_ref[0], n_blocks, blk_start, 0)
        lax.fori_loop(nv_ref[0], n_blocks, blk_wait, 0)


def _dispatch(pad_lo, pad_hi, n_valid, dest_flat, h2, n_slots):
    N, D = h2.shape
    tm = TOKEN_TILE
    n_tiles = N // tm
    grid_spec = pltpu.PrefetchScalarGridSpec(
        num_scalar_prefetch=3,
        grid=(n_tiles,),
        in_specs=[pl.BlockSpec((tm * TOP_K,), lambda i, lo, hi, nv: (i,), memory_space=pltpu.SMEM),
                  pl.BlockSpec((tm, D), lambda i, lo, hi, nv: (i, 0))],
        out_specs=pl.BlockSpec(memory_space=pl.ANY),
        scratch_shapes=[pltpu.VMEM((1, D), F32), pltpu.VMEM((MOE_ROWS, D), F32),
                        pltpu.SemaphoreType.DMA(()), pltpu.SemaphoreType.DMA(())],
    )
    return pl.pallas_call(
        functools.partial(_dispatch_kernel, n_tiles=n_tiles),
        grid_spec=grid_spec,
        out_shape=jax.ShapeDtypeStruct((n_slots, D), F32),
        compiler_params=_params(("arbitrary",)),
        name="moe_dispatch",
    )(pad_lo, pad_hi, n_valid, dest_flat, h2)


def _moe_kernel(li_ref, be_ref, nv_ref, xs_ref, wgu_ref, bgu_ref, wd_ref, bd_ref, y_ref, wgu_bf, wd_bf):
    del li_ref
    i = pl.program_id(0)
    valid = i < nv_ref[0]
    new_expert = jnp.logical_or(i == 0, be_ref[i] != be_ref[jnp.maximum(i - 1, 0)])

    @pl.when(jnp.logical_and(valid, new_expert))
    def _():
        wgu_bf[...] = wgu_ref[...].astype(BF16)
        wd_bf[...] = wd_ref[...].astype(BF16)

    @pl.when(valid)
    def _():
        dff = wd_ref.shape[0]
        gu = _dot(xs_ref[...].astype(BF16), wgu_bf[...]) + bgu_ref[...]
        g = jnp.minimum(gu[:, :dff], SWIGLU_LIMIT)
        u = jnp.clip(gu[:, dff:], -SWIGLU_LIMIT, SWIGLU_LIMIT)
        act = g * _sigmoid(SWIGLU_ALPHA * g) * (u + 1.0)
        y_ref[...] = _dot(act.astype(BF16), wd_bf[...]) + bd_ref[...]

    @pl.when(i >= nv_ref[0])
    def _():
        y_ref[...] = jnp.zeros_like(y_ref)


def _moe_blocks(layer_idx, block_e, n_valid, xs, w_gu, b_gu, w_down, b_down):
    P, D = xs.shape
    depth, E, _, F2 = w_gu.shape
    bm = MOE_ROWS
    expert = lambda i, li, be, nv: (li[0], be[i], 0, 0)
    grid_spec = pltpu.PrefetchScalarGridSpec(
        num_scalar_prefetch=3,
        grid=(P // bm,),
        in_specs=[pl.BlockSpec((bm, D), lambda i, li, be, nv: (i, 0)),
                  pl.BlockSpec((None, None, D, F2), expert),
                  pl.BlockSpec((None, None, 1, F2), expert),
                  pl.BlockSpec((None, None, F2 // 2, D), expert),
                  pl.BlockSpec((None, None, 1, D), expert)],
        out_specs=pl.BlockSpec((bm, D), lambda i, li, be, nv: (i, 0)),
        scratch_shapes=[pltpu.VMEM((D, F2), BF16), pltpu.VMEM((F2 // 2, D), BF16)],
    )
    return pl.pallas_call(
        _moe_kernel,
        grid_spec=grid_spec,
        out_shape=jax.ShapeDtypeStruct((P, D), F32),
        compiler_params=_params(("arbitrary",)),
        name="moe_experts",
    )(layer_idx, block_e, n_valid, xs, w_gu, b_gu.reshape(depth, E, 1, F2), w_down, b_down.reshape(depth, E, 1, D))


def _combine_kernel(dcur_ref, dnxt_ref, x_ref, tg_ref, mod_ref, yb_ref, o_ref, ybuf, sem, *, n_tiles):
    i = pl.program_id(0)
    tm = x_ref.shape[0]
    slot = i % 2

    def issue(dref, s):
        for t in range(tm):
            for kk in range(TOP_K):
                pltpu.make_async_copy(yb_ref.at[pl.ds(dref[t * TOP_K + kk], 1)], ybuf.at[s, kk, pl.ds(t, 1)],
                                      sem.at[s]).start(priority=kk % 2)

    @pl.when(i == 0)
    def _():
        issue(dcur_ref, 0)

    @pl.when(i + 1 < n_tiles)
    def _():
        issue(dnxt_ref, 1 - slot)

    for kk in range(TOP_K):
        pltpu.make_async_copy(yb_ref.at[pl.ds(0, tm)], ybuf.at[slot, kk], sem.at[slot]).wait()
    tg = tg_ref[...]
    rows = lambda kk: ybuf[slot, kk]
    y = tg[:, 0:1] * rows(0)
    for kk in range(1, TOP_K):
        y = y + tg[:, kk:kk + 1] * rows(kk)
    o_ref[...] = x_ref[...] + mod_ref[5:6, :] * y


def _combine(dest_flat, x1, top_g, yb, mod, nb, tpb):
    N, D = x1.shape
    tm = TOKEN_TILE
    n_tiles = N // tm

    def mod_idx(i):
        return (jnp.where(i % tpb == 0, nb, i // tpb), 0, 0)

    kern = functools.partial(_combine_kernel, n_tiles=n_tiles)
    return pl.pallas_call(
        kern,
        grid=(n_tiles,),
        in_specs=[pl.BlockSpec((tm * TOP_K,), lambda i: (i,), memory_space=pltpu.SMEM),
                  pl.BlockSpec((tm * TOP_K,), lambda i: (jnp.minimum(i + 1, n_tiles - 1),),
                               memory_space=pltpu.SMEM),
                  pl.BlockSpec((tm, D), lambda i: (i, 0)),
                  pl.BlockSpec((tm, SUB), lambda i: (i, 0)),
                  pl.BlockSpec((None, 6, D), mod_idx),
                  pl.BlockSpec(memory_space=pl.ANY)],
        out_specs=pl.BlockSpec((tm, D), lambda i: (i, 0)),
        out_shape=jax.ShapeDtypeStruct((N, D), F32),
        scratch_shapes=[pltpu.VMEM((2, TOP_K, tm, D), F32), pltpu.SemaphoreType.DMA((2,))],
        compiler_params=_params(("arbitrary",)),
        name="moe_combine",
    )(dest_flat, dest_flat, x1, top_g, mod, yb)


def _final_kernel(x_ref, g_ref, o_ref):
    x = x_ref[...]
    ms = jnp.mean(x * x, axis=-1, keepdims=True)
    o_ref[...] = x * lax.rsqrt(ms + EPS) * g_ref[...]


def _final_norm(x, g, nb, tpb, S):
    N, D = x.shape
    tm = TOKEN_TILE
    lt = S // tm
    ct = tpb - lt
    out = pl.pallas_call(
        _final_kernel,
        grid=(nb, lt),
        in_specs=[pl.BlockSpec((tm, D), lambda b, i: (b * tpb + ct + i, 0)),
                  pl.BlockSpec((1, D), lambda b, i: (0, 0))],
        out_specs=pl.BlockSpec((tm, D), lambda b, i: (b * lt + i, 0)),
        out_shape=jax.ShapeDtypeStruct((nb * S, D), F32),
        compiler_params=_params(("arbitrary", "arbitrary")),
        name="final_norm",
    )(x, g.reshape(1, D))
    return out.reshape(nb, S, D)


def _rope_tables(S, L):
    half = GLA_DK // 2
    nfreq = half // 2
    inv_freq = ROPE_THETA ** (-jnp.arange(0, half, 2, dtype=F32) / half)
    t = jnp.arange(S)
    pos = jnp.stack([(t // GRID_W).astype(F32), (t % GRID_W).astype(F32)], axis=0)
    d = np.arange(GLA_QK) % GLA_DK
    axis = d // half
    e = d % half
    fidx = e % nfreq
    sign = np.where(e < nfreq, -1.0, 1.0).astype(np.float32)
    ang = pos[axis].T * inv_freq[fidx][None, :]
    cos = jnp.cos(ang)
    sin = jnp.sin(ang) * sign[None, :]
    cos = jnp.concatenate([jnp.ones((L, GLA_QK), F32), cos], axis=0)
    sin = jnp.concatenate([jnp.zeros((L, GLA_QK), F32), sin], axis=0)
    pad = jnp.zeros((L + S, QK_PAD - GLA_QK), F32)
    qs = GLA_DK ** -0.5
    cos_t = jnp.concatenate([cos * qs, pad, cos, pad], axis=1)
    sin_t = jnp.concatenate([sin * qs, pad, sin, pad], axis=1)
    return cos_t, sin_t


def _partner_perm():
    d = np.arange(GLA_QK) % GLA_DK
    e = d % (GLA_DK // 2)
    nfreq = GLA_DK // 4
    return np.where(e < nfreq, np.arange(GLA_QK) + nfreq, np.arange(GLA_QK) - nfreq)


def _layout_w_in(w_in):
    D = w_in.shape[0]
    o = np.cumsum([0, GLA_QK, GLA_QK, GLA_V, GLA_RANK, GLA_RANK, GLA_V, NA_W, NA_W, NA_W, POOL_WIDTH])
    piece = lambda n: w_in[:, o[n]:o[n + 1]]
    perm = _partner_perm()
    z = lambda n: jnp.zeros((D, n), w_in.dtype)
    qpad = z(QK_PAD - GLA_QK)
    cols = [piece(0), qpad, piece(1), qpad,
            piece(0)[:, perm], qpad, piece(1)[:, perm], qpad,
            piece(2), piece(5),
            piece(3), piece(4), z(LANE - 2 * GLA_RANK),
            piece(6), piece(7), piece(8), piece(9)]
    return jnp.concatenate(cols, axis=1).astype(BF16)


def _layout_gate(wa2, ba):
    w = jnp.zeros((2, LANE, GLA_QK), F32)
    w = w.at[0, 0:GLA_RANK].set(wa2[0]).at[1, GLA_RANK:2 * GLA_RANK].set(wa2[1])
    return w.astype(BF16), ba.reshape(2, 1, GLA_QK)


def _block_diag(pool_w):
    G = pool_w.shape[0]
    w = jnp.zeros((POOL_WIDTH, POOL_WIDTH), F32)
    for g in range(G):
        w = w.at[g * POOL_GROUP:(g + 1) * POOL_GROUP, g * POOL_GROUP:(g + 1) * POOL_GROUP].set(pool_w[g])
    return w.astype(BF16)


def _routing(top_e, rank, counts, n_blocks):
    bm = MOE_ROWS
    counts = counts[:, 0]
    padded = (counts + bm - 1) // bm * bm
    pend = jnp.cumsum(padded)
    pstart = pend - padded
    tm = top_e.shape[1]
    te = top_e.reshape(-1, SUB, tm)[:, :TOP_K]
    rk = rank.reshape(-1, SUB, tm)[:, :TOP_K]
    start_of = jnp.zeros_like(te)
    for e in range(N_EXPERTS):
        start_of = jnp.where(te == e, pstart[e], start_of)
    dest = jnp.transpose(start_of + rk, (0, 2, 1)).astype(jnp.int32).reshape(-1)
    first_row = jnp.arange(n_blocks, dtype=jnp.int32) * bm
    block_e = jnp.minimum(jnp.sum((pend[None, :] <= first_row[:, None]).astype(jnp.int32), axis=1),
                          N_EXPERTS - 1).astype(jnp.int32)
    n_valid = (pend[-1] // bm).astype(jnp.int32).reshape(1)
    pad_lo = (pstart + counts).astype(jnp.int32)
    return dest, block_e, n_valid, pad_lo, pend.astype(jnp.int32)


def kernel(x, c, ctx, c_ctx, norm1_g, norm2_g, w_ada, b_ada, w_in, gla_wa2, gla_ba, gla_norm_g, na_rpb,
           pool_w, pool_scale, w_out, router_w, router_b, w_gu, b_gu, w_down, b_down, final_g):
    B, S, D = x.shape
    L = ctx.shape[1]
    assert L == TOKEN_TILE and S % TOKEN_TILE == 0 and S % GRID_W == 0
    T = L + S
    tpb = T // TOKEN_TILE
    N = B * T
    A = N * TOP_K
    n_blocks = (A + MOE_ROWS - 1) // MOE_ROWS + N_EXPERTS

    xa = jnp.concatenate([ctx, x], axis=1).reshape(N, D)
    cond_rows = 8 * ((B + 1 + 7) // 8)
    cond = jnp.zeros((cond_rows, D), F32).at[:B].set(c).at[B].set(c_ctx)
    cos_t, sin_t = _rope_tables(S, L)

    def layer(xa, p):
        (n1, n2, wada, bada, win, wa2, ba, gng, rpb, pw, psc, wout, rw, rb, layer_idx) = p
        mod = _modulation(cond, wada, bada).reshape(cond_rows, 6, D)
        wa, bap = _layout_gate(wa2, ba)
        q, k, v, g, bf, bb, nq, nk, nv, pu = _input_projection(xa, mod, n1, _layout_w_in(win), cos_t, sin_t,
                                                               wa, bap, B, tpb)
        gla_o = _gla(q, k, v, g, bf, bb, gng, B, T, L)
        na_o = _neighbourhood_attention(nq, nk, nv, _na_bias_table(rpb, S // GRID_W), B, T, L)
        pool_o = _pool(pu, _block_diag(pw), psc, B, T, L)
        rwp = jnp.zeros((D, LANE), F32).at[:, :N_EXPERTS].set(rw)
        rbp = jnp.full((1, LANE), NEG, F32).at[0, :N_EXPERTS].set(rb)
        x1, h2, top_e, top_g, rank, counts = _output_projection(gla_o, na_o, pool_o, xa, mod, n2,
                                                                wout.astype(BF16), rwp, rbp, B, tpb)
        dest, block_e, n_valid, pad_lo, pad_hi = _routing(top_e, rank, counts, n_blocks)
        xs = _dispatch(pad_lo, pad_hi, n_valid, dest, h2, n_blocks * MOE_ROWS)
        yb = _moe_blocks(layer_idx.reshape(1), block_e, n_valid, xs, w_gu, b_gu, w_down, b_down)
        gates = jnp.transpose(top_g.reshape(-1, SUB, TOKEN_TILE), (0, 2, 1)).reshape(N, SUB)
        return _combine(dest, x1, gates, yb, mod, B, tpb), None

    params = (norm1_g, norm2_g, w_ada, b_ada, w_in, gla_wa2, gla_ba, gla_norm_g, na_rpb, pool_w, pool_scale,
              w_out, router_w, router_b, jnp.arange(w_gu.shape[0], dtype=jnp.int32))
    xa, _ = lax.scan(layer, xa, params)
    return _final_norm(xa, final_g, B, tpb, S)
```

```python
import functools

import numpy as np
import jax
import jax.numpy as jnp
from jax import lax
from jax.experimental import pallas as pl
from jax.experimental.pallas import tpu as pltpu

F32 = jnp.float32
BF16 = jnp.bfloat16

EPS = 1e-6
ROPE_THETA = 10000.0
GRID_W = 64

GLA_HEADS = 4
GLA_DK = 48
GLA_DV = 96
GLA_QK = GLA_HEADS * GLA_DK
GLA_V = GLA_HEADS * GLA_DV
GLA_RANK = 16
GLA_TAU = 16.0
GLA_CHUNK = 64

NA_HEADS = 6
NA_DH = 64
NA_W = NA_HEADS * NA_DH
NA_KH = 8
NA_KW = 16

POOL_WINDOWS = (2, 4, 8, 16)
POOL_WIDTH = 256
POOL_GROUP = 64
POOL_HALO = 8

N_EXPERTS = 32
TOP_K = 4
SWIGLU_LIMIT = 7.0
SWIGLU_ALPHA = 1.702

LANE = 128
TOKEN_TILE = 256
MOE_ROWS = 512
NEG = -1e30
VMEM_LIMIT = 56 * 1024 * 1024

QK_PAD = 256
C_QK = 0
C_SW = C_QK + 2 * QK_PAD
C_V = C_SW + 2 * QK_PAD
C_G = C_V + GLA_V
C_A = C_G + GLA_V
C_NQ = C_A + LANE
C_NK = C_NQ + NA_W
C_NV = C_NK + NA_W
C_PU = C_NV + NA_W
C_END = C_PU + POOL_WIDTH


def _dot(a, b):
    return jnp.dot(a, b, preferred_element_type=F32)


def _dot_nt(a, b):
    return lax.dot_general(a, b, (((1,), (1,)), ((), ())), preferred_element_type=F32)


def _split(x):
    hi = x.astype(BF16)
    lo = (x - hi.astype(F32)).astype(BF16)
    return hi, lo


def _sigmoid(x):
    return 1.0 / (1.0 + jnp.exp(-x))


def _params(sem):
    return pltpu.CompilerParams(dimension_semantics=sem, vmem_limit_bytes=VMEM_LIMIT)


def _mod_kernel(c_ref, w_ref, b_ref, o_ref):
    c = c_ref[...]
    s = c * _sigmoid(c)
    sh, sl = _split(s)
    wh, wl = _split(w_ref[...])
    o_ref[...] = _dot(sh, wh) + _dot(sl, wh) + _dot(sh, wl) + b_ref[...]


def _modulation(cond, w_ada, b_ada):
    R, D = cond.shape
    N = w_ada.shape[1]
    bn = 1024
    return pl.pallas_call(
        _mod_kernel,
        grid=(N // bn,),
        in_specs=[pl.BlockSpec((R, D), lambda j: (0, 0)),
                  pl.BlockSpec((D, bn), lambda j: (0, j)),
                  pl.BlockSpec((1, bn), lambda j: (0, j))],
        out_specs=pl.BlockSpec((R, bn), lambda j: (0, j)),
        out_shape=jax.ShapeDtypeStruct((R, N), F32),
        compiler_params=_params(("arbitrary",)),
        name="adaln_mod",
    )(cond, w_ada, b_ada.reshape(1, N))


def _inproj_kernel(x_ref, mod_ref, g_ref, w_ref, cos_ref, sin_ref, wa_ref, ba_ref, tri_ref,
                   q_ref, k_ref, v_ref, gg_ref, bf_ref, bb_ref, nq_ref, nk_ref, nv_ref, pu_ref):
    x = x_ref[...]
    ms = jnp.mean(x * x, axis=-1, keepdims=True)
    h = x * lax.rsqrt(ms + EPS) * g_ref[...]
    h = h * (1.0 + mod_ref[1:2, :]) + mod_ref[0:1, :]
    hb = h.astype(BF16)

    def proj(lo, hi):
        return _dot(hb, w_ref[:, lo:hi])

    rot = proj(C_QK, C_SW) * cos_ref[...] + proj(C_SW, C_V) * sin_ref[...]
    q_ref[...] = rot[:, 0:GLA_QK].astype(BF16)
    k_ref[...] = rot[:, QK_PAD:QK_PAD + GLA_QK].astype(BF16)
    v_ref[...] = proj(C_V, C_G).astype(BF16)
    gg_ref[...] = proj(C_G, C_A).astype(BF16)
    a = proj(C_A, C_NQ).astype(BF16)
    for d, out_ref in enumerate((bf_ref, bb_ref)):
        z = _dot(a, wa_ref[d]) + ba_ref[d]
        la = (jnp.minimum(z, 0.0) - jnp.log(1.0 + jnp.exp(-jnp.abs(z)))) * (1.0 / GLA_TAU)
        lh, ll = _split(la)
        out_ref[...] = _dot(tri_ref[d], lh) + _dot(tri_ref[d], ll)
    nq_ref[...] = (proj(C_NQ, C_NK) * (NA_DH ** -0.5)).astype(BF16)
    nk_ref[...] = proj(C_NK, C_NV).astype(BF16)
    nv_ref[...] = proj(C_NV, C_PU).astype(BF16)
    pu_ref[...] = proj(C_PU, C_END)


def _chunk_triangles():
    r = np.arange(TOKEN_TILE)[:, None]
    c = np.arange(TOKEN_TILE)[None, :]
    same = (r // GLA_CHUNK) == (c // GLA_CHUNK)
    return jnp.asarray(np.stack([same & (c <= r), same & (c >= r)]).astype(np.float32), dtype=BF16)


def _input_projection(x, mod, norm_g, w, cos_t, sin_t, wa, ba, nb, tpb):
    N, D = x.shape
    tm = TOKEN_TILE

    def mod_idx(i):
        return (jnp.where(i % tpb == 0, nb, i // tpb), 0, 0)

    tok = lambda w_: pl.BlockSpec((tm, w_), lambda i: (i, 0))
    outs = [(GLA_QK, BF16), (GLA_QK, BF16), (GLA_V, BF16), (GLA_V, BF16), (GLA_QK, F32), (GLA_QK, F32),
            (NA_W, BF16), (NA_W, BF16), (NA_W, BF16), (POOL_WIDTH, F32)]
    return pl.pallas_call(
        _inproj_kernel,
        grid=(N // tm,),
        in_specs=[tok(D),
                  pl.BlockSpec((None, 6, D), mod_idx),
                  pl.BlockSpec((1, D), lambda i: (0, 0)),
                  pl.BlockSpec((D, C_END), lambda i: (0, 0)),
                  pl.BlockSpec((tm, 2 * QK_PAD), lambda i: (i % tpb, 0)),
                  pl.BlockSpec((tm, 2 * QK_PAD), lambda i: (i % tpb, 0)),
                  pl.BlockSpec((2, LANE, GLA_QK), lambda i: (0, 0, 0)),
                  pl.BlockSpec((2, 1, GLA_QK), lambda i: (0, 0, 0)),
                  pl.BlockSpec((2, tm, tm), lambda i: (0, 0, 0))],
        out_specs=[tok(w_) for w_, _ in outs],
        out_shape=[jax.ShapeDtypeStruct((N, w_), dt) for w_, dt in outs],
        compiler_params=_params(("arbitrary",)),
        name="in_proj",
    )(x, mod, norm_g.reshape(1, D), w, cos_t, sin_t, wa, ba, _chunk_triangles())


def _lane_head(idx, width):
    h = jnp.zeros_like(idx)
    for t in range(1, GLA_HEADS):
        h = h + (idx >= t * width).astype(jnp.int32)
    return h


def _gla_kernel(q_ref, k_ref, v_ref, g_ref, bf_ref, bb_ref, ng_ref, o_ref,
                stf_ref, stb_ref, of_ref, ob_ref, *, n_ctx, n_all):
    assert n_all % 2 == 0
    C = GLA_CHUNK
    i32 = jnp.int32
    kmask = (lax.broadcasted_iota(i32, (GLA_HEADS * C, GLA_QK), 0) // C
             == _lane_head(lax.broadcasted_iota(i32, (GLA_HEADS * C, GLA_QK), 1), GLA_DK))
    vmask = (lax.broadcasted_iota(i32, (GLA_HEADS * C, GLA_V), 0) // C
             == _lane_head(lax.broadcasted_iota(i32, (GLA_HEADS * C, GLA_V), 1), GLA_DV))
    smask = (_lane_head(lax.broadcasted_iota(i32, (GLA_V, GLA_QK), 0), GLA_DV)
             == _lane_head(lax.broadcasted_iota(i32, (GLA_V, GLA_QK), 1), GLA_DK))
    hind = jnp.where(_lane_head(lax.broadcasted_iota(i32, (GLA_V, GLA_V), 0), GLA_DV)
                     == _lane_head(lax.broadcasted_iota(i32, (GLA_V, GLA_V), 1), GLA_DV),
                     1.0, 0.0).astype(BF16)
    qi_ = lax.broadcasted_iota(i32, (C, GLA_HEADS * C), 0)
    kj_ = lax.broadcasted_iota(i32, (C, GLA_HEADS * C), 1) % C
    causal_f = qi_ >= kj_
    causal_b = qi_ <= kj_

    def chunk(ci, fwd):
        st_ref = stf_ref if fwd else stb_ref
        r0 = pl.multiple_of(ci * C, C)
        rows = pl.ds(r0, C)
        q = q_ref[rows, :].astype(F32)
        k = k_ref[rows, :].astype(F32)
        v = v_ref[rows, :]
        b = (bf_ref if fwd else bb_ref)[rows, :]
        if fwd:
            bmid = b[C // 2 - 1:C // 2, :]
            btot = b[C - 1:C, :]
        else:
            bmid = b[C // 2:C // 2 + 1, :]
            btot = b[0:1, :]
        qe = (q * jnp.exp(b - bmid)).astype(BF16)
        ke = (k * jnp.exp(bmid - b)).astype(BF16)
        qs = (q * jnp.exp(b)).astype(BF16)
        kt = (k * jnp.exp(btot - b)).astype(BF16)
        zero = jnp.zeros((), BF16)
        kbd = jnp.where(kmask, jnp.concatenate([ke] * GLA_HEADS, axis=0), zero)
        vbd = jnp.where(vmask, jnp.concatenate([v] * GLA_HEADS, axis=0), zero)
        sc = _dot_nt(qe, kbd)
        sc = jnp.where(causal_f if fwd else causal_b, sc, 0.0)
        st = st_ref[...]
        o = _dot(sc.astype(BF16), vbd) + _dot_nt(qs, st.astype(BF16))
        vt = v.astype(F32).T.astype(BF16)
        ds = _dot(vt, kt)
        st_ref[...] = st * jnp.exp(btot) + jnp.where(smask, ds, 0.0)
        (of_ref if fwd else ob_ref)[rows, :] = o

    def finish(ci):
        rows = pl.ds(pl.multiple_of(ci * C, C), C)
        tot = of_ref[rows, :] + ob_ref[rows, :]
        th, tl = _split(tot * tot)
        ms = (_dot(th, hind) + _dot(tl, hind)) * (1.0 / GLA_DV)
        g = g_ref[rows, :].astype(F32)
        y = tot * lax.rsqrt(ms + EPS) * ng_ref[...] * (g * _sigmoid(g))
        o_ref[rows, :] = y.astype(o_ref.dtype)

    def both(f0, b0, count):
        def body(i, carry):
            chunk(f0 + i, True)
            chunk(b0 - i, False)
            return carry
        lax.fori_loop(0, count, body, 0)

    stf_ref[...] = jnp.zeros_like(stf_ref)
    stb_ref[...] = jnp.zeros_like(stb_ref)
    both(0, n_ctx - 1, n_ctx)
    both(n_ctx, n_all - 1, n_all - n_ctx)

    def fin_body(i, carry):
        finish(2 * i)
        finish(2 * i + 1)
        return carry
    lax.fori_loop(0, n_all // 2, fin_body, 0)


def _gla(q, k, v, g, bf, bb, norm_g, nb, T, L):
    seq = lambda w_: pl.BlockSpec((T, w_), lambda b: (b, 0))
    once = lambda w_: pl.BlockSpec((T, w_), lambda b: (b, 0), pipeline_mode=pl.Buffered(1))
    kern = functools.partial(_gla_kernel, n_ctx=L // GLA_CHUNK, n_all=T // GLA_CHUNK)
    return pl.pallas_call(
        kern,
        grid=(nb,),
        in_specs=[seq(GLA_QK), seq(GLA_QK), seq(GLA_V), seq(GLA_V), once(GLA_QK), once(GLA_QK),
                  pl.BlockSpec((1, GLA_V), lambda b: (0, 0))],
        out_specs=seq(GLA_V),
        out_shape=jax.ShapeDtypeStruct((nb * T, GLA_V), BF16),
        scratch_shapes=[pltpu.VMEM((GLA_V, GLA_QK), F32), pltpu.VMEM((GLA_V, GLA_QK), F32),
                        pltpu.VMEM((T, GLA_V), F32), pltpu.VMEM((T, GLA_V), F32)],
        compiler_params=_params(("arbitrary",)),
        name="gla",
    )(q, k, v, g, bf, bb, norm_g.reshape(1, GLA_V))


NA_GROUP = TOKEN_TILE // GRID_W
NA_WIN = NA_KH + NA_GROUP - 1


def _na_window_start(group, n_rows):
    return jnp.clip(group * NA_GROUP - NA_KH // 2, 0, n_rows - NA_WIN)


def _na_kernel(q_ref, k_ref, v_ref, bias_ref, o_ref, *, n_rows, L):
    j = pl.program_id(1)
    W = GRID_W
    Q = q_ref.shape[0]
    lane = lax.broadcasted_iota(jnp.int32, (2 * Q, LANE), 1)
    row = lax.broadcasted_iota(jnp.int32, (2 * Q, LANE), 0)
    qmask = (row < Q) == (lane < NA_DH)
    out_lo = lax.broadcasted_iota(jnp.int32, (Q, LANE), 1) < NA_DH
    zero = jnp.zeros((), BF16)

    def pair_q(p):
        q2 = q_ref[:, p * LANE:(p + 1) * LANE]
        return jnp.where(qmask, jnp.concatenate([q2, q2], axis=0), zero)

    def finish(p, acc, l):
        o2 = acc / l
        o_ref[:, p * LANE:(p + 1) * LANE] = jnp.where(out_lo, o2[:Q], o2[Q:]).astype(o_ref.dtype)

    @pl.when(j == 0)
    def _():
        for p in range(NA_HEADS // 2):
            cols = slice(p * LANE, (p + 1) * LANE)
            s = _dot_nt(pair_q(p), k_ref[0:L, cols])
            m = jnp.max(s, axis=-1, keepdims=True)
            e = jnp.exp(s - m)
            l = jnp.sum(e, axis=-1, keepdims=True)
            finish(p, _dot(e.astype(BF16), v_ref[0:L, cols]), l)

    @pl.when(j > 0)
    def _():
        k0 = pl.multiple_of(L + _na_window_start(j - 1, n_rows) * W, W)
        win = pl.ds(k0, NA_WIN * W)
        for p in range(NA_HEADS // 2):
            cols = slice(p * LANE, (p + 1) * LANE)
            qb = pair_q(p)
            s_w = _dot_nt(qb, k_ref[win, cols]) + bias_ref[p]
            s_c = _dot_nt(qb, k_ref[0:L, cols])
            m = jnp.maximum(jnp.max(s_w, axis=-1, keepdims=True), jnp.max(s_c, axis=-1, keepdims=True))
            e_w = jnp.exp(s_w - m)
            e_c = jnp.exp(s_c - m)
            l = jnp.sum(e_w, axis=-1, keepdims=True) + jnp.sum(e_c, axis=-1, keepdims=True)
            acc = _dot(e_w.astype(BF16), v_ref[win, cols]) + _dot(e_c.astype(BF16), v_ref[0:L, cols])
            finish(p, acc, l)


def _na_bias_table(rpb, n_rows):
    W, G = GRID_W, NA_GROUP
    n_groups = n_rows // G
    assert n_rows >= NA_WIN and n_groups >= 3
    qc = np.arange(W)[:, None]
    kc = np.arange(W)[None, :]
    cs = np.clip(qc - NA_KW // 2, 0, W - NA_KW)
    cvalid = (kc >= cs) & (kc < cs + NA_KW)
    cidx = np.clip(kc - qc + NA_KW - 1, 0, 2 * NA_KW - 2)
    rvalid, ridx = [], []
    for grp in (0, 1, n_groups - 1):
        ws = int(np.clip(grp * G - NA_KH // 2, 0, n_rows - NA_WIN))
        r = (grp * G + np.arange(G))[:, None]
        rs = np.clip(r - NA_KH // 2, 0, n_rows - NA_KH)
        key_row = (ws + np.arange(NA_WIN))[None, :]
        rvalid.append((key_row >= rs) & (key_row < rs + NA_KH))
        ridx.append(np.clip(key_row - r + NA_KH - 1, 0, 2 * NA_KH - 2))
    e_r = jnp.asarray(np.eye(2 * NA_KH - 1, dtype=np.float32)[np.stack(ridx)])
    e_c = jnp.asarray(np.eye(2 * NA_KW - 1, dtype=np.float32)[cidx])
    t = jnp.einsum('vria,hab,qkb->vhrqik', e_r, rpb.astype(F32), e_c, precision=lax.Precision.HIGHEST)
    valid = np.stack(rvalid)[:, None, :, None, :, None] & cvalid[None, None, None, :, None, :]
    t = jnp.where(jnp.asarray(valid), t, NEG)
    return t.reshape(3, NA_HEADS // 2, 2 * G * W, NA_WIN * W)


def _neighbourhood_attention(nq, nk, nv, bias, nb, T, L):
    W = GRID_W
    n_rows = (T - L) // W
    n_groups = n_rows // NA_GROUP
    Q = NA_GROUP * W
    spb = T // Q

    def bias_idx(b, j):
        return (jnp.where(j <= 1, 0, jnp.where(j == n_groups, 2, 1)), 0, 0, 0)

    kern = functools.partial(_na_kernel, n_rows=n_rows, L=L)
    return pl.pallas_call(
        kern,
        grid=(nb, spb),
        in_specs=[pl.BlockSpec((Q, NA_W), lambda b, j: (b * spb + j, 0)),
                  pl.BlockSpec((T, NA_W), lambda b, j: (b, 0)),
                  pl.BlockSpec((T, NA_W), lambda b, j: (b, 0)),
                  pl.BlockSpec((None, NA_HEADS // 2, 2 * Q, NA_WIN * W), bias_idx)],
        out_specs=pl.BlockSpec((Q, NA_W), lambda b, j: (b * spb + j, 0)),
        out_shape=jax.ShapeDtypeStruct((nb * T, NA_W), BF16),
        compiler_params=_params(("arbitrary", "arbitrary")),
        name="neigh_attn",
    )(nq, nk, nv, bias)


def _pool_kernel(u_ref, w_ref, sc_ref, o_ref, *, T, L):
    i = pl.program_id(1)
    tm = TOKEN_TILE
    ext = tm + 2 * POOL_HALO
    t0 = i * tm
    e0 = pl.multiple_of(jnp.clip(t0 - POOL_HALO, 0, T - ext), 8)
    seg_lo = jnp.where(t0 < L, 0, L)
    seg_hi = jnp.where(t0 < L, L, T)
    uh, ul = _split(u_ref[pl.ds(e0, ext), :])
    t = t0 + lax.broadcasted_iota(jnp.int32, (tm, ext), 0)
    c = e0 + lax.broadcasted_iota(jnp.int32, (tm, ext), 1)
    tcol = t0 + lax.broadcasted_iota(jnp.int32, (tm, 1), 0)
    lane = lax.broadcasted_iota(jnp.int32, (tm, POOL_WIDTH), 1)
    y = jnp.zeros((tm, POOL_WIDTH), F32)
    for gi, w in enumerate(POOL_WINDOWS):
        lo = jnp.maximum(t - w // 2, seg_lo)
        hi = jnp.minimum(t + w - w // 2, seg_hi)
        band = jnp.where((c >= lo) & (c < hi), 1.0, 0.0).astype(BF16)
        cnt = (jnp.minimum(tcol + w - w // 2, seg_hi) - jnp.maximum(tcol - w // 2, seg_lo)).astype(F32)
        s = (_dot(band, uh) + _dot(band, ul)) / cnt
        y = jnp.where((lane >= gi * POOL_GROUP) & (lane < (gi + 1) * POOL_GROUP), s, y)
    y = y - u_ref[pl.ds(pl.multiple_of(t0, tm), tm), :]
    o_ref[...] = (_dot(y.astype(BF16), w_ref[...]) * sc_ref[...]).astype(o_ref.dtype)


def _pool(u, w_bd, scale, nb, T, L):
    tpb = T // TOKEN_TILE
    kern = functools.partial(_pool_kernel, T=T, L=L)
    return pl.pallas_call(
        kern,
        grid=(nb, tpb),
        in_specs=[pl.BlockSpec((T, POOL_WIDTH), lambda b, i: (b, 0)),
                  pl.BlockSpec((POOL_WIDTH, POOL_WIDTH), lambda b, i: (0, 0)),
                  pl.BlockSpec((1, POOL_WIDTH), lambda b, i: (0, 0))],
        out_specs=pl.BlockSpec((TOKEN_TILE, POOL_WIDTH), lambda b, i: (b * tpb + i, 0)),
        out_shape=jax.ShapeDtypeStruct((nb * T, POOL_WIDTH), BF16),
        compiler_params=_params(("arbitrary", "arbitrary")),
        name="pool_mix",
    )(u, w_bd, scale.reshape(1, POOL_WIDTH))


def _outproj_kernel(gla_ref, na_ref, pm_ref, x_ref, mod_ref, g_ref, wo_ref, rw_ref, rb_ref,
                    x1_ref, h2_ref, te_ref, tg_ref, rk_ref, cnt_ref, carry_ref):
    @pl.when(pl.program_id(0) == 0)
    def _():
        carry_ref[...] = jnp.zeros_like(carry_ref)

    mx = (_dot(gla_ref[...], wo_ref[0:GLA_V, :])
          + _dot(na_ref[...], wo_ref[GLA_V:GLA_V + NA_W, :])
          + _dot(pm_ref[...], wo_ref[GLA_V + NA_W:, :]))
    x1 = x_ref[...] + mod_ref[2:3, :] * mx
    x1_ref[...] = x1
    ms = jnp.mean(x1 * x1, axis=-1, keepdims=True)
    h = x1 * lax.rsqrt(ms + EPS) * g_ref[...]
    h = h * (1.0 + mod_ref[4:5, :]) + mod_ref[3:4, :]
    half = h.shape[1] // 2
    hr = h.astype(BF16).astype(F32)
    h2_ref[...] = jnp.bitwise_or(pltpu.bitcast(hr[:, :half], jnp.uint32),
                                 lax.shift_right_logical(pltpu.bitcast(hr[:, half:], jnp.uint32), jnp.uint32(16)))
    hh, hl = _split(h)
    wh, wl = _split(rw_ref[...])
    logits = _dot(hh, wh) + _dot(hl, wh) + _dot(hh, wl) + rb_ref[...]
    tm = logits.shape[0]
    E = N_EXPERTS
    lt = logits.T[0:E, :]
    erow = lax.broadcasted_iota(jnp.int32, (E, tm), 0)
    cur = lt
    vals, idxs = [], []
    for kk in range(TOP_K):
        m = jnp.max(cur, axis=0, keepdims=True)
        idx = jnp.min(jnp.where(cur == m, erow, E), axis=0, keepdims=True)
        vals.append(m)
        idxs.append(idx)
        cur = jnp.where(erow == idx, NEG, cur)
    expv = [jnp.exp(v - vals[0]) for v in vals]
    den = sum(expv)
    hot = sum(jnp.where(erow == idx, 1.0, 0.0) for idx in idxs)
    before = (lax.broadcasted_iota(jnp.int32, (tm, tm), 0) < lax.broadcasted_iota(jnp.int32, (tm, tm), 1))
    base = _dot(hot.astype(BF16), jnp.where(before, 1.0, 0.0).astype(BF16)) + carry_ref[...]
    krow = lax.broadcasted_iota(jnp.int32, (SUB, tm), 0)
    te = jnp.zeros((SUB, tm), jnp.int32)
    rk = jnp.zeros((SUB, tm), jnp.int32)
    tv = jnp.zeros((SUB, tm), F32)
    for kk in range(TOP_K):
        r = jnp.sum(jnp.where(erow == idxs[kk], base, 0.0), axis=0, keepdims=True)
        te = jnp.where(krow == kk, idxs[kk], te)
        rk = jnp.where(krow == kk, r.astype(jnp.int32), rk)
        tv = jnp.where(krow == kk, expv[kk] / den, tv)
    te_ref[...] = te
    rk_ref[...] = rk
    tg_ref[...] = tv
    carry_ref[...] = carry_ref[...] + jnp.sum(hot, axis=1, keepdims=True)
    cnt_ref[...] = jnp.broadcast_to(carry_ref[...], cnt_ref.shape).astype(jnp.int32)


def _output_projection(gla, na, pm, x, mod, norm_g, w_out, rw, rb, nb, tpb):
    N, D = x.shape
    tm = TOKEN_TILE

    def mod_idx(i):
        return (jnp.where(i % tpb == 0, nb, i // tpb), 0, 0)

    tok = lambda w_: pl.BlockSpec((tm, w_), lambda i: (i, 0))
    n_tiles = N // tm
    route = pl.BlockSpec((SUB, tm), lambda i: (i, 0))
    return pl.pallas_call(
        _outproj_kernel,
        grid=(n_tiles,),
        in_specs=[tok(GLA_V), tok(NA_W), tok(POOL_WIDTH), tok(D),
                  pl.BlockSpec((None, 6, D), mod_idx),
                  pl.BlockSpec((1, D), lambda i: (0, 0)),
                  pl.BlockSpec((D, D), lambda i: (0, 0)),
                  pl.BlockSpec((D, LANE), lambda i: (0, 0)),
                  pl.BlockSpec((1, LANE), lambda i: (0, 0))],
        out_specs=[tok(D), tok(D // 2), route, route, route,
                   pl.BlockSpec((N_EXPERTS, LANE), lambda i: (0, 0))],
        out_shape=[jax.ShapeDtypeStruct((N, D), F32), jax.ShapeDtypeStruct((N, D // 2), jnp.uint32),
                   jax.ShapeDtypeStruct((n_tiles * SUB, tm), jnp.int32),
                   jax.ShapeDtypeStruct((n_tiles * SUB, tm), F32),
                   jax.ShapeDtypeStruct((n_tiles * SUB, tm), jnp.int32),
                   jax.ShapeDtypeStruct((N_EXPERTS, LANE), jnp.int32)],
        scratch_shapes=[pltpu.VMEM((N_EXPERTS, 1), F32)],
        compiler_params=_params(("arbitrary",)),
        name="out_proj_router",
    )(gla, na, pm, x, mod, norm_g.reshape(1, D), w_out, rw, rb)


SUB = 8


def _dispatch_kernel(plo_ref, phi_ref, nv_ref, dest_ref, h_ref, xs_ref, zrow, zblk, sem, zsem, *, n_tiles):
    tm = h_ref.shape[0]
    for t in range(tm):
        for kk in range(TOP_K):
            d = dest_ref[t * TOP_K + kk]
            pltpu.make_async_copy(h_ref.at[pl.ds(t, 1)], xs_ref.at[pl.ds(d, 1)], sem).start(priority=kk % 2)
    for kk in range(TOP_K):
        pltpu.make_async_copy(h_ref, xs_ref.at[pl.ds(0, tm)], sem).wait()

    @pl.when(pl.program_id(0) == n_tiles - 1)
    def _():
        zrow[...] = jnp.zeros_like(zrow)

        def pad_copy(r):
            return pltpu.make_async_copy(zrow, xs_ref.at[pl.ds(r, 1)], zsem)

        zblk[...] = jnp.zeros_like(zblk)

        def pad_copy8(r8):
            return pltpu.make_async_copy(zblk.at[pl.ds(0, SUB)], xs_ref.at[pl.ds(pl.multiple_of(r8, SUB), SUB)], zsem)

        def per_expert(e, carry):
            lo = plo_ref[e]
            hi = phi_ref[e]
            mid = jnp.minimum(lax.shift_left(lax.shift_right_logical(lo + (SUB - 1), 3), 3), hi)
            pieces = lax.shift_right_logical(hi - mid, 3)

            def start(r, c):
                pad_copy(r).start()
                return c

            def wait(r, c):
                pad_copy(r).wait()
                return c

            def start8(j, c):
                pad_copy8(mid + j * SUB).start()
                return c

            def wait8(j, c):
                pad_copy8(mid + j * SUB).wait()
                return c

            lax.fori_loop(lo, mid, start, 0)
            lax.fori_loop(0, pieces, start8, 0)
            lax.fori_loop(lo, mid, wait, 0)
            lax.fori_loop(0, pieces, wait8, 0)
            return carry

        lax.fori_loop(0, N_EXPERTS, per_expert, 0)

        bg = zblk.shape[0]

        def blk_copy(b):
            return pltpu.make_async_copy(zblk, xs_ref.at[pl.ds(pl.multiple_of(b * bg, SUB), bg)], zsem)

        def blk_start(b, c):
            blk_copy(b).start()
            return c

        def blk_wait(b, c):
            blk_copy(b).wait()
            return c

        n_blocks = xs_ref.shape[0] // bg
        lax.fori_loop(nv_ref[0], n_blocks, blk_start, 0)
        lax.fori_loop(nv_ref[0], n_blocks, blk_wait, 0)


def _dispatch(pad_lo, pad_hi, n_valid, dest_flat, h2, n_slots):
    N, D = h2.shape
    tm = TOKEN_TILE
    n_tiles = N // tm
    grid_spec = pltpu.PrefetchScalarGridSpec(
        num_scalar_prefetch=3,
        grid=(n_tiles,),
        in_specs=[pl.BlockSpec((tm * TOP_K,), lambda i, lo, hi, nv: (i,), memory_space=pltpu.SMEM),
                  pl.BlockSpec((tm, D), lambda i, lo, hi, nv: (i, 0))],
        out_specs=pl.BlockSpec(memory_space=pl.ANY),
        scratch_shapes=[pltpu.VMEM((1, D), h2.dtype), pltpu.VMEM((MOE_ROWS, D), h2.dtype),
                        pltpu.SemaphoreType.DMA(()), pltpu.SemaphoreType.DMA(())],
    )
    return pl.pallas_call(
        functools.partial(_dispatch_kernel, n_tiles=n_tiles),
        grid_spec=grid_spec,
        out_shape=jax.ShapeDtypeStruct((n_slots, D), h2.dtype),
        compiler_params=_params(("arbitrary",)),
        name="moe_dispatch",
    )(pad_lo, pad_hi, n_valid, dest_flat, h2)


def _moe_kernel(li_ref, be_ref, nv_ref, xs_ref, wgu_ref, bgu_ref, wd_ref, bd_ref, y_ref, wgu_bf, wd_bf):
    del li_ref
    i = pl.program_id(0)
    valid = i < nv_ref[0]
    new_expert = jnp.logical_or(i == 0, be_ref[i] != be_ref[jnp.maximum(i - 1, 0)])

    @pl.when(jnp.logical_and(valid, new_expert))
    def _():
        wgu_bf[...] = wgu_ref[...].astype(BF16)
        wd_bf[...] = wd_ref[...].astype(BF16)

    @pl.when(valid)
    def _():
        dff = wd_ref.shape[0]
        words = xs_ref[...]
        half = words.shape[1]
        x_lo = pltpu.bitcast(jnp.bitwise_and(words, jnp.uint32(0xFFFF0000)), F32).astype(BF16)
        x_hi = pltpu.bitcast(lax.shift_left(words, jnp.uint32(16)), F32).astype(BF16)
        gu = _dot(x_lo, wgu_bf[0:half, :]) + _dot(x_hi, wgu_bf[half:, :]) + bgu_ref[...]
        g = jnp.minimum(gu[:, :dff], SWIGLU_LIMIT)
        u = jnp.clip(gu[:, dff:], -SWIGLU_LIMIT, SWIGLU_LIMIT)
        act = g * _sigmoid(SWIGLU_ALPHA * g) * (u + 1.0)
        y_ref[...] = _dot(act.astype(BF16), wd_bf[...]) + bd_ref[...]

    @pl.when(i >= nv_ref[0])
    def _():
        y_ref[...] = jnp.zeros_like(y_ref)


def _moe_blocks(layer_idx, block_e, n_valid, xs, w_gu, b_gu, w_down, b_down):
    P, row_words = xs.shape
    depth, E, D, F2 = w_gu.shape
    bm = MOE_ROWS
    expert = lambda i, li, be, nv: (li[0], be[i], 0, 0)
    grid_spec = pltpu.PrefetchScalarGridSpec(
        num_scalar_prefetch=3,
        grid=(P // bm,),
        in_specs=[pl.BlockSpec((bm, row_words), lambda i, li, be, nv: (i, 0)),
                  pl.BlockSpec((None, None, D, F2), expert),
                  pl.BlockSpec((None, None, 1, F2), expert),
                  pl.BlockSpec((None, None, F2 // 2, D), expert),
                  pl.BlockSpec((None, None, 1, D), expert)],
        out_specs=pl.BlockSpec((bm, D), lambda i, li, be, nv: (i, 0)),
        scratch_shapes=[pltpu.VMEM((D, F2), BF16), pltpu.VMEM((F2 // 2, D), BF16)],
    )
    return pl.pallas_call(
        _moe_kernel,
        grid_spec=grid_spec,
        out_shape=jax.ShapeDtypeStruct((P, D), F32),
        compiler_params=_params(("arbitrary",)),
        name="moe_experts",
    )(layer_idx, block_e, n_valid, xs, w_gu, b_gu.reshape(depth, E, 1, F2), w_down, b_down.reshape(depth, E, 1, D))


def _combine_kernel(dcur_ref, dnxt_ref, x_ref, tg_ref, mod_ref, yb_ref, o_ref, ybuf, sem, *, n_tiles):
    i = pl.program_id(0)
    tm = x_ref.shape[0]
    slot = i % 2

    def issue(dref, s):
        for t in range(tm):
            for kk in range(TOP_K):
                pltpu.make_async_copy(yb_ref.at[pl.ds(dref[t * TOP_K + kk], 1)], ybuf.at[s, kk, pl.ds(t, 1)],
                                      sem.at[s]).start(priority=kk % 2)

    @pl.when(i == 0)
    def _():
        issue(dcur_ref, 0)

    @pl.when(i + 1 < n_tiles)
    def _():
        issue(dnxt_ref, 1 - slot)

    for kk in range(TOP_K):
        pltpu.make_async_copy(yb_ref.at[pl.ds(0, tm)], ybuf.at[slot, kk], sem.at[slot]).wait()
    tg = tg_ref[...]
    rows = lambda kk: ybuf[slot, kk]
    y = tg[:, 0:1] * rows(0)
    for kk in range(1, TOP_K):
        y = y + tg[:, kk:kk + 1] * rows(kk)
    o_ref[...] = x_ref[...] + mod_ref[5:6, :] * y


def _combine(dest_flat, x1, top_g, yb, mod, nb, tpb):
    N, D = x1.shape
    tm = TOKEN_TILE
    n_tiles = N // tm

    def mod_idx(i):
        return (jnp.where(i % tpb == 0, nb, i // tpb), 0, 0)

    kern = functools.partial(_combine_kernel, n_tiles=n_tiles)
    return pl.pallas_call(
        kern,
        grid=(n_tiles,),
        in_specs=[pl.BlockSpec((tm * TOP_K,), lambda i: (i,), memory_space=pltpu.SMEM),
                  pl.BlockSpec((tm * TOP_K,), lambda i: (jnp.minimum(i + 1, n_tiles - 1),),
                               memory_space=pltpu.SMEM),
                  pl.BlockSpec((tm, D), lambda i: (i, 0)),
                  pl.BlockSpec((tm, SUB), lambda i: (i, 0)),
                  pl.BlockSpec((None, 6, D), mod_idx),
                  pl.BlockSpec(memory_space=pl.ANY)],
        out_specs=pl.BlockSpec((tm, D), lambda i: (i, 0)),
        out_shape=jax.ShapeDtypeStruct((N, D), F32),
        scratch_shapes=[pltpu.VMEM((2, TOP_K, tm, D), F32), pltpu.SemaphoreType.DMA((2,))],
        compiler_params=_params(("arbitrary",)),
        name="moe_combine",
    )(dest_flat, dest_flat, x1, top_g, mod, yb)


def _final_kernel(x_ref, g_ref, o_ref):
    x = x_ref[...]
    ms = jnp.mean(x * x, axis=-1, keepdims=True)
    o_ref[...] = x * lax.rsqrt(ms + EPS) * g_ref[...]


def _final_norm(x, g, nb, tpb, S):
    N, D = x.shape
    tm = TOKEN_TILE
    lt = S // tm
    ct = tpb - lt
    out = pl.pallas_call(
        _final_kernel,
        grid=(nb, lt),
        in_specs=[pl.BlockSpec((tm, D), lambda b, i: (b * tpb + ct + i, 0)),
                  pl.BlockSpec((1, D), lambda b, i: (0, 0))],
        out_specs=pl.BlockSpec((tm, D), lambda b, i: (b * lt + i, 0)),
        out_shape=jax.ShapeDtypeStruct((nb * S, D), F32),
        compiler_params=_params(("arbitrary", "arbitrary")),
        name="final_norm",
    )(x, g.reshape(1, D))
    return out.reshape(nb, S, D)


def _rope_tables(S, L):
    half = GLA_DK // 2
    nfreq = half // 2
    inv_freq = ROPE_THETA ** (-jnp.arange(0, half, 2, dtype=F32) / half)
    t = jnp.arange(S)
    pos = jnp.stack([(t // GRID_W).astype(F32), (t % GRID_W).astype(F32)], axis=0)
    d = np.arange(GLA_QK) % GLA_DK
    axis = d // half
    e = d % half
    fidx = e % nfreq
    sign = np.where(e < nfreq, -1.0, 1.0).astype(np.float32)
    ang = pos[axis].T * inv_freq[fidx][None, :]
    cos = jnp.cos(ang)
    sin = jnp.sin(ang) * sign[None, :]
    cos = jnp.concatenate([jnp.ones((L, GLA_QK), F32), cos], axis=0)
    sin = jnp.concatenate([jnp.zeros((L, GLA_QK), F32), sin], axis=0)
    pad = jnp.zeros((L + S, QK_PAD - GLA_QK), F32)
    qs = GLA_DK ** -0.5
    cos_t = jnp.concatenate([cos * qs, pad, cos, pad], axis=1)
    sin_t = jnp.concatenate([sin * qs, pad, sin, pad], axis=1)
    return cos_t, sin_t


def _partner_perm():
    d = np.arange(GLA_QK) % GLA_DK
    e = d % (GLA_DK // 2)
    nfreq = GLA_DK // 4
    return np.where(e < nfreq, np.arange(GLA_QK) + nfreq, np.arange(GLA_QK) - nfreq)


def _layout_w_in(w_in):
    D = w_in.shape[0]
    o = np.cumsum([0, GLA_QK, GLA_QK, GLA_V, GLA_RANK, GLA_RANK, GLA_V, NA_W, NA_W, NA_W, POOL_WIDTH])
    piece = lambda n: w_in[:, o[n]:o[n + 1]]
    perm = _partner_perm()
    z = lambda n: jnp.zeros((D, n), w_in.dtype)
    qpad = z(QK_PAD - GLA_QK)
    cols = [piece(0), qpad, piece(1), qpad,
            piece(0)[:, perm], qpad, piece(1)[:, perm], qpad,
            piece(2), piece(5),
            piece(3), piece(4), z(LANE - 2 * GLA_RANK),
            piece(6), piece(7), piece(8), piece(9)]
    return jnp.concatenate(cols, axis=1).astype(BF16)


def _layout_gate(wa2, ba):
    w = jnp.zeros((2, LANE, GLA_QK), F32)
    w = w.at[0, 0:GLA_RANK].set(wa2[0]).at[1, GLA_RANK:2 * GLA_RANK].set(wa2[1])
    return w.astype(BF16), ba.reshape(2, 1, GLA_QK)


def _block_diag(pool_w):
    G = pool_w.shape[0]
    w = jnp.zeros((POOL_WIDTH, POOL_WIDTH), F32)
    for g in range(G):
        w = w.at[g * POOL_GROUP:(g + 1) * POOL_GROUP, g * POOL_GROUP:(g + 1) * POOL_GROUP].set(pool_w[g])
    return w.astype(BF16)


def _routing(top_e, rank, counts, n_blocks):
    bm = MOE_ROWS
    counts = counts[:, 0]
    padded = (counts + bm - 1) // bm * bm
    pend = jnp.cumsum(padded)
    pstart = pend - padded
    tm = top_e.shape[1]
    te = top_e.reshape(-1, SUB, tm)[:, :TOP_K]
    rk = rank.reshape(-1, SUB, tm)[:, :TOP_K]
    start_of = jnp.zeros_like(te)
    for e in range(N_EXPERTS):
        start_of = jnp.where(te == e, pstart[e], start_of)
    dest = jnp.transpose(start_of + rk, (0, 2, 1)).astype(jnp.int32).reshape(-1)
    first_row = jnp.arange(n_blocks, dtype=jnp.int32) * bm
    block_e = jnp.minimum(jnp.sum((pend[None, :] <= first_row[:, None]).astype(jnp.int32), axis=1),
                          N_EXPERTS - 1).astype(jnp.int32)
    n_valid = (pend[-1] // bm).astype(jnp.int32).reshape(1)
    pad_lo = (pstart + counts).astype(jnp.int32)
    return dest, block_e, n_valid, pad_lo, pend.astype(jnp.int32)


def kernel(x, c, ctx, c_ctx, norm1_g, norm2_g, w_ada, b_ada, w_in, gla_wa2, gla_ba, gla_norm_g, na_rpb,
           pool_w, pool_scale, w_out, router_w, router_b, w_gu, b_gu, w_down, b_down, final_g):
    B, S, D = x.shape
    L = ctx.shape[1]
    assert L == TOKEN_TILE and S % TOKEN_TILE == 0 and S % GRID_W == 0
    T = L + S
    tpb = T // TOKEN_TILE
    N = B * T
    A = N * TOP_K
    n_blocks = (A + MOE_ROWS - 1) // MOE_ROWS + N_EXPERTS

    xa = jnp.concatenate([ctx, x], axis=1).reshape(N, D)
    cond_rows = 8 * ((B + 1 + 7) // 8)
    cond = jnp.zeros((cond_rows, D), F32).at[:B].set(c).at[B].set(c_ctx)
    cos_t, sin_t = _rope_tables(S, L)

    def layer(xa, p):
        (n1, n2, wada, bada, win, wa2, ba, gng, rpb, pw, psc, wout, rw, rb, layer_idx) = p
        mod = _modulation(cond, wada, bada).reshape(cond_rows, 6, D)
        wa, bap = _layout_gate(wa2, ba)
        q, k, v, g, bf, bb, nq, nk, nv, pu = _input_projection(xa, mod, n1, _layout_w_in(win), cos_t, sin_t,
                                                               wa, bap, B, tpb)
        gla_o = _gla(q, k, v, g, bf, bb, gng, B, T, L)
        na_o = _neighbourhood_attention(nq, nk, nv, _na_bias_table(rpb, S // GRID_W), B, T, L)
        pool_o = _pool(pu, _block_diag(pw), psc, B, T, L)
        rwp = jnp.zeros((D, LANE), F32).at[:, :N_EXPERTS].set(rw)
        rbp = jnp.full((1, LANE), NEG, F32).at[0, :N_EXPERTS].set(rb)
        x1, h2, top_e, top_g, rank, counts = _output_projection(gla_o, na_o, pool_o, xa, mod, n2,
                                                                wout.astype(BF16), rwp, rbp, B, tpb)
        dest, block_e, n_valid, pad_lo, pad_hi = _routing(top_e, rank, counts, n_blocks)
        xs = _dispatch(pad_lo, pad_hi, n_valid, dest, h2, n_blocks * MOE_ROWS)
        yb = _moe_blocks(layer_idx.reshape(1), block_e, n_valid, xs, w_gu, b_gu, w_down, b_down)
        gates = jnp.transpose(top_g.reshape(-1, SUB, TOKEN_TILE), (0, 2, 1)).reshape(N, SUB)
        return _combine(dest, x1, gates, yb, mod, B, tpb), None

    params = (norm1_g, norm2_g, w_ada, b_ada, w_in, gla_wa2, gla_ba, gla_norm_g, na_rpb, pool_w, pool_scale,
              w_out, router_w, router_b, jnp.arange(w_gu.shape[0], dtype=jnp.int32))
    xa, _ = lax.scan(layer, xa, params)
    return _final_norm(xa, final_g, B, tpb, S)
```
